```python
import math, functools
import jax, jax.numpy as jnp
from jax import lax
import numpy as np

D_MODEL = 2048
BATCH = 2
SEQ = 8192
DEPTH = 1
DEC_BATCH = 32
DEC_SEQ = 4
PAST_LEN = 16384
PAGE_SIZE = 128

HD_ATT = 64
H_ATT = D_MODEL // 128
D_ATT = H_ATT * HD_ATT
BRANCHES = ((128, 1), (512, 4), (2048, 16))
WIN_MAX = 2048
SSM_HD = 64
D_SSM = D_MODEL
H_SSM = D_SSM // SSM_HD
SSM_GROUPS = 4
D_STATE = 128
CONV_W = 4
CHUNK = 128
CONV_DIM = D_SSM + 2 * SSM_GROUPS * D_STATE
D_MIX = D_ATT + D_SSM
D_IN = 3 * D_ATT + D_SSM + CONV_DIM + H_SSM
SPLITS = [D_ATT, 2 * D_ATT, 3 * D_ATT, 3 * D_ATT + D_SSM, 3 * D_ATT + D_SSM + CONV_DIM]
N_EXP = 32
TOP_K = 4
D_FF = D_MODEL
SWIGLU_LIMIT = 7.0
SWIGLU_ALPHA = 1.702
MOE_BLOCK = 128
EPS = 1e-5

kernel_name = "hymba_ssd_dilated_moe_step"


def rmsnorm(x, w):
    xf = x.astype(jnp.float32)
    y = xf * lax.rsqrt(jnp.mean(xf * xf, axis=-1, keepdims=True) + EPS)
    return (y * w.astype(jnp.float32)).astype(x.dtype)


def _branch_stats_prompt(q, k, v, window, dil):
    b, t, h, d = q.shape
    blk = window // dil
    span = dil * blk
    tp = -(-t // span) * span
    n_sub = tp // dil
    nb = n_sub // blk

    def to_sub(a):
        a = jnp.pad(a, ((0, 0), (0, tp - t), (0, 0), (0, 0)))
        a = a.reshape(b, n_sub, dil, h, d).transpose(0, 2, 1, 3, 4)
        return a.reshape(b, dil, nb, blk, h, d)

    def from_sub(a):
        rest = a.shape[4:]
        a = a.reshape((b, dil, n_sub) + rest)
        a = jnp.moveaxis(a, 1, 2).reshape((b, tp) + rest)
        return a[:, :t]

    def with_prev(a):
        prev = jnp.pad(a, ((0, 0), (0, 0), (1, 0), (0, 0), (0, 0), (0, 0)))[:, :, :-1]
        return jnp.concatenate([prev, a], axis=3)

    qs = to_sub(q)
    kc = with_prev(to_sub(k))
    vc = with_prev(to_sub(v))
    s = jnp.einsum('brnqhd,brnkhd->brnhqk', qs, kc,
                   preferred_element_type=jnp.float32) * (HD_ATT ** -0.5)
    qq = jnp.arange(blk)[:, None]
    cc = jnp.arange(2 * blk)[None, :]
    nn = jnp.arange(nb)[:, None, None]
    mask = (cc >= qq) & (cc <= qq + blk) & ((nn >= 1) | (cc >= blk))
    s = jnp.where(mask[None, None, :, None], s, -jnp.inf)
    m = jnp.max(s, axis=-1)
    p = jnp.exp(s - m[..., None])
    den = jnp.sum(p, axis=-1)
    num = jnp.einsum('brnhqk,brnkhd->brnqhd', p, vc.astype(jnp.float32))
    return (from_sub(jnp.swapaxes(m, 3, 4)), from_sub(jnp.swapaxes(den, 3, 4)),
            from_sub(num))


def _branch_stats_sample(q, k_all, v_all, n_buf, window, dil):
    dq = q.shape[1]
    steps = jnp.arange(window // dil + 1)
    idx = n_buf + jnp.arange(dq)[:, None] - dil * steps[None, :]
    valid = idx >= 0
    idx = jnp.maximum(idx, 0)
    kg = k_all[:, idx]
    vg = v_all[:, idx]
    s = jnp.einsum('bqhd,bqkhd->bqhk', q, kg,
                   preferred_element_type=jnp.float32) * (HD_ATT ** -0.5)
    s = jnp.where(valid[None, :, None, :], s, -jnp.inf)
    m = jnp.max(s, axis=-1)
    p = jnp.exp(s - m[..., None])
    den = jnp.sum(p, axis=-1)
    num = jnp.einsum('bqhk,bqkhd->bqhd', p, vg.astype(jnp.float32))
    return m, den, num


def _combine(stats):
    m = jnp.stack([st[0] for st in stats])
    den = jnp.stack([st[1] for st in stats])
    num = jnp.stack([st[2] for st in stats])
    wgt = jnp.exp(m - jnp.max(m, axis=0))
    return jnp.sum(wgt[..., None] * num, axis=0) / jnp.sum(wgt * den, axis=0)[..., None]


def dilated_attention_prompt(q, k, v):
    return _combine([_branch_stats_prompt(q, k, v, w, r) for (w, r) in BRANCHES])


def dilated_attention_sample(q, k, v, k_buf, v_buf):
    n_buf = k_buf.shape[1]
    k_all = jnp.concatenate([k_buf.astype(k.dtype), k], axis=1)
    v_all = jnp.concatenate([v_buf.astype(v.dtype), v], axis=1)
    return _combine([_branch_stats_sample(q, k_all, v_all, n_buf, w, r) for (w, r) in BRANCHES])


def ssd_scan(x, dt, a, bm, cm, h0):
    b, t, nh, p = x.shape
    g, n = bm.shape[2], bm.shape[3]
    e = nh // g
    cs = math.gcd(t, CHUNK)
    nc = t // cs
    x = x.reshape(b, nc, cs, g, e, p)
    dt = dt.reshape(b, nc, cs, g, e)
    bm = bm.reshape(b, nc, cs, g, n)
    cm = cm.reshape(b, nc, cs, g, n)
    acum = jnp.cumsum(jnp.moveaxis(dt * a.reshape(g, e), 2, -1), axis=-1)
    tri = jnp.tril(jnp.ones((cs, cs), dtype=bool))
    decay_in = jnp.exp(jnp.where(tri, acum[..., :, None] - acum[..., None, :], -jnp.inf))
    cb = jnp.einsum('bclgn,bcsgn->bcgls', cm, bm)
    y_diag = jnp.einsum('bcgls,bcgels,bcsge,bcsgep->bclgep', cb, decay_in, dt, x)
    decay_out = jnp.exp(acum[..., -1:] - acum)
    states = jnp.einsum('bcsgn,bcges,bcsge,bcsgep->bcgepn', bm, decay_out, dt, x)
    chunk_decay = jnp.exp(acum[..., -1])

    def step(h, inp):
        dec, st = inp
        return dec[..., None, None] * h + st, h

    h_last, h_prev = lax.scan(step, h0.reshape(b, g, e, p, n),
                              (jnp.moveaxis(chunk_decay, 1, 0), jnp.moveaxis(states, 1, 0)))
    h_prev = jnp.moveaxis(h_prev, 0, 1)
    y_off = jnp.einsum('bclgn,bcgepn,bcgel->bclgep', cm, h_prev, jnp.exp(acum))
    y = (y_diag + y_off).reshape(b, t, nh, p)
    return y, h_last.reshape(b, nh, p, n)


def ssd_mixer(z, xbc, dt_raw, conv_prev, h0, conv_w, conv_b, dt_bias, a_log, d_skip, norm_w):
    b, t, _ = xbc.shape
    xfull = jnp.concatenate([conv_prev.astype(xbc.dtype), xbc], axis=1)
    conv = lax.conv_general_dilated(xfull, conv_w[:, None, :].astype(xbc.dtype),
                                    window_strides=(1,), padding='VALID',
                                    dimension_numbers=('NWC', 'WIO', 'NWC'),
                                    feature_group_count=CONV_DIM)
    conv_new = xfull[:, t:]
    act = jax.nn.silu((conv + conv_b).astype(jnp.float32))
    xs, bm, cm = jnp.split(act, [D_SSM, D_SSM + SSM_GROUPS * D_STATE], axis=-1)
    xs = xs.reshape(b, t, H_SSM, SSM_HD)
    bm = bm.reshape(b, t, SSM_GROUPS, D_STATE)
    cm = cm.reshape(b, t, SSM_GROUPS, D_STATE)
    dt = jax.nn.softplus(dt_raw.astype(jnp.float32) + dt_bias.astype(jnp.float32))
    a = -jnp.exp(a_log.astype(jnp.float32))
    y, h_new = ssd_scan(xs, dt, a, bm, cm, h0.astype(jnp.float32))
    y = y + d_skip.astype(jnp.float32)[:, None] * xs
    y = y.reshape(b, t, D_SSM) * jax.nn.silu(z.astype(jnp.float32))
    yg = y.reshape(b, t, SSM_GROUPS, D_SSM // SSM_GROUPS)
    yg = yg * lax.rsqrt(jnp.mean(yg * yg, axis=-1, keepdims=True) + EPS)
    y = yg.reshape(b, t, D_SSM) * norm_w.astype(jnp.float32)
    return y, conv_new, h_new


def moe_ffn(h, w_router, b_router, w_gate, b_gate, w_up, b_up, w_down, b_down):
    b, t, d = h.shape
    xt = h.reshape(-1, d)
    nt = xt.shape[0]
    logits = xt.astype(jnp.float32) @ w_router.astype(jnp.float32) + b_router.astype(jnp.float32)
    top_val, top_idx = lax.top_k(logits, TOP_K)
    gates = jax.nn.softmax(top_val, axis=-1)
    n_assign = nt * TOP_K
    e_flat = top_idx.reshape(-1)
    tok_flat = jnp.arange(n_assign) // TOP_K
    g_flat = gates.reshape(-1)
    order = jnp.argsort(e_flat)
    e_s, tok_s, g_s = e_flat[order], tok_flat[order], g_flat[order]
    counts = jnp.bincount(e_flat, length=N_EXP)
    starts = jnp.cumsum(counts) - counts
    padded = (counts + MOE_BLOCK - 1) // MOE_BLOCK * MOE_BLOCK
    pends = jnp.cumsum(padded)
    pstarts = pends - padded
    dest = pstarts[e_s] + (jnp.arange(n_assign) - starts[e_s])
    n_blocks = -(-n_assign // MOE_BLOCK) + N_EXP
    cap = n_blocks * MOE_BLOCK
    x_disp = jnp.zeros((cap, d), h.dtype).at[dest].set(xt[tok_s])
    row_tok = jnp.full((cap,), nt, jnp.int32).at[dest].set(tok_s.astype(jnp.int32))
    row_gate = jnp.zeros((cap,), jnp.float32).at[dest].set(g_s)
    blk_start = jnp.arange(n_blocks) * MOE_BLOCK
    blk_exp = jnp.minimum(jnp.sum(pends[None, :] <= blk_start[:, None], axis=1), N_EXP - 1)

    def expert_block(args):
        xb, e = args
        gt = (xb @ w_gate[e] + b_gate[e]).astype(jnp.float32)
        up = (xb @ w_up[e] + b_up[e]).astype(jnp.float32)
        gt = jnp.minimum(gt, SWIGLU_LIMIT)
        up = jnp.clip(up, -SWIGLU_LIMIT, SWIGLU_LIMIT)
        act = (up + 1.0) * gt * jax.nn.sigmoid(SWIGLU_ALPHA * gt)
        return (act.astype(xb.dtype) @ w_down[e] + b_down[e]).astype(jnp.float32)

    y_disp = lax.map(expert_block, (x_disp.reshape(n_blocks, MOE_BLOCK, d), blk_exp))
    y_disp = y_disp.reshape(cap, d) * row_gate[:, None]
    y = jax.ops.segment_sum(y_disp, row_tok, num_segments=nt + 1)[:nt]
    return y.reshape(b, t, d).astype(h.dtype)


def hybrid_layer(x, conv_prev, h0, attn_fn, norm_mix_w, w_in, conv_w, conv_b, dt_bias, a_log,
                 d_skip, ssd_norm_w, w_out, norm_ffn_w, w_router, b_router, w_gate, b_gate,
                 w_up, b_up, w_down, b_down):
    b, t, _ = x.shape
    h = rmsnorm(x, norm_mix_w)
    proj = h @ w_in
    q, k, v, z, xbc, dt_raw = jnp.split(proj, SPLITS, axis=-1)
    q = q.reshape(b, t, H_ATT, HD_ATT)
    k = k.reshape(b, t, H_ATT, HD_ATT)
    v = v.reshape(b, t, H_ATT, HD_ATT)
    y_att = attn_fn(q, k, v).reshape(b, t, D_ATT)
    y_ssd, conv_new, h_new = ssd_mixer(z, xbc, dt_raw, conv_prev, h0, conv_w, conv_b,
                                       dt_bias, a_log, d_skip, ssd_norm_w)
    mix = jnp.concatenate([y_att, y_ssd], axis=-1).astype(x.dtype) @ w_out
    x = x + mix
    x = x + moe_ffn(rmsnorm(x, norm_ffn_w), w_router, b_router, w_gate, b_gate,
                    w_up, b_up, w_down, b_down)
    return x, k, v, conv_new, h_new


def setup_inputs(seed: int = 0) -> dict:
    key = jax.random.key(seed)
    ks = jax.random.split(key, 26)
    f32 = jnp.float32

    def nrm(k, shape, scale):
        return scale * jax.random.normal(k, shape, f32)

    win_buf = min(WIN_MAX, PAST_LEN)
    dt0 = jnp.exp(jax.random.uniform(ks[10], (DEPTH, H_SSM), f32, math.log(1e-3), math.log(1e-1)))
    return {
        "x_prompt": nrm(ks[0], (BATCH, SEQ, D_MODEL), 1.0),
        "x_sample": nrm(ks[1], (DEC_BATCH, DEC_SEQ, D_MODEL), 1.0),
        "cache_attn_k": nrm(ks[2], (DEPTH, DEC_BATCH, win_buf, H_ATT, HD_ATT), 1.0),
        "cache_attn_v": nrm(ks[3], (DEPTH, DEC_BATCH, win_buf, H_ATT, HD_ATT), 1.0),
        "state_conv": nrm(ks[4], (DEPTH, DEC_BATCH, CONV_W - 1, CONV_DIM), 1.0),
        "state_ssm": nrm(ks[5], (DEPTH, DEC_BATCH, H_SSM, SSM_HD, D_STATE), 0.5),
        "norm_mix_w": 1.0 + nrm(ks[6], (DEPTH, D_MODEL), 0.02),
        "w_in": nrm(ks[7], (DEPTH, D_MODEL, D_IN), D_MODEL ** -0.5),
        "conv_w": nrm(ks[8], (DEPTH, CONV_W, CONV_DIM), CONV_W ** -0.5),
        "conv_b": nrm(ks[9], (DEPTH, CONV_DIM), 0.02),
        "dt_bias": dt0 + jnp.log(-jnp.expm1(-dt0)),
        "a_log": jnp.log(jax.random.uniform(ks[11], (DEPTH, H_SSM), f32, 1.0, 16.0)),
        "d_skip": 1.0 + nrm(ks[12], (DEPTH, H_SSM), 0.1),
        "ssd_norm_w": 1.0 + nrm(ks[13], (DEPTH, D_SSM), 0.02),
        "w_out": nrm(ks[14], (DEPTH, D_MIX, D_MODEL), D_MIX ** -0.5),
        "norm_ffn_w": 1.0 + nrm(ks[15], (DEPTH, D_MODEL), 0.02),
        "w_router": nrm(ks[16], (DEPTH, D_MODEL, N_EXP), D_MODEL ** -0.5),
        "b_router": nrm(ks[17], (DEPTH, N_EXP), 0.01),
        "w_gate": nrm(ks[18], (DEPTH, N_EXP, D_MODEL, D_FF), D_MODEL ** -0.5),
        "b_gate": nrm(ks[19], (DEPTH, N_EXP, D_FF), 0.01),
        "w_up": nrm(ks[20], (DEPTH, N_EXP, D_MODEL, D_FF), D_MODEL ** -0.5),
        "b_up": nrm(ks[21], (DEPTH, N_EXP, D_FF), 0.01),
        "w_down": nrm(ks[22], (DEPTH, N_EXP, D_FF, D_MODEL), D_FF ** -0.5),
        "b_down": nrm(ks[23], (DEPTH, N_EXP, D_MODEL), 0.01),
        "norm_final_w": 1.0 + nrm(ks[24], (D_MODEL,), 0.02),
    }


def reference(x_prompt, x_sample, cache_attn_k, cache_attn_v, state_conv, state_ssm,
              norm_mix_w, w_in, conv_w, conv_b, dt_bias, a_log, d_skip, ssd_norm_w, w_out,
              norm_ffn_w, w_router, b_router, w_gate, b_gate, w_up, b_up, w_down, b_down,
              norm_final_w):
    xp, xs = x_prompt, x_sample
    bp = xp.shape[0]
    win_p = min(WIN_MAX, xp.shape[1])
    kp_l, vp_l, cp_l, hp_l = [], [], [], []
    ks_l, vs_l, cs_l, hs_l = [], [], [], []
    for l in range(DEPTH):
        params = (norm_mix_w[l], w_in[l], conv_w[l], conv_b[l], dt_bias[l], a_log[l], d_skip[l],
                  ssd_norm_w[l], w_out[l], norm_ffn_w[l], w_router[l], b_router[l], w_gate[l],
                  b_gate[l], w_up[l], b_up[l], w_down[l], b_down[l])
        conv0 = jnp.zeros((bp, CONV_W - 1, CONV_DIM), xp.dtype)
        h0 = jnp.zeros((bp, H_SSM, SSM_HD, D_STATE), jnp.float32)
        xp, kp, vp, cp, hp = hybrid_layer(xp, conv0, h0, dilated_attention_prompt, *params)
        kp_l.append(kp[:, -win_p:])
        vp_l.append(vp[:, -win_p:])
        cp_l.append(cp)
        hp_l.append(hp)
        attn_s = functools.partial(dilated_attention_sample,
                                   k_buf=cache_attn_k[l], v_buf=cache_attn_v[l])
        xs, k_s, v_s, c_s, h_s = hybrid_layer(xs, state_conv[l], state_ssm[l], attn_s, *params)
        ks_l.append(k_s)
        vs_l.append(v_s)
        cs_l.append(c_s)
        hs_l.append(h_s)
    y_prompt = rmsnorm(xp, norm_final_w)
    y_sample = rmsnorm(xs, norm_final_w)
    new_attn_k_prompt = jnp.stack(kp_l)
    new_attn_v_prompt = jnp.stack(vp_l)
    new_conv_prompt = jnp.stack(cp_l)
    new_ssm_prompt = jnp.stack(hp_l)
    new_attn_k_sample = jnp.stack(ks_l)
    new_attn_v_sample = jnp.stack(vs_l)
    new_conv_sample = jnp.stack(cs_l)
    new_ssm_sample = jnp.stack(hs_l)
    return (y_prompt, y_sample, new_attn_k_prompt, new_attn_v_prompt, new_conv_prompt,
            new_ssm_prompt, new_attn_k_sample, new_attn_v_sample, new_conv_sample,
            new_ssm_sample)
```

```python
import functools

import jax
import jax.numpy as jnp
from jax import lax
from jax.experimental import pallas as pl
from jax.experimental.pallas import tpu as pltpu

F32 = jnp.float32
BF16 = jnp.bfloat16
I32 = jnp.int32

EPS = 1e-5
NEG = -1e30
LANES = 128
HD_ATT = 64
BRANCHES = ((128, 1), (512, 4), (2048, 16))
RES = 16
ATT_BLK = 128
SUPER = RES * ATT_BLK
SSM_HD = 64
SSM_GROUPS = 4
D_STATE = 128
CONV_W = 4
CHUNK = 128
TOP_K = 4
SWIGLU_LIMIT = 7.0
SWIGLU_ALPHA = 1.702

TM_PROJ = 384
TN_PROJ = 1024
TM_GMM = 512
TN_GMM = 512
ROWS_GATHER = 256
TOK_COMBINE = 64


def _row_tile(m, cap):
    t = cap - cap % LANES
    while m % t:
        t -= LANES
    return t


def _cparams(sem, vmem_mb):
    return pltpu.CompilerParams(dimension_semantics=sem, vmem_limit_bytes=vmem_mb << 20)


def _dot(a, b):
    return jnp.dot(a, b, preferred_element_type=F32)


def _dot_nt(a, b):
    return lax.dot_general(a, b, (((1,), (1,)), ((), ())), preferred_element_type=F32)


def _split_dot(v, onehot_bf16):
    hi = v.astype(BF16)
    lo = (v - hi.astype(F32)).astype(BF16)
    return _dot(hi, onehot_bf16) + _dot(lo, onehot_bf16)


def _inproj_body(x_ref, nw_ref, w_ref, wdt_ref, o_ref, dt_ref, h_sc):
    @pl.when(pl.program_id(1) == 0)
    def _():
        x = x_ref[...]
        ms = jnp.mean(x * x, axis=-1, keepdims=True)
        h = (x * lax.rsqrt(ms + EPS) * nw_ref[...]).astype(BF16)
        h_sc[...] = h
        dt_ref[...] = _dot(h, wdt_ref[...])

    o_ref[...] = _dot(h_sc[...], w_ref[...])


def _inproj(x_all, norm_w, w_main, w_dt):
    m, d = x_all.shape
    n = w_main.shape[1]
    tm, tn = _row_tile(m, TM_PROJ), TN_PROJ
    return pl.pallas_call(
        _inproj_body,
        grid=(m // tm, n // tn),
        in_specs=[
            pl.BlockSpec((tm, d), lambda i, j: (i, 0)),
            pl.BlockSpec((1, d), lambda i, j: (0, 0)),
            pl.BlockSpec((d, tn), lambda i, j: (0, j)),
            pl.BlockSpec((d, LANES), lambda i, j: (0, 0)),
        ],
        out_specs=[
            pl.BlockSpec((tm, tn), lambda i, j: (i, j)),
            pl.BlockSpec((tm, LANES), lambda i, j: (i, 0)),
        ],
        out_shape=[jax.ShapeDtypeStruct((m, n), F32), jax.ShapeDtypeStruct((m, LANES), F32)],
        scratch_shapes=[pltpu.VMEM((tm, d), BF16)],
        compiler_params=_cparams(("parallel", "arbitrary"), 48),
        name="inproj",
    )(x_all, norm_w, w_main, w_dt)


def _attn_block(q, kp, kc, vp, vc, mo, lo, ao, bias):
    lane = lax.broadcasted_iota(I32, (ATT_BLK, LANES), 1)
    h0 = lane < HD_ATT
    qs = jnp.concatenate([jnp.where(h0, q, 0.0), jnp.where(h0, 0.0, q)], axis=0).astype(BF16)
    kk = jnp.concatenate([kp, kc], axis=0).astype(BF16)
    vv = jnp.concatenate([vp, vc], axis=0).astype(BF16)
    s = _dot_nt(qs, kk) + jnp.concatenate([bias, bias], axis=0)
    mn = jnp.maximum(mo, jnp.max(s, axis=1, keepdims=True))
    p = jnp.exp(s - mn)
    al = jnp.exp(mo - mn)
    ln = al * lo + jnp.sum(p, axis=1, keepdims=True)
    pv = _dot(p.astype(BF16), vv)
    an = jnp.where(h0, al[:ATT_BLK] * ao + pv[:ATT_BLK], al[ATT_BLK:] * ao + pv[ATT_BLK:])
    return mn, ln, an


def _attn_prompt_body(q_ref, k_ref, v_ref, o_ref, qr, kr, vr, m_sc, l_sc, acc_sc, bias_sc, *, seq):
    ni = seq // RES
    scale = HD_ATT ** -0.5
    for j in range(RES):
        qr[j] = q_ref[pl.ds(j, ni, stride=RES), :] * scale
        kr[j] = k_ref[pl.ds(j, ni, stride=RES), :]
        vr[j] = v_ref[pl.ds(j, ni, stride=RES), :]

    rq = lax.broadcasted_iota(I32, (ATT_BLK, 2 * ATT_BLK), 0)
    cc = lax.broadcasted_iota(I32, (ATT_BLK, 2 * ATT_BLK), 1)
    is_cur = cc >= ATT_BLK
    rk = jnp.where(is_cur, cc - ATT_BLK, cc)
    for br, (_, dil) in enumerate(BRANCHES):
        nseg = RES // dil
        seg = ATT_BLK // nseg
        oq = nseg * (rq % seg) + rq // seg
        ok = nseg * (rk % seg) + rk // seg
        allowed = (is_cur & (ok <= oq)) | (jnp.logical_not(is_cur) & (ok >= oq))
        bias_sc[2 * br + 1] = jnp.where(allowed, 0.0, NEG)
        bias_sc[2 * br] = jnp.where(allowed & is_cur, 0.0, NEG)

    def run_block(br, rows, qoff, poff, has_prev):
        seg = ATT_BLK // len(rows)
        q = jnp.concatenate([qr[r, pl.ds(qoff, seg), :] for r in rows], axis=0)
        kc = jnp.concatenate([kr[r, pl.ds(qoff, seg), :] for r in rows], axis=0)
        vc = jnp.concatenate([vr[r, pl.ds(qoff, seg), :] for r in rows], axis=0)
        kp = jnp.concatenate([kr[r, pl.ds(poff, seg), :] for r in rows], axis=0)
        vp = jnp.concatenate([vr[r, pl.ds(poff, seg), :] for r in rows], axis=0)
        return q, kp, kc, vp, vc

    def super_block(u, carry):
        base = pl.multiple_of(u * ATT_BLK, ATT_BLK)
        m_sc[...] = jnp.full(m_sc.shape, NEG, F32)
        l_sc[...] = jnp.zeros(l_sc.shape, F32)
        acc_sc[...] = jnp.zeros(acc_sc.shape, F32)

        def branch_step(br, rows, loc, seg, off_in_super):
            qoff = pl.multiple_of(base + loc, 8)
            poff = pl.multiple_of(jnp.maximum(base + loc - seg, 0), 8)
            has_prev = (base + loc >= seg).astype(I32)
            q, kp, kc, vp, vc = run_block(br, rows, qoff, poff, has_prev)
            bias = bias_sc[2 * br + has_prev]
            sl = pl.ds(pl.multiple_of(loc, 8), seg)
            mo = jnp.concatenate([m_sc[h, r, sl, :] for h in range(2) for r in rows], axis=0)
            lo = jnp.concatenate([l_sc[h, r, sl, :] for h in range(2) for r in rows], axis=0)
            ao = jnp.concatenate([acc_sc[r, sl, :] for r in rows], axis=0)
            mn, ln, an = _attn_block(q, kp, kc, vp, vc, mo, lo, ao, bias)
            for h in range(2):
                for si, r in enumerate(rows):
                    lo_r = h * ATT_BLK + si * seg
                    m_sc[h, r, sl, :] = mn[lo_r:lo_r + seg]
                    l_sc[h, r, sl, :] = ln[lo_r:lo_r + seg]
            for si, r in enumerate(rows):
                acc_sc[r, sl, :] = an[si * seg:(si + 1) * seg]

        def b1(n, c):
            branch_step(0, list(range(RES)), n * 8, 8, None)
            return c
        lax.fori_loop(0, 16, b1, 0)

        def b4(t, c):
            j4 = t // 4
            n4 = t % 4
            branch_step(1, [j4 + 4 * m for m in range(4)], n4 * 32, 32, None)
            return c
        lax.fori_loop(0, 16, b4, 0)

        def b16(j, c):
            branch_step(2, [j], 0, ATT_BLK, None)
            return c
        lax.fori_loop(0, 16, b16, 0)

        lane = lax.broadcasted_iota(I32, (ATT_BLK, LANES), 1)
        h0 = lane < HD_ATT
        for j in range(RES):
            den = jnp.where(h0, l_sc[0, j], l_sc[1, j])
            o_ref[pl.ds(u * SUPER + j, ATT_BLK, stride=RES), :] = acc_sc[j] / den
        return carry

    lax.fori_loop(0, seq // SUPER, super_block, 0)


def _attn_prompt(proj, batch, seq, col_q, col_k, col_v, n_pairs):
    ni = seq // RES
    body = functools.partial(_attn_prompt_body, seq=seq)

    def spec(col):
        return pl.BlockSpec((seq, LANES), lambda b, p: (b, col + p))

    return pl.pallas_call(
        body,
        grid=(batch, n_pairs),
        in_specs=[spec(col_q), spec(col_k), spec(col_v)],
        out_specs=pl.BlockSpec((seq, LANES), lambda b, p: (b, p)),
        out_shape=jax.ShapeDtypeStruct((batch * seq, n_pairs * LANES), F32),
        scratch_shapes=[
            pltpu.VMEM((RES, ni, LANES), F32),
            pltpu.VMEM((RES, ni, LANES), F32),
            pltpu.VMEM((RES, ni, LANES), F32),
            pltpu.VMEM((2, RES, ATT_BLK, 1), F32),
            pltpu.VMEM((2, RES, ATT_BLK, 1), F32),
            pltpu.VMEM((RES, ATT_BLK, LANES), F32),
            pltpu.VMEM((2 * len(BRANCHES), ATT_BLK, 2 * ATT_BLK), F32),
        ],
        compiler_params=_cparams(("parallel", "parallel"), 60),
        name="attn_prompt",
    )(proj, proj, proj)


def _attn_sample_body(q_ref, kn_ref, vn_ref, kc_ref, vc_ref, o_ref, *, dq, nbuf):
    b = pl.program_id(0)
    rows = pl.ds(b * dq, dq)
    q = q_ref[rows, :] * (HD_ATT ** -0.5)
    lane = lax.broadcasted_iota(I32, (dq, LANES), 1)
    h0 = lane < HD_ATT
    qs = jnp.concatenate([jnp.where(h0, q, 0.0), jnp.where(h0, 0.0, q)], axis=0).astype(BF16)
    zpad = jnp.zeros((8 - dq, LANES), F32)
    kn = jnp.concatenate([kn_ref[rows, :], zpad], axis=0).astype(BF16)
    vn = jnp.concatenate([vn_ref[rows, :], zpad], axis=0).astype(BF16)
    kc = kc_ref[0].astype(BF16)
    vc = vc_ref[0].astype(BF16)
    s_c = _dot_nt(qs, kc)
    s_n = _dot_nt(qs, kn)

    def counts(delta):
        cnt = jnp.zeros(delta.shape, F32)
        for win, dil in BRANCHES:
            hit = (delta >= 0) & (delta <= win) & (delta % dil == 0)
            cnt = cnt + hit.astype(F32)
        return cnt

    qi_c = lax.broadcasted_iota(I32, s_c.shape, 0) % dq
    cnt_c = counts(nbuf + qi_c - lax.broadcasted_iota(I32, s_c.shape, 1))
    qi_n = lax.broadcasted_iota(I32, s_n.shape, 0) % dq
    col_n = lax.broadcasted_iota(I32, s_n.shape, 1)
    cnt_n = jnp.where(col_n < dq, counts(qi_n - col_n), 0.0)
    s_c = jnp.where(cnt_c > 0, s_c, NEG)
    s_n = jnp.where(cnt_n > 0, s_n, NEG)
    m = jnp.maximum(jnp.max(s_c, axis=1, keepdims=True), jnp.max(s_n, axis=1, keepdims=True))
    e_c = cnt_c * jnp.exp(s_c - m)
    e_n = cnt_n * jnp.exp(s_n - m)
    den = jnp.sum(e_c, axis=1, keepdims=True) + jnp.sum(e_n, axis=1, keepdims=True)
    num = _dot(e_c.astype(BF16), vc) + _dot(e_n.astype(BF16), vn)
    out = num / den
    o_ref[0] = jnp.where(h0, out[:dq], out[dq:])


def _attn_sample(proj, cache_k, cache_v, row_blk, col_q, col_k, col_v, n_pairs, dec_batch, dq):
    nbuf = cache_k.shape[1]
    body = functools.partial(_attn_sample_body, dq=dq, nbuf=nbuf)
    nrows = dec_batch * dq

    def pspec(col):
        return pl.BlockSpec((nrows, LANES), lambda b, p: (row_blk, col + p))

    cspec = pl.BlockSpec((1, nbuf, LANES), lambda b, p: (b, 0, p))
    return pl.pallas_call(
        body,
        grid=(dec_batch, n_pairs),
        in_specs=[pspec(col_q), pspec(col_k), pspec(col_v), cspec, cspec],
        out_specs=pl.BlockSpec((1, dq, LANES), lambda b, p: (b, 0, p)),
        out_shape=jax.ShapeDtypeStruct((dec_batch, dq, n_pairs * LANES), F32),
        compiler_params=_cparams(("parallel", "parallel"), 32),
        name="attn_sample",
    )(proj, proj, proj, cache_k, cache_v)


def _ssd_body(xbc_ref, z_ref, dt_ref, cprev_ref, h0_ref, cw_ref, cb_ref, dtb_ref, alog_ref, dskx_ref,
              nw_ref, e_ref, r_ref, y_ref, cnew_ref, hout_ref, h_sc, xf_sc, *, nvalid, d_ssm, n_heads):
    L = CHUNK
    c = pl.program_id(1)
    gw = d_ssm // SSM_GROUPS
    hpg = n_heads // SSM_GROUPS

    @pl.when(c == 0)
    def _():
        h_sc[...] = h0_ref[0]
        xf_sc[0:8] = cprev_ref[0]

    xf_sc[8:8 + L] = xbc_ref[...]
    conv = cb_ref[...] + cw_ref[0:1, :] * xf_sc[5:5 + L]
    for w in range(1, CONV_W):
        conv = conv + cw_ref[w:w + 1, :] * xf_sc[5 + w:5 + w + L]
    cnew_ref[0] = xf_sc[nvalid:nvalid + 8]
    xf_sc[0:8] = xf_sc[L:L + 8]

    row = lax.broadcasted_iota(I32, (L, LANES), 0)
    lane = lax.broadcasted_iota(I32, (L, LANES), 1)
    if nvalid < L:
        rowc = lax.broadcasted_iota(I32, conv.shape, 0)
        conv = jnp.where(rowc < nvalid, conv, 0.0)
    act = conv * jax.nn.sigmoid(conv)
    xs = act[:, :d_ssm]
    bm = act[:, d_ssm:d_ssm + SSM_GROUPS * D_STATE].astype(BF16)
    cm = act[:, d_ssm + SSM_GROUPS * D_STATE:].astype(BF16)

    dt = jax.nn.softplus(dt_ref[...] + dtb_ref[...])
    live = lane < n_heads
    if nvalid < L:
        live = live & (row < nvalid)
    dt = jnp.where(live, dt, 0.0)
    dta = dt * (-jnp.exp(alog_ref[...]))
    tril = row >= lane
    acum = jnp.dot(tril.astype(F32), dta, precision=lax.Precision.HIGHEST, preferred_element_type=F32)
    last = acum[L - 1:L, :]
    acum_t = acum.T
    eb = e_ref[...]
    stacked = jnp.concatenate([jnp.exp(last - acum) * dt, jnp.exp(acum), dt], axis=0)
    ex = _split_dot(stacked, eb)
    xw = (xs * ex[0:L]).astype(BF16)
    eac_x = ex[L:2 * L]
    xd = xs * ex[2 * L:3 * L]

    cd_col = jnp.broadcast_to(jnp.exp(acum_t[:, L - 1:L]), (LANES, LANES))
    cd_rows = _split_dot_left(r_ref[...], cd_col)

    h0l = lane < SSM_HD
    y_parts = []
    yoff_parts = []
    for g in range(SSM_GROUPS):
        bg = bm[:, g * D_STATE:(g + 1) * D_STATE]
        cg = cm[:, g * D_STATE:(g + 1) * D_STATE]
        cbg = _dot_nt(cg, bg)
        rows = slice(g * gw, (g + 1) * gw)
        hg = h_sc[rows, :]
        yoff_parts.append(_dot_nt(cg, hg.astype(BF16)))
        st = lax.dot_general(xw[:, rows], bg, (((0,), (0,)), ((), ())), preferred_element_type=F32)
        h_sc[rows, :] = cd_rows[rows, :] * hg + st
        for jp in range(hpg // 2):
            ms = []
            for hh in range(2):
                h = g * hpg + 2 * jp + hh
                diff = acum[:, h:h + 1] - acum_t[h:h + 1, :]
                ms.append(cbg * jnp.exp(jnp.where(tril, diff, NEG)))
            lhs = jnp.concatenate(ms, axis=1).astype(BF16)
            col = (g * hpg + 2 * jp) * SSM_HD
            xp = xd[:, col:col + LANES]
            rhs = jnp.concatenate([jnp.where(h0l, xp, 0.0), jnp.where(h0l, 0.0, xp)], axis=0).astype(BF16)
            y_parts.append(_dot(lhs, rhs))
    y = jnp.concatenate(y_parts, axis=1) + jnp.concatenate(yoff_parts, axis=1) * eac_x + dskx_ref[...] * xs
    z = z_ref[...]
    y = y * (z * jax.nn.sigmoid(z))
    outs = []
    for g in range(SSM_GROUPS):
        yg = y[:, g * gw:(g + 1) * gw]
        outs.append(yg * lax.rsqrt(jnp.mean(yg * yg, axis=-1, keepdims=True) + EPS))
    y_ref[...] = (jnp.concatenate(outs, axis=1) * nw_ref[...]).astype(y_ref.dtype)
    hout_ref[0] = h_sc[...]


def _split_dot_left(onehot_bf16, v):
    p1 = v.astype(BF16)
    r1 = v - p1.astype(F32)
    p2 = r1.astype(BF16)
    p3 = (r1 - p2.astype(F32)).astype(BF16)
    return _dot(onehot_bf16, p1) + _dot(onehot_bf16, p2) + _dot(onehot_bf16, p3)


def _ssd(xbc_src, z_src, dt_src, conv_prev8, h0, consts, *, batch, n_chunks, nvalid, xbc_spec, z_spec, dt_spec,
         d_ssm, n_heads):
    conv_dim = conv_prev8.shape[-1]
    hp = h0.shape[1]
    body = functools.partial(_ssd_body, nvalid=nvalid, d_ssm=d_ssm, n_heads=n_heads)
    cw, cb, dtb, alog, dskx, nw, e_mat, r_mat = consts

    def const(a):
        return pl.BlockSpec(a.shape, lambda b, c: (0,) * a.ndim)

    return pl.pallas_call(
        body,
        grid=(batch, n_chunks),
        in_specs=[
            xbc_spec, z_spec, dt_spec,
            pl.BlockSpec((1, 8, conv_dim), lambda b, c: (b, 0, 0)),
            pl.BlockSpec((1, hp, D_STATE), lambda b, c: (b, 0, 0)),
            const(cw), const(cb), const(dtb), const(alog), const(dskx), const(nw), const(e_mat), const(r_mat),
        ],
        out_specs=[
            pl.BlockSpec((CHUNK, d_ssm), lambda b, c: (b * n_chunks + c, 0)),
            pl.BlockSpec((1, 8, conv_dim), lambda b, c: (b, 0, 0)),
            pl.BlockSpec((1, hp, D_STATE), lambda b, c: (b, 0, 0)),
        ],
        out_shape=[
            jax.ShapeDtypeStruct((batch * n_chunks * CHUNK, d_ssm), BF16),
            jax.ShapeDtypeStruct((batch, 8, conv_dim), F32),
            jax.ShapeDtypeStruct((batch, hp, D_STATE), F32),
        ],
        scratch_shapes=[pltpu.VMEM((hp, D_STATE), F32), pltpu.VMEM((CHUNK + 8, conv_dim), F32)],
        compiler_params=_cparams(("parallel", "arbitrary"), 48),
        name="ssd",
    )(xbc_src, z_src, dt_src, conv_prev8, h0, cw, cb, dtb, alog, dskx, nw, e_mat, r_mat)


def _outproj_body(x_ref, ya_ref, ys_ref, wa_ref, ws_ref, nfw_ref, wr_ref, br_ref,
                  xmid_ref, xn_ref, idx_ref, gate_ref):
    mix = _dot(ya_ref[...].astype(BF16), wa_ref[...]) + _dot(ys_ref[...], ws_ref[...])
    xm = x_ref[...] + mix
    xmid_ref[...] = xm
    xn = xm * lax.rsqrt(jnp.mean(xm * xm, axis=-1, keepdims=True) + EPS) * nfw_ref[...]
    xn_ref[...] = xn
    logits = jnp.dot(xn, wr_ref[...], precision=lax.Precision.HIGHEST, preferred_element_type=F32) + br_ref[...]
    lane = lax.broadcasted_iota(I32, logits.shape, 1)
    idx_out = jnp.zeros(logits.shape, I32)
    val_out = jnp.full(logits.shape, NEG, F32)
    for k in range(TOP_K):
        mx = jnp.max(logits, axis=1, keepdims=True)
        ix = jnp.min(jnp.where(logits == mx, lane, LANES), axis=1, keepdims=True)
        idx_out = jnp.where(lane == k, ix, idx_out)
        val_out = jnp.where(lane == k, mx, val_out)
        logits = jnp.where(lane == ix, -jnp.inf, logits)
    ev = jnp.exp(val_out - jnp.max(val_out, axis=1, keepdims=True))
    gate_ref[...] = ev / jnp.sum(ev, axis=1, keepdims=True)
    idx_ref[...] = idx_out


def _outproj(x_all, y_att, y_ssd, w_att, w_ssd, nfw, w_router, b_router):
    m, d = x_all.shape
    tm = _row_tile(m, TM_PROJ)

    def rows(width):
        return pl.BlockSpec((tm, width), lambda i: (i, 0))

    def const(a):
        return pl.BlockSpec(a.shape, lambda i: (0,) * a.ndim)

    return pl.pallas_call(
        _outproj_body,
        grid=(m // tm,),
        in_specs=[rows(d), rows(y_att.shape[1]), rows(y_ssd.shape[1]), const(w_att), const(w_ssd), const(nfw),
                  const(w_router), const(b_router)],
        out_specs=[rows(d), rows(d), rows(LANES), rows(LANES)],
        out_shape=[jax.ShapeDtypeStruct((m, d), F32), jax.ShapeDtypeStruct((m, d), F32),
                   jax.ShapeDtypeStruct((m, LANES), I32), jax.ShapeDtypeStruct((m, LANES), F32)],
        compiler_params=_cparams(("parallel",), 56),
        name="outproj",
    )(x_all, y_att, y_ssd, w_att, w_ssd, nfw, w_router, b_router)


def _gather_body(tok_ref, xn_hbm, o_ref, buf, sem):
    i = pl.program_id(0)
    n = pl.num_programs(0)
    rg = ROWS_GATHER

    def row_copy(step, slot, r):
        t = tok_ref[step * rg + r]
        return pltpu.make_async_copy(xn_hbm.at[pl.ds(t, 1), :], buf.at[slot, pl.ds(r, 1), :], sem.at[slot])

    def issue(step, slot):
        def body(r, c):
            row_copy(step, slot, r).start()
            return c
        lax.fori_loop(0, rg, body, 0)

    @pl.when(i == 0)
    def _():
        issue(0, 0)

    @pl.when(i + 1 < n)
    def _():
        issue(i + 1, (i + 1) % 2)

    slot = i % 2

    def wait_body(r, c):
        row_copy(i, slot, r).wait()
        return c
    lax.fori_loop(0, rg, wait_body, 0)
    o_ref[...] = buf[slot].astype(o_ref.dtype)


def _gather(row_tok, xn):
    cap = row_tok.shape[0]
    d = xn.shape[1]
    rg = ROWS_GATHER
    return pl.pallas_call(
        _gather_body,
        grid_spec=pltpu.PrefetchScalarGridSpec(
            num_scalar_prefetch=1,
            grid=(cap // rg,),
            in_specs=[pl.BlockSpec(memory_space=pl.ANY)],
            out_specs=pl.BlockSpec((rg, d), lambda i, tok: (i, 0)),
            scratch_shapes=[pltpu.VMEM((2, rg, d), F32), pltpu.SemaphoreType.DMA((2,))],
        ),
        out_shape=jax.ShapeDtypeStruct((cap, d), BF16),
        compiler_params=_cparams(("arbitrary",), 32),
        name="gather",
    )(row_tok, xn)


def _tile_changed(te_ref, i):
    return (i == 0) | (te_ref[i] != te_ref[jnp.maximum(i - 1, 0)])


def _gmm1_body(te_ref, nu_ref, x_ref, wg_ref, wu_ref, bg_ref, bu_ref, h_ref, wgb, wub):
    i = pl.program_id(1)

    @pl.when(_tile_changed(te_ref, i))
    def _():
        wgb[...] = wg_ref[0].astype(BF16)
        wub[...] = wu_ref[0].astype(BF16)

    @pl.when(i < nu_ref[0])
    def _():
        x = x_ref[...]
        gt = _dot(x, wgb[...]) + bg_ref[0]
        up = _dot(x, wub[...]) + bu_ref[0]
        gt = jnp.minimum(gt, SWIGLU_LIMIT)
        up = jnp.clip(up, -SWIGLU_LIMIT, SWIGLU_LIMIT)
        act = (up + 1.0) * gt * jax.nn.sigmoid(SWIGLU_ALPHA * gt)
        h_ref[...] = act.astype(h_ref.dtype)

    @pl.when(i >= nu_ref[0])
    def _():
        h_ref[...] = jnp.zeros(h_ref.shape, h_ref.dtype)


def _gmm2_body(te_ref, nu_ref, h_ref, wd_ref, bd_ref, y_ref, wdb):
    i = pl.program_id(1)

    @pl.when(_tile_changed(te_ref, i))
    def _():
        wdb[...] = wd_ref[0].astype(BF16)

    @pl.when(i < nu_ref[0])
    def _():
        y_ref[...] = _dot(h_ref[...], wdb[...]) + bd_ref[0]

    @pl.when(i >= nu_ref[0])
    def _():
        y_ref[...] = jnp.zeros(y_ref.shape, y_ref.dtype)


def _gmm_specs(d_in, tn):
    tm = TM_GMM

    def tile(i, nu):
        return jnp.minimum(i, nu[0] - 1)

    x_spec = pl.BlockSpec((tm, d_in), lambda j, i, te, nu: (tile(i, nu), 0))
    w_spec = pl.BlockSpec((1, d_in, tn), lambda j, i, te, nu: (te[tile(i, nu)], 0, j))
    b_spec = pl.BlockSpec((1, 1, tn), lambda j, i, te, nu: (te[tile(i, nu)], 0, j))
    o_spec = pl.BlockSpec((tm, tn), lambda j, i, te, nu: (i, j))
    return x_spec, w_spec, b_spec, o_spec


def _gmm1(tile_exp, n_used, x_disp, w_gate, w_up, b_gate, b_up):
    cap, d = x_disp.shape
    dff = w_gate.shape[2]
    tn = TN_GMM
    x_spec, w_spec, b_spec, o_spec = _gmm_specs(d, tn)
    return pl.pallas_call(
        _gmm1_body,
        grid_spec=pltpu.PrefetchScalarGridSpec(
            num_scalar_prefetch=2,
            grid=(dff // tn, cap // TM_GMM),
            in_specs=[x_spec, w_spec, w_spec, b_spec, b_spec],
            out_specs=o_spec,
            scratch_shapes=[pltpu.VMEM((d, tn), BF16), pltpu.VMEM((d, tn), BF16)],
        ),
        out_shape=jax.ShapeDtypeStruct((cap, dff), BF16),
        compiler_params=_cparams(("arbitrary", "arbitrary"), 48),
        name="gmm1",
    )(tile_exp, n_used, x_disp, w_gate, w_up, b_gate, b_up)


def _gmm2(tile_exp, n_used, h_disp, w_down, b_down):
    cap, dff = h_disp.shape
    d = w_down.shape[2]
    tn = TN_GMM
    x_spec, w_spec, b_spec, o_spec = _gmm_specs(dff, tn)
    return pl.pallas_call(
        _gmm2_body,
        grid_spec=pltpu.PrefetchScalarGridSpec(
            num_scalar_prefetch=2,
            grid=(d // tn, cap // TM_GMM),
            in_specs=[x_spec, w_spec, b_spec],
            out_specs=o_spec,
            scratch_shapes=[pltpu.VMEM((dff, tn), BF16)],
        ),
        out_shape=jax.ShapeDtypeStruct((cap, d), F32),
        compiler_params=_cparams(("arbitrary", "arbitrary"), 48),
        name="gmm2",
    )(tile_exp, n_used, h_disp, w_down, b_down)


def _combine_body(pos_ref, y_hbm, xmid_ref, gate_ref, fw_ref, op_ref, os_ref, buf, sem, *, n_prompt_steps):
    i = pl.program_id(0)
    n = pl.num_programs(0)
    tc = TOK_COMBINE

    def row_copy(step, slot, r, k):
        p = pos_ref[(step * tc + r) * TOP_K + k]
        return pltpu.make_async_copy(y_hbm.at[pl.ds(p, 1), :], buf.at[slot, k, pl.ds(r, 1), :], sem.at[slot])

    def issue(step, slot):
        def body(r, c):
            for k in range(TOP_K):
                row_copy(step, slot, r, k).start()
            return c
        lax.fori_loop(0, tc, body, 0)

    @pl.when(i == 0)
    def _():
        issue(0, 0)

    @pl.when(i + 1 < n)
    def _():
        issue(i + 1, (i + 1) % 2)

    slot = i % 2

    def wait_body(r, c):
        for k in range(TOP_K):
            row_copy(i, slot, r, k).wait()
        return c
    lax.fori_loop(0, tc, wait_body, 0)

    gates = gate_ref[...]
    acc = xmid_ref[...]
    for k in range(TOP_K):
        acc = acc + gates[:, k:k + 1] * buf[slot, k]
    out = acc * lax.rsqrt(jnp.mean(acc * acc, axis=-1, keepdims=True) + EPS) * fw_ref[...]

    @pl.when(i < n_prompt_steps)
    def _():
        op_ref[...] = out

    @pl.when(i >= n_prompt_steps)
    def _():
        os_ref[...] = out


def _combine(pos_flat, y_disp, x_mid, gates, final_w, n_prompt):
    m, d = x_mid.shape
    tc = TOK_COMBINE
    nps = n_prompt // tc
    body = functools.partial(_combine_body, n_prompt_steps=nps)
    return pl.pallas_call(
        body,
        grid_spec=pltpu.PrefetchScalarGridSpec(
            num_scalar_prefetch=1,
            grid=(m // tc,),
            in_specs=[
                pl.BlockSpec(memory_space=pl.ANY),
                pl.BlockSpec((tc, d), lambda i, pos: (i, 0)),
                pl.BlockSpec((tc, LANES), lambda i, pos: (i, 0)),
                pl.BlockSpec((1, d), lambda i, pos: (0, 0)),
            ],
            out_specs=[
                pl.BlockSpec((tc, d), lambda i, pos: (jnp.minimum(i, nps - 1), 0)),
                pl.BlockSpec((tc, d), lambda i, pos: (jnp.maximum(i - nps, 0), 0)),
            ],
            scratch_shapes=[pltpu.VMEM((2, TOP_K, tc, d), F32), pltpu.SemaphoreType.DMA((2,))],
        ),
        out_shape=[jax.ShapeDtypeStruct((n_prompt, d), F32), jax.ShapeDtypeStruct((m - n_prompt, d), F32)],
        compiler_params=_cparams(("arbitrary",), 32),
        name="combine",
    )(pos_flat, y_disp, x_mid, gates, final_w)


def _routing(top_idx, n_exp):
    n_assign = top_idx.size
    e_flat = top_idx.reshape(-1)
    order = jnp.argsort(e_flat, stable=True).astype(I32)
    e_s = e_flat[order]
    tok_s = order // TOP_K
    counts = jnp.sum((e_flat[:, None] == jnp.arange(n_exp, dtype=I32)[None, :]).astype(I32), axis=0)
    starts = jnp.cumsum(counts) - counts
    tiles = (counts + TM_GMM - 1) // TM_GMM
    tile_end = jnp.cumsum(tiles)
    row_start = (tile_end - tiles) * TM_GMM
    n_tiles = -(-n_assign // TM_GMM) + n_exp
    cap = n_tiles * TM_GMM
    n_used = tile_end[-1]
    tile_ids = jnp.arange(n_tiles, dtype=I32)
    tile_exp = jnp.minimum(jnp.sum((tile_end[None, :] <= jnp.minimum(tile_ids, n_used - 1)[:, None]).astype(I32),
                                   axis=1), n_exp - 1).astype(I32)
    rows = jnp.arange(cap, dtype=I32)
    row_e = tile_exp[rows // TM_GMM]
    a_sorted = rows - row_start[row_e] + starts[row_e]
    valid = (rows - row_start[row_e] < counts[row_e]) & (rows < n_used * TM_GMM)
    row_tok = jnp.where(valid, tok_s[jnp.clip(a_sorted, 0, n_assign - 1)], 0).astype(I32)
    dest = row_start[e_s] + (jnp.arange(n_assign, dtype=I32) - starts[e_s])
    pos_flat = jnp.zeros((n_assign,), I32).at[order].set(dest.astype(I32))
    return row_tok, pos_flat, tile_exp, n_used.reshape(1).astype(I32)


def kernel(x_prompt, x_sample, cache_attn_k, cache_attn_v, state_conv, state_ssm, norm_mix_w, w_in, conv_w, conv_b, dt_bias, a_log, d_skip, ssd_norm_w, w_out, norm_ffn_w, w_router, b_router, w_gate, b_gate, w_up, b_up, w_down, b_down, norm_final_w):
    depth = norm_mix_w.shape[0]
    assert depth == 1, "single-layer step"
    bp, seq, d = x_prompt.shape
    bs, dq, _ = x_sample.shape
    n_att_heads = cache_attn_k.shape[3]
    d_att = n_att_heads * HD_ATT
    n_pairs = d_att // LANES
    n_heads = a_log.shape[1]
    d_ssm = n_heads * SSM_HD
    conv_dim = conv_w.shape[2]
    n_exp = w_router.shape[2]
    n_prompt = bp * seq
    n_sample = bs * dq
    assert n_sample == CHUNK and seq % SUPER == 0 and n_heads <= LANES

    wl = w_in[0]
    o_q, o_k, o_v, o_z, o_x, o_dt = 0, d_att, 2 * d_att, 3 * d_att, 3 * d_att + d_ssm, 3 * d_att + d_ssm + conv_dim
    w_main = jnp.concatenate([wl[:, o_x:o_dt], wl[:, o_q:o_z], wl[:, o_z:o_x]], axis=1).astype(BF16)
    w_dt = jnp.pad(wl[:, o_dt:], ((0, 0), (0, LANES - n_heads))).astype(BF16)
    col_q = conv_dim // LANES
    col_k = col_q + n_pairs
    col_v = col_k + n_pairs
    w_att = w_out[0, :d_att].astype(BF16)
    w_ssd = w_out[0, d_att:].astype(BF16)
    w_r = jnp.pad(w_router[0], ((0, 0), (0, LANES - n_exp)))
    b_r = jnp.pad(b_router[0], (0, LANES - n_exp), constant_values=NEG).reshape(1, LANES)
    pad_h = (0, LANES - n_heads)
    head_of_col = jnp.arange(d_ssm, dtype=I32) // SSM_HD
    e_mat = (jnp.arange(LANES, dtype=I32)[:, None] == head_of_col[None, :]).astype(BF16)
    r_mat = (head_of_col[:, None] == jnp.arange(LANES, dtype=I32)[None, :]).astype(BF16)
    ssd_consts = (
        conv_w[0], conv_b[0].reshape(1, conv_dim),
        jnp.pad(dt_bias[0], pad_h).reshape(1, LANES), jnp.pad(a_log[0], pad_h).reshape(1, LANES),
        jnp.repeat(d_skip[0], SSM_HD).reshape(1, d_ssm), ssd_norm_w[0].reshape(1, d_ssm), e_mat, r_mat,
    )

    x_all = jnp.concatenate([x_prompt.reshape(n_prompt, d), x_sample.reshape(n_sample, d)], axis=0)
    proj, dt_all = _inproj(x_all, norm_mix_w[0].reshape(1, d), w_main, w_dt)
    n_main = proj.shape[1]
    z_col = (conv_dim + 3 * d_att) // d_ssm
    assert z_col * d_ssm == conv_dim + 3 * d_att

    y_att_p = _attn_prompt(proj, bp, seq, col_q, col_k, col_v, n_pairs)
    ck = cache_attn_k[0].reshape(bs, -1, d_att)
    cv = cache_attn_v[0].reshape(bs, -1, d_att)
    y_att_s = _attn_sample(proj, ck, cv, n_prompt // n_sample, col_q, col_k, col_v, n_pairs, bs, dq)
    y_att = jnp.concatenate([y_att_p, y_att_s.reshape(n_sample, d_att)], axis=0)

    n_chunks = seq // CHUNK
    hp = n_heads * SSM_HD
    y_ssd_p, cnew_p, h_p = _ssd(
        proj, proj, dt_all, jnp.zeros((bp, 8, conv_dim), F32), jnp.zeros((bp, hp, D_STATE), F32), ssd_consts,
        batch=bp, n_chunks=n_chunks, nvalid=CHUNK,
        xbc_spec=pl.BlockSpec((CHUNK, conv_dim), lambda b, c: (b * n_chunks + c, 0)),
        z_spec=pl.BlockSpec((CHUNK, d_ssm), lambda b, c: (b * n_chunks + c, z_col)),
        dt_spec=pl.BlockSpec((CHUNK, LANES), lambda b, c: (b * n_chunks + c, 0)),
        d_ssm=d_ssm, n_heads=n_heads)

    proj_s = proj[n_prompt:].reshape(bs, dq, n_main)
    padr = ((0, 0), (0, CHUNK - dq), (0, 0))
    xbc_s = jnp.pad(proj_s[:, :, :conv_dim], padr).reshape(bs * CHUNK, conv_dim)
    z_s = jnp.pad(proj_s[:, :, z_col * d_ssm:], padr).reshape(bs * CHUNK, d_ssm)
    dt_s = jnp.pad(dt_all[n_prompt:].reshape(bs, dq, LANES), padr).reshape(bs * CHUNK, LANES)
    cprev_s = jnp.pad(state_conv[0], ((0, 0), (8 - (CONV_W - 1), 0), (0, 0)))
    y_ssd_s, cnew_s, h_s = _ssd(
        xbc_s, z_s, dt_s, cprev_s, state_ssm[0].reshape(bs, hp, D_STATE), ssd_consts,
        batch=bs, n_chunks=1, nvalid=dq,
        xbc_spec=pl.BlockSpec((CHUNK, conv_dim), lambda b, c: (b, 0)),
        z_spec=pl.BlockSpec((CHUNK, d_ssm), lambda b, c: (b, 0)),
        dt_spec=pl.BlockSpec((CHUNK, LANES), lambda b, c: (b, 0)),
        d_ssm=d_ssm, n_heads=n_heads)
    y_ssd = jnp.concatenate([y_ssd_p, y_ssd_s.reshape(bs, CHUNK, d_ssm)[:, :dq].reshape(n_sample, d_ssm)], axis=0)

    x_mid, xn, top_idx, gates = _outproj(x_all, y_att, y_ssd, w_att, w_ssd, norm_ffn_w[0].reshape(1, d), w_r, b_r)

    row_tok, pos_flat, tile_exp, n_used = _routing(top_idx[:, :TOP_K], n_exp)
    x_disp = _gather(row_tok, xn)
    h_disp = _gmm1(tile_exp, n_used, x_disp, w_gate[0], w_up[0],
                   b_gate[0].reshape(n_exp, 1, -1), b_up[0].reshape(n_exp, 1, -1))
    y_disp = _gmm2(tile_exp, n_used, h_disp, w_down[0], b_down[0].reshape(n_exp, 1, -1))
    y_p, y_s = _combine(pos_flat, y_disp, x_mid, gates, norm_final_w.reshape(1, d), n_prompt)

    win_p = min(BRANCHES[-1][0], seq)
    pr = proj[:n_prompt].reshape(bp, seq, n_main)
    k_p = pr[:, seq - win_p:, col_k * LANES:col_v * LANES].reshape(1, bp, win_p, n_att_heads, HD_ATT)
    v_p = pr[:, seq - win_p:, col_v * LANES:col_v * LANES + d_att].reshape(1, bp, win_p, n_att_heads, HD_ATT)
    k_s = proj_s[:, :, col_k * LANES:col_v * LANES].reshape(1, bs, dq, n_att_heads, HD_ATT)
    v_s = proj_s[:, :, col_v * LANES:col_v * LANES + d_att].reshape(1, bs, dq, n_att_heads, HD_ATT)
    return (
        y_p.reshape(bp, seq, d), y_s.reshape(bs, dq, d), k_p, v_p,
        cnew_p[:, 8 - (CONV_W - 1):][None], h_p.reshape(1, bp, n_heads, SSM_HD, D_STATE),
        k_s, v_s,
        cnew_s[:, 8 - (CONV_W - 1):][None], h_s.reshape(1, bs, n_heads, SSM_HD, D_STATE),
    )
```

```python
import functools

import jax
import jax.numpy as jnp
from jax import lax
from jax.experimental import pallas as pl
from jax.experimental.pallas import tpu as pltpu

F32 = jnp.float32
BF16 = jnp.bfloat16
I32 = jnp.int32
U32 = jnp.uint32

EPS = 1e-5
NEG = -1e30
LANES = 128
HD_ATT = 64
BRANCHES = ((128, 1), (512, 4), (2048, 16))
RES = 16
ATT_BLK = 128
SUPER = RES * ATT_BLK
SSM_HD = 64
SSM_GROUPS = 4
D_STATE = 128
CONV_W = 4
CHUNK = 128
TOP_K = 4
SWIGLU_LIMIT = 7.0
SWIGLU_ALPHA = 1.702

PACK_ROWS = 8

TM_PROJ = 512
TN_PROJ = 1024
TM_ROUTE = 512
TM_DISPATCH = 512
TM_GMM = 512
TN_GMM = 512
TOK_COMBINE = 64


def _row_tile(m, cap):
    t = cap - cap % LANES
    while m % t:
        t -= LANES
    return t


def _cparams(sem, vmem_mb):
    return pltpu.CompilerParams(dimension_semantics=sem, vmem_limit_bytes=vmem_mb << 20)


def _dot(a, b):
    return jnp.dot(a, b, preferred_element_type=F32)


def _dot_nt(a, b):
    return lax.dot_general(a, b, (((1,), (1,)), ((), ())), preferred_element_type=F32)


def _split_dot(v, onehot_bf16):
    hi = v.astype(BF16)
    lo = (v - hi.astype(F32)).astype(BF16)
    return _dot(hi, onehot_bf16) + _dot(lo, onehot_bf16)


def _inproj_body(x_ref, nw_ref, w_ref, wdt_ref, o_ref, dt_ref, h_sc):
    @pl.when(pl.program_id(1) == 0)
    def _():
        x = x_ref[...]
        ms = jnp.mean(x * x, axis=-1, keepdims=True)
        h = (x * lax.rsqrt(ms + EPS) * nw_ref[...]).astype(BF16)
        h_sc[...] = h
        dt_ref[...] = _dot(h, wdt_ref[...])

    o_ref[...] = _dot(h_sc[...], w_ref[...])


def _inproj(x_all, norm_w, w_main, w_dt):
    m, d = x_all.shape
    n = w_main.shape[1]
    tm, tn = _row_tile(m, TM_PROJ), TN_PROJ
    return pl.pallas_call(
        _inproj_body,
        grid=(m // tm, n // tn),
        in_specs=[
            pl.BlockSpec((tm, d), lambda i, j: (i, 0)),
            pl.BlockSpec((1, d), lambda i, j: (0, 0)),
            pl.BlockSpec((d, tn), lambda i, j: (0, j)),
            pl.BlockSpec((d, LANES), lambda i, j: (0, 0)),
        ],
        out_specs=[
            pl.BlockSpec((tm, tn), lambda i, j: (i, j)),
            pl.BlockSpec((tm, LANES), lambda i, j: (i, 0)),
        ],
        out_shape=[jax.ShapeDtypeStruct((m, n), F32), jax.ShapeDtypeStruct((m, LANES), F32)],
        scratch_shapes=[pltpu.VMEM((tm, d), BF16)],
        compiler_params=_cparams(("parallel", "arbitrary"), 48),
        name="inproj",
    )(x_all, norm_w, w_main, w_dt)


def _attn_block(q, kp, kc, vp, vc, mo, lo, ao, bias):
    lane = lax.broadcasted_iota(I32, (ATT_BLK, LANES), 1)
    h0 = lane < HD_ATT
    qs = jnp.concatenate([jnp.where(h0, q, 0.0), jnp.where(h0, 0.0, q)], axis=0).astype(BF16)
    kk = jnp.concatenate([kp, kc], axis=0).astype(BF16)
    vv = jnp.concatenate([vp, vc], axis=0).astype(BF16)
    s = _dot_nt(qs, kk) + jnp.concatenate([bias, bias], axis=0)
    mn = jnp.maximum(mo, jnp.max(s, axis=1, keepdims=True))
    p = jnp.exp(s - mn)
    al = jnp.exp(mo - mn)
    ln = al * lo + jnp.sum(p, axis=1, keepdims=True)
    pv = _dot(p.astype(BF16), vv)
    an = jnp.where(h0, al[:ATT_BLK] * ao + pv[:ATT_BLK], al[ATT_BLK:] * ao + pv[ATT_BLK:])
    return mn, ln, an


def _attn_prompt_body(q_ref, k_ref, v_ref, o_ref, kw_ref, vw_ref, qr, kr, vr, m_sc, l_sc, acc_sc, bias_sc, *, seq):
    ni = seq // RES
    scale = HD_ATT ** -0.5
    win = kw_ref.shape[0]
    kw_ref[...] = k_ref[seq - win:, :]
    vw_ref[...] = v_ref[seq - win:, :]
    for j in range(RES):
        qr[j] = q_ref[pl.ds(j, ni, stride=RES), :] * scale
        kr[j] = k_ref[pl.ds(j, ni, stride=RES), :]
        vr[j] = v_ref[pl.ds(j, ni, stride=RES), :]

    rq = lax.broadcasted_iota(I32, (ATT_BLK, 2 * ATT_BLK), 0)
    cc = lax.broadcasted_iota(I32, (ATT_BLK, 2 * ATT_BLK), 1)
    is_cur = cc >= ATT_BLK
    rk = jnp.where(is_cur, cc - ATT_BLK, cc)
    for br, (_, dil) in enumerate(BRANCHES):
        nseg = RES // dil
        seg = ATT_BLK // nseg
        oq = nseg * (rq % seg) + rq // seg
        ok = nseg * (rk % seg) + rk // seg
        allowed = (is_cur & (ok <= oq)) | (jnp.logical_not(is_cur) & (ok >= oq))
        bias_sc[2 * br + 1] = jnp.where(allowed, 0.0, NEG)
        bias_sc[2 * br] = jnp.where(allowed & is_cur, 0.0, NEG)

    def run_block(br, rows, qoff, poff, has_prev):
        seg = ATT_BLK // len(rows)
        q = jnp.concatenate([qr[r, pl.ds(qoff, seg), :] for r in rows], axis=0)
        kc = jnp.concatenate([kr[r, pl.ds(qoff, seg), :] for r in rows], axis=0)
        vc = jnp.concatenate([vr[r, pl.ds(qoff, seg), :] for r in rows], axis=0)
        kp = jnp.concatenate([kr[r, pl.ds(poff, seg), :] for r in rows], axis=0)
        vp = jnp.concatenate([vr[r, pl.ds(poff, seg), :] for r in rows], axis=0)
        return q, kp, kc, vp, vc

    def super_block(u, carry):
        base = pl.multiple_of(u * ATT_BLK, ATT_BLK)
        m_sc[...] = jnp.full(m_sc.shape, NEG, F32)
        l_sc[...] = jnp.zeros(l_sc.shape, F32)
        acc_sc[...] = jnp.zeros(acc_sc.shape, F32)

        def branch_step(br, rows, loc, seg, off_in_super):
            qoff = pl.multiple_of(base + loc, 8)
            poff = pl.multiple_of(jnp.maximum(base + loc - seg, 0), 8)
            has_prev = (base + loc >= seg).astype(I32)
            q, kp, kc, vp, vc = run_block(br, rows, qoff, poff, has_prev)
            bias = bias_sc[2 * br + has_prev]
            sl = pl.ds(pl.multiple_of(loc, 8), seg)
            mo = jnp.concatenate([m_sc[h, r, sl, :] for h in range(2) for r in rows], axis=0)
            lo = jnp.concatenate([l_sc[h, r, sl, :] for h in range(2) for r in rows], axis=0)
            ao = jnp.concatenate([acc_sc[r, sl, :] for r in rows], axis=0)
            mn, ln, an = _attn_block(q, kp, kc, vp, vc, mo, lo, ao, bias)
            for h in range(2):
                for si, r in enumerate(rows):
                    lo_r = h * ATT_BLK + si * seg
                    m_sc[h, r, sl, :] = mn[lo_r:lo_r + seg]
                    l_sc[h, r, sl, :] = ln[lo_r:lo_r + seg]
            for si, r in enumerate(rows):
                acc_sc[r, sl, :] = an[si * seg:(si + 1) * seg]

        def b1(n, c):
            branch_step(0, list(range(RES)), n * 8, 8, None)
            return c
        lax.fori_loop(0, 16, b1, 0)

        def b4(t, c):
            j4 = t // 4
            n4 = t % 4
            branch_step(1, [j4 + 4 * m for m in range(4)], n4 * 32, 32, None)
            return c
        lax.fori_loop(0, 16, b4, 0)

        def b16(j, c):
            branch_step(2, [j], 0, ATT_BLK, None)
            return c
        lax.fori_loop(0, 16, b16, 0)

        lane = lax.broadcasted_iota(I32, (ATT_BLK, LANES), 1)
        h0 = lane < HD_ATT
        for j in range(RES):
            den = jnp.where(h0, l_sc[0, j], l_sc[1, j])
            o_ref[pl.ds(u * SUPER + j, ATT_BLK, stride=RES), :] = acc_sc[j] / den
        return carry

    lax.fori_loop(0, seq // SUPER, super_block, 0)


def _attn_prompt(proj, batch, seq, win, col_q, col_k, col_v, n_pairs):
    ni = seq // RES
    body = functools.partial(_attn_prompt_body, seq=seq)

    def spec(col):
        return pl.BlockSpec((seq, LANES), lambda b, p: (b, col + p))

    wspec = pl.BlockSpec((win, LANES), lambda b, p: (b, p))
    wshape = jax.ShapeDtypeStruct((batch * win, n_pairs * LANES), F32)
    return pl.pallas_call(
        body,
        grid=(batch, n_pairs),
        in_specs=[spec(col_q), spec(col_k), spec(col_v)],
        out_specs=[pl.BlockSpec((seq, LANES), lambda b, p: (b, p)), wspec, wspec],
        out_shape=[jax.ShapeDtypeStruct((batch * seq, n_pairs * LANES), F32), wshape, wshape],
        scratch_shapes=[
            pltpu.VMEM((RES, ni, LANES), F32),
            pltpu.VMEM((RES, ni, LANES), F32),
            pltpu.VMEM((RES, ni, LANES), F32),
            pltpu.VMEM((2, RES, ATT_BLK, 1), F32),
            pltpu.VMEM((2, RES, ATT_BLK, 1), F32),
            pltpu.VMEM((RES, ATT_BLK, LANES), F32),
            pltpu.VMEM((2 * len(BRANCHES), ATT_BLK, 2 * ATT_BLK), F32),
        ],
        compiler_params=_cparams(("parallel", "parallel"), 60),
        name="attn_prompt",
    )(proj, proj, proj)


def _attn_sample_body(q_ref, kn_ref, vn_ref, kc_ref, vc_ref, o_ref, *, dq, nbuf):
    b = pl.program_id(0)
    rows = pl.ds(b * dq, dq)
    q = q_ref[rows, :] * (HD_ATT ** -0.5)
    lane = lax.broadcasted_iota(I32, (dq, LANES), 1)
    h0 = lane < HD_ATT
    qs = jnp.concatenate([jnp.where(h0, q, 0.0), jnp.where(h0, 0.0, q)], axis=0).astype(BF16)
    zpad = jnp.zeros((8 - dq, LANES), F32)
    kn = jnp.concatenate([kn_ref[rows, :], zpad], axis=0).astype(BF16)
    vn = jnp.concatenate([vn_ref[rows, :], zpad], axis=0).astype(BF16)
    kc = kc_ref[0].astype(BF16)
    vc = vc_ref[0].astype(BF16)
    s_c = _dot_nt(qs, kc)
    s_n = _dot_nt(qs, kn)

    def counts(delta):
        cnt = jnp.zeros(delta.shape, F32)
        for win, dil in BRANCHES:
            hit = (delta >= 0) & (delta <= win) & (delta % dil == 0)
            cnt = cnt + hit.astype(F32)
        return cnt

    qi_c = lax.broadcasted_iota(I32, s_c.shape, 0) % dq
    cnt_c = counts(nbuf + qi_c - lax.broadcasted_iota(I32, s_c.shape, 1))
    qi_n = lax.broadcasted_iota(I32, s_n.shape, 0) % dq
    col_n = lax.broadcasted_iota(I32, s_n.shape, 1)
    cnt_n = jnp.where(col_n < dq, counts(qi_n - col_n), 0.0)
    s_c = jnp.where(cnt_c > 0, s_c, NEG)
    s_n = jnp.where(cnt_n > 0, s_n, NEG)
    m = jnp.maximum(jnp.max(s_c, axis=1, keepdims=True), jnp.max(s_n, axis=1, keepdims=True))
    e_c = cnt_c * jnp.exp(s_c - m)
    e_n = cnt_n * jnp.exp(s_n - m)
    den = jnp.sum(e_c, axis=1, keepdims=True) + jnp.sum(e_n, axis=1, keepdims=True)
    num = _dot(e_c.astype(BF16), vc) + _dot(e_n.astype(BF16), vn)
    out = num / den
    o_ref[0] = jnp.where(h0, out[:dq], out[dq:])


def _attn_sample(proj, cache_k, cache_v, row_blk, col_q, col_k, col_v, n_pairs, dec_batch, dq):
    nbuf = cache_k.shape[1]
    body = functools.partial(_attn_sample_body, dq=dq, nbuf=nbuf)
    nrows = dec_batch * dq

    def pspec(col):
        return pl.BlockSpec((nrows, LANES), lambda b, p: (row_blk, col + p))

    cspec = pl.BlockSpec((1, nbuf, LANES), lambda b, p: (b, 0, p))
    return pl.pallas_call(
        body,
        grid=(dec_batch, n_pairs),
        in_specs=[pspec(col_q), pspec(col_k), pspec(col_v), cspec, cspec],
        out_specs=pl.BlockSpec((1, dq, LANES), lambda b, p: (b, 0, p)),
        out_shape=jax.ShapeDtypeStruct((dec_batch, dq, n_pairs * LANES), F32),
        compiler_params=_cparams(("parallel", "parallel"), 32),
        name="attn_sample",
    )(proj, proj, proj, cache_k, cache_v)


def _ssd_body(xbc_ref, z_ref, dt_ref, cprev_ref, h0_ref, cw_ref, cb_ref, dtb_ref, alog_ref, dskx_ref,
              nw_ref, e_ref, r_ref, y_ref, cnew_ref, hout_ref, h_sc, xf_sc, *, nvalid, d_ssm, n_heads):
    L = CHUNK
    c = pl.program_id(1)
    gw = d_ssm // SSM_GROUPS
    hpg = n_heads // SSM_GROUPS

    @pl.when(c == 0)
    def _():
        h_sc[...] = h0_ref[0]
        xf_sc[0:8] = cprev_ref[0]

    if nvalid < L:
        xf_sc[8:8 + L] = jnp.zeros((L, xf_sc.shape[1]), F32)
        xf_sc[8:8 + nvalid] = xbc_ref[0]
        zpad = jnp.zeros((L - nvalid, LANES), F32)
        dt_raw = jnp.concatenate([dt_ref[0], zpad], axis=0)
    else:
        xf_sc[8:8 + L] = xbc_ref[...]
        dt_raw = dt_ref[...]
    conv = cb_ref[...] + cw_ref[0:1, :] * xf_sc[5:5 + L]
    for w in range(1, CONV_W):
        conv = conv + cw_ref[w:w + 1, :] * xf_sc[5 + w:5 + w + L]
    cnew_ref[0] = xf_sc[nvalid:nvalid + 8]
    xf_sc[0:8] = xf_sc[L:L + 8]

    row = lax.broadcasted_iota(I32, (L, LANES), 0)
    lane = lax.broadcasted_iota(I32, (L, LANES), 1)
    if nvalid < L:
        rowc = lax.broadcasted_iota(I32, conv.shape, 0)
        conv = jnp.where(rowc < nvalid, conv, 0.0)
    act = conv * jax.nn.sigmoid(conv)
    xs = act[:, :d_ssm]
    bm = act[:, d_ssm:d_ssm + SSM_GROUPS * D_STATE].astype(BF16)
    cm = act[:, d_ssm + SSM_GROUPS * D_STATE:].astype(BF16)

    dt = jax.nn.softplus(dt_raw + dtb_ref[...])
    live = lane < n_heads
    if nvalid < L:
        live = live & (row < nvalid)
    dt = jnp.where(live, dt, 0.0)
    dta = dt * (-jnp.exp(alog_ref[...]))
    tril = row >= lane
    acum = jnp.dot(tril.astype(F32), dta, precision=lax.Precision.HIGHEST, preferred_element_type=F32)
    last = acum[L - 1:L, :]
    acum_t = acum.T
    eb = e_ref[...]
    stacked = jnp.concatenate([jnp.exp(last - acum) * dt, jnp.exp(acum), dt], axis=0)
    ex = _split_dot(stacked, eb)
    xw = (xs * ex[0:L]).astype(BF16)
    eac_x = ex[L:2 * L]
    xd = xs * ex[2 * L:3 * L]

    cd_col = jnp.broadcast_to(jnp.exp(acum_t[:, L - 1:L]), (LANES, LANES))
    cd_rows = _split_dot_left(r_ref[...], cd_col)

    h0l = lane < SSM_HD
    y_parts = []
    yoff_parts = []
    for g in range(SSM_GROUPS):
        bg = bm[:, g * D_STATE:(g + 1) * D_STATE]
        cg = cm[:, g * D_STATE:(g + 1) * D_STATE]
        cbg = _dot_nt(cg, bg)
        rows = slice(g * gw, (g + 1) * gw)
        hg = h_sc[rows, :]
        yoff_parts.append(_dot_nt(cg, hg.astype(BF16)))
        st = lax.dot_general(xw[:, rows], bg, (((0,), (0,)), ((), ())), preferred_element_type=F32)
        h_sc[rows, :] = cd_rows[rows, :] * hg + st
        for jp in range(hpg // 2):
            ms = []
            for hh in range(2):
                h = g * hpg + 2 * jp + hh
                diff = acum[:, h:h + 1] - acum_t[h:h + 1, :]
                ms.append(cbg * jnp.exp(jnp.where(tril, diff, NEG)))
            lhs = jnp.concatenate(ms, axis=1).astype(BF16)
            col = (g * hpg + 2 * jp) * SSM_HD
            xp = xd[:, col:col + LANES]
            rhs = jnp.concatenate([jnp.where(h0l, xp, 0.0), jnp.where(h0l, 0.0, xp)], axis=0).astype(BF16)
            y_parts.append(_dot(lhs, rhs))
    y = jnp.concatenate(y_parts, axis=1) + jnp.concatenate(yoff_parts, axis=1) * eac_x + dskx_ref[...] * xs
    if nvalid < L:
        y = y[:8]
        z = jnp.concatenate([z_ref[0], jnp.zeros((8 - nvalid, d_ssm), F32)], axis=0)
    else:
        z = z_ref[...]
    y = y * (z * jax.nn.sigmoid(z))
    outs = []
    for g in range(SSM_GROUPS):
        yg = y[:, g * gw:(g + 1) * gw]
        outs.append(yg * lax.rsqrt(jnp.mean(yg * yg, axis=-1, keepdims=True) + EPS))
    y = jnp.concatenate(outs, axis=1) * nw_ref[...]
    if nvalid < L:
        y_ref[0] = y[:nvalid].astype(y_ref.dtype)
    else:
        y_ref[...] = y.astype(y_ref.dtype)
    hout_ref[0] = h_sc[...]


def _split_dot_left(onehot_bf16, v):
    p1 = v.astype(BF16)
    r1 = v - p1.astype(F32)
    p2 = r1.astype(BF16)
    p3 = (r1 - p2.astype(F32)).astype(BF16)
    return _dot(onehot_bf16, p1) + _dot(onehot_bf16, p2) + _dot(onehot_bf16, p3)


def _ssd(xbc_src, z_src, dt_src, conv_prev8, h0, consts, *, batch, n_chunks, nvalid, xbc_spec, z_spec, dt_spec,
         d_ssm, n_heads):
    conv_dim = conv_prev8.shape[-1]
    hp = h0.shape[1]
    body = functools.partial(_ssd_body, nvalid=nvalid, d_ssm=d_ssm, n_heads=n_heads)
    cw, cb, dtb, alog, dskx, nw, e_mat, r_mat = consts

    def const(a):
        return pl.BlockSpec(a.shape, lambda b, c: (0,) * a.ndim)

    if nvalid < CHUNK:
        assert n_chunks == 1 and nvalid <= 8
        y_spec = pl.BlockSpec((1, nvalid, d_ssm), lambda b, c: (b, 0, 0))
        y_shape = jax.ShapeDtypeStruct((batch, nvalid, d_ssm), BF16)
    else:
        y_spec = pl.BlockSpec((CHUNK, d_ssm), lambda b, c: (b * n_chunks + c, 0))
        y_shape = jax.ShapeDtypeStruct((batch * n_chunks * CHUNK, d_ssm), BF16)
    return pl.pallas_call(
        body,
        grid=(batch, n_chunks),
        in_specs=[
            xbc_spec, z_spec, dt_spec,
            pl.BlockSpec((1, 8, conv_dim), lambda b, c: (b, 0, 0)),
            pl.BlockSpec((1, hp, D_STATE), lambda b, c: (b, 0, 0)),
            const(cw), const(cb), const(dtb), const(alog), const(dskx), const(nw), const(e_mat), const(r_mat),
        ],
        out_specs=[
            y_spec,
            pl.BlockSpec((1, 8, conv_dim), lambda b, c: (b, 0, 0)),
            pl.BlockSpec((1, hp, D_STATE), lambda b, c: (b, 0, 0)),
        ],
        out_shape=[
            y_shape,
            jax.ShapeDtypeStruct((batch, 8, conv_dim), F32),
            jax.ShapeDtypeStruct((batch, hp, D_STATE), F32),
        ],
        scratch_shapes=[pltpu.VMEM((hp, D_STATE), F32), pltpu.VMEM((CHUNK + 8, conv_dim), F32)],
        compiler_params=_cparams(("parallel", "arbitrary"), 48),
        name="ssd",
    )(xbc_src, z_src, dt_src, conv_prev8, h0, cw, cb, dtb, alog, dskx, nw, e_mat, r_mat)


def _outproj_body(x_ref, ya_ref, ys_ref, wa_ref, ws_ref, nfw_ref, wr_ref, br_ref,
                  xmid_ref, xpk_ref, idx_ref, gate_ref):
    mix = _dot(ya_ref[...].astype(BF16), wa_ref[...]) + _dot(ys_ref[...], ws_ref[...])
    xm = x_ref[...] + mix
    xmid_ref[...] = xm
    xn = xm * lax.rsqrt(jnp.mean(xm * xm, axis=-1, keepdims=True) + EPS) * nfw_ref[...]
    tm, d = xn.shape
    bits = lax.bitcast_convert_type(xn.astype(BF16).astype(F32), U32)
    words = (bits[:, d // 2:] & jnp.uint32(0xFFFF0000)) | (bits[:, :d // 2] >> 16)
    for s in range(PACK_ROWS):
        xpk_ref[pl.ds(s, tm, stride=PACK_ROWS), :] = words[:, s * LANES:(s + 1) * LANES]
    logits = jnp.dot(xn, wr_ref[...], precision=lax.Precision.HIGHEST, preferred_element_type=F32) + br_ref[...]
    lane = lax.broadcasted_iota(I32, logits.shape, 1)
    idx_out = jnp.zeros(logits.shape, I32)
    val_out = jnp.full(logits.shape, NEG, F32)
    for k in range(TOP_K):
        mx = jnp.max(logits, axis=1, keepdims=True)
        ix = jnp.min(jnp.where(logits == mx, lane, LANES), axis=1, keepdims=True)
        idx_out = jnp.where(lane == k, ix, idx_out)
        val_out = jnp.where(lane == k, mx, val_out)
        logits = jnp.where(lane == ix, -jnp.inf, logits)
    ev = jnp.exp(val_out - jnp.max(val_out, axis=1, keepdims=True))
    gate_ref[...] = ev / jnp.sum(ev, axis=1, keepdims=True)
    idx_ref[...] = idx_out


def _outproj(x_all, y_att, y_ssd, w_att, w_ssd, nfw, w_router, b_router):
    m, d = x_all.shape
    assert d == 2 * PACK_ROWS * LANES
    tm = _row_tile(m, TM_PROJ)

    def rows(width):
        return pl.BlockSpec((tm, width), lambda i: (i, 0))

    def const(a):
        return pl.BlockSpec(a.shape, lambda i: (0,) * a.ndim)

    return pl.pallas_call(
        _outproj_body,
        grid=(m // tm,),
        in_specs=[rows(d), rows(y_att.shape[1]), rows(y_ssd.shape[1]), const(w_att), const(w_ssd), const(nfw),
                  const(w_router), const(b_router)],
        out_specs=[rows(d), pl.BlockSpec((tm * PACK_ROWS, LANES), lambda i: (i, 0)), rows(LANES), rows(LANES)],
        out_shape=[jax.ShapeDtypeStruct((m, d), F32), jax.ShapeDtypeStruct((m * PACK_ROWS, LANES), U32),
                   jax.ShapeDtypeStruct((m, LANES), I32), jax.ShapeDtypeStruct((m, LANES), F32)],
        compiler_params=_cparams(("parallel",), 56),
        name="outproj",
    )(x_all, y_att, y_ssd, w_att, w_ssd, nfw, w_router, b_router)


def _route_body(idx_ref, cin_ref, rank_ref, cout_ref, carry):
    @pl.when(pl.program_id(0) == 0)
    def _():
        carry[...] = cin_ref[...]

    tr = idx_ref.shape[0]
    lane = lax.broadcasted_iota(I32, (tr, LANES), 1)
    tril = (lax.broadcasted_iota(I32, (tr, tr), 0) >= lax.broadcasted_iota(I32, (tr, tr), 1))
    tril = jnp.where(tril, 1.0, 0.0).astype(BF16)
    idx = idx_ref[...]
    cnt = carry[...]
    out = jnp.zeros((tr, LANES), I32)
    for k in range(TOP_K):
        hit = lane == idx[:, k:k + 1]
        pre = _dot(tril, jnp.where(hit, 1.0, 0.0).astype(BF16))
        rk = jnp.sum(jnp.where(hit, pre + cnt - 1.0, 0.0), axis=1, keepdims=True)
        out = jnp.where(lane == k, rk.astype(I32), out)
        cnt = cnt + pre[tr - 1:tr, :]
    carry[...] = cnt
    rank_ref[...] = out
    cout_ref[...] = cnt


def _route(top_idx, counts_in):
    m = top_idx.shape[0]
    tr = _row_tile(m, TM_ROUTE)
    return pl.pallas_call(
        _route_body,
        grid=(m // tr,),
        in_specs=[pl.BlockSpec((tr, LANES), lambda i: (i, 0)), pl.BlockSpec((1, LANES), lambda i: (0, 0))],
        out_specs=[pl.BlockSpec((tr, LANES), lambda i: (i, 0)), pl.BlockSpec((1, LANES), lambda i: (0, 0))],
        out_shape=[jax.ShapeDtypeStruct((m, LANES), I32), jax.ShapeDtypeStruct((1, LANES), F32)],
        scratch_shapes=[pltpu.VMEM((1, LANES), F32)],
        compiler_params=_cparams(("arbitrary",), 32),
        name="route",
    )(top_idx, counts_in)


def _dispatch_body(pos_ref, x_ref, xd_in, xd_hbm, sem):
    del xd_in
    i = pl.program_id(0)
    ts = x_ref.shape[0] // PACK_ROWS

    def row_copy(r, k):
        p = pos_ref[(i * ts + r) * TOP_K + k]
        src = x_ref.at[pl.ds(pl.multiple_of(r * PACK_ROWS, PACK_ROWS), PACK_ROWS), :]
        dst = xd_hbm.at[pl.ds(pl.multiple_of(p * PACK_ROWS, PACK_ROWS), PACK_ROWS), :]
        return pltpu.make_async_copy(src, dst, sem)

    def issue(r, c):
        for k in range(TOP_K):
            row_copy(r, k).start()
        return c
    lax.fori_loop(0, ts, issue, 0)

    def drain(r, c):
        for k in range(TOP_K):
            row_copy(r, k).wait()
        return c
    lax.fori_loop(0, ts, drain, 0)


def _dispatch(pos_flat, x_packed, x_disp):
    m = x_packed.shape[0] // PACK_ROWS
    ts = _row_tile(m, TM_DISPATCH)
    return pl.pallas_call(
        _dispatch_body,
        grid_spec=pltpu.PrefetchScalarGridSpec(
            num_scalar_prefetch=1,
            grid=(m // ts,),
            in_specs=[pl.BlockSpec((ts * PACK_ROWS, LANES), lambda i, pos: (i, 0)),
                      pl.BlockSpec(memory_space=pl.ANY)],
            out_specs=pl.BlockSpec(memory_space=pl.ANY),
            scratch_shapes=[pltpu.SemaphoreType.DMA(())],
        ),
        out_shape=jax.ShapeDtypeStruct(x_disp.shape, x_disp.dtype),
        input_output_aliases={2: 0},
        compiler_params=_cparams(("arbitrary",), 32),
        name="dispatch",
    )(pos_flat, x_packed, x_disp)


def _tile_changed(te_ref, i):
    return (i == 0) | (te_ref[i] != te_ref[jnp.maximum(i - 1, 0)])


def _gmm1_body(te_ref, nu_ref, x_ref, wg_ref, wu_ref, bg_ref, bu_ref, h_ref, wgb, wub):
    i = pl.program_id(1)

    @pl.when(_tile_changed(te_ref, i))
    def _():
        wgb[...] = wg_ref[0].astype(BF16)
        wub[...] = wu_ref[0].astype(BF16)

    @pl.when(i < nu_ref[0])
    def _():
        tm = x_ref.shape[0] // PACK_ROWS
        lo, hi = [], []
        for s in range(PACK_ROWS):
            w = x_ref[pl.ds(s, tm, stride=PACK_ROWS), :]
            lo.append(lax.bitcast_convert_type(w << 16, F32).astype(BF16))
            hi.append(lax.bitcast_convert_type(w & jnp.uint32(0xFFFF0000), F32).astype(BF16))
        x = jnp.concatenate(lo + hi, axis=1)
        gt = _dot(x, wgb[...]) + bg_ref[0]
        up = _dot(x, wub[...]) + bu_ref[0]
        gt = jnp.minimum(gt, SWIGLU_LIMIT)
        up = jnp.clip(up, -SWIGLU_LIMIT, SWIGLU_LIMIT)
        act = (up + 1.0) * gt * jax.nn.sigmoid(SWIGLU_ALPHA * gt)
        h_ref[...] = act.astype(h_ref.dtype)

    @pl.when(i >= nu_ref[0])
    def _():
        h_ref[...] = jnp.zeros(h_ref.shape, h_ref.dtype)


def _gmm2_body(te_ref, nu_ref, h_ref, wd_ref, bd_ref, y_ref, wdb):
    i = pl.program_id(1)

    @pl.when(_tile_changed(te_ref, i))
    def _():
        wdb[...] = wd_ref[0].astype(BF16)

    @pl.when(i < nu_ref[0])
    def _():
        y_ref[...] = _dot(h_ref[...], wdb[...]) + bd_ref[0]

    @pl.when(i >= nu_ref[0])
    def _():
        y_ref[...] = jnp.zeros(y_ref.shape, y_ref.dtype)


def _gmm_specs(d_in, tn):
    tm = TM_GMM

    def tile(i, nu):
        return jnp.minimum(i, nu[0] - 1)

    x_spec = pl.BlockSpec((tm, d_in), lambda j, i, te, nu: (tile(i, nu), 0))
    w_spec = pl.BlockSpec((1, d_in, tn), lambda j, i, te, nu: (te[tile(i, nu)], 0, j))
    b_spec = pl.BlockSpec((1, 1, tn), lambda j, i, te, nu: (te[tile(i, nu)], 0, j))
    o_spec = pl.BlockSpec((tm, tn), lambda j, i, te, nu: (i, j))
    return x_spec, w_spec, b_spec, o_spec


def _gmm1(tile_exp, n_used, x_disp, w_gate, w_up, b_gate, b_up):
    cap = x_disp.shape[0] // PACK_ROWS
    d, dff = w_gate.shape[1], w_gate.shape[2]
    tn = TN_GMM
    _, w_spec, b_spec, o_spec = _gmm_specs(d, tn)
    x_spec = pl.BlockSpec((TM_GMM * PACK_ROWS, LANES), lambda j, i, te, nu: (jnp.minimum(i, nu[0] - 1), 0))
    return pl.pallas_call(
        _gmm1_body,
        grid_spec=pltpu.PrefetchScalarGridSpec(
            num_scalar_prefetch=2,
            grid=(dff // tn, cap // TM_GMM),
            in_specs=[x_spec, w_spec, w_spec, b_spec, b_spec],
            out_specs=o_spec,
            scratch_shapes=[pltpu.VMEM((d, tn), BF16), pltpu.VMEM((d, tn), BF16)],
        ),
        out_shape=jax.ShapeDtypeStruct((cap, dff), BF16),
        compiler_params=_cparams(("arbitrary", "arbitrary"), 48),
        name="gmm1",
    )(tile_exp, n_used, x_disp, w_gate, w_up, b_gate, b_up)


def _gmm2(tile_exp, n_used, h_disp, w_down, b_down):
    cap, dff = h_disp.shape
    d = w_down.shape[2]
    tn = TN_GMM
    x_spec, w_spec, b_spec, o_spec = _gmm_specs(dff, tn)
    return pl.pallas_call(
        _gmm2_body,
        grid_spec=pltpu.PrefetchScalarGridSpec(
            num_scalar_prefetch=2,
            grid=(d // tn, cap // TM_GMM),
            in_specs=[x_spec, w_spec, b_spec],
            out_specs=o_spec,
            scratch_shapes=[pltpu.VMEM((dff, tn), BF16)],
        ),
        out_shape=jax.ShapeDtypeStruct((cap, d), F32),
        compiler_params=_cparams(("arbitrary", "arbitrary"), 48),
        name="gmm2",
    )(tile_exp, n_used, h_disp, w_down, b_down)


def _combine_body(pos_ref, y_hbm, xmid_ref, gate_ref, fw_ref, o_ref, buf, sem):
    i = pl.program_id(0)
    n = pl.num_programs(0)
    tc = TOK_COMBINE

    def row_copy(step, slot, r, k):
        p = pos_ref[(step * tc + r) * TOP_K + k]
        return pltpu.make_async_copy(y_hbm.at[pl.ds(p, 1), :], buf.at[slot, k, pl.ds(r, 1), :], sem.at[slot])

    def issue(step, slot):
        def body(r, c):
            for k in range(TOP_K):
                row_copy(step, slot, r, k).start()
            return c
        lax.fori_loop(0, tc, body, 0)

    @pl.when(i == 0)
    def _():
        issue(0, 0)

    @pl.when(i + 1 < n)
    def _():
        issue(i + 1, (i + 1) % 2)

    slot = i % 2

    def wait_body(r, c):
        for k in range(TOP_K):
            row_copy(i, slot, r, k).wait()
        return c
    lax.fori_loop(0, tc, wait_body, 0)

    gates = gate_ref[...]
    acc = xmid_ref[...]
    for k in range(TOP_K):
        acc = acc + gates[:, k:k + 1] * buf[slot, k]
    o_ref[...] = acc * lax.rsqrt(jnp.mean(acc * acc, axis=-1, keepdims=True) + EPS) * fw_ref[...]


def _combine(pos_flat, y_disp, x_mid, gates, final_w):
    m, d = x_mid.shape
    tc = TOK_COMBINE
    return pl.pallas_call(
        _combine_body,
        grid_spec=pltpu.PrefetchScalarGridSpec(
            num_scalar_prefetch=1,
            grid=(m // tc,),
            in_specs=[
                pl.BlockSpec(memory_space=pl.ANY),
                pl.BlockSpec((tc, d), lambda i, pos: (i, 0)),
                pl.BlockSpec((tc, LANES), lambda i, pos: (i, 0)),
                pl.BlockSpec((1, d), lambda i, pos: (0, 0)),
            ],
            out_specs=pl.BlockSpec((tc, d), lambda i, pos: (i, 0)),
            scratch_shapes=[pltpu.VMEM((2, TOP_K, tc, d), F32), pltpu.SemaphoreType.DMA((2,))],
        ),
        out_shape=jax.ShapeDtypeStruct((m, d), F32),
        compiler_params=_cparams(("arbitrary",), 32),
        name="combine",
    )(pos_flat, y_disp, x_mid, gates, final_w)


def _tile_layout(counts, n_assign, n_exp):
    tiles = (counts + TM_GMM - 1) // TM_GMM
    tile_end = jnp.cumsum(tiles)
    row_start = (tile_end - tiles) * TM_GMM
    n_tiles = -(-n_assign // TM_GMM) + n_exp
    n_used = tile_end[-1]
    tile_ids = jnp.minimum(jnp.arange(n_tiles, dtype=I32), n_used - 1)
    tile_exp = jnp.minimum(jnp.sum((tile_end[None, :] <= tile_ids[:, None]).astype(I32), axis=1), n_exp - 1)
    return row_start.astype(I32), tile_exp.astype(I32), n_used.reshape(1).astype(I32), n_tiles * TM_GMM


def _positions(top_idx, rank, row_start, n_exp):
    idx = top_idx[:, :TOP_K]
    onehot = idx[:, :, None] == jnp.arange(n_exp, dtype=I32)[None, None, :]
    base = jnp.sum(jnp.where(onehot, row_start[None, None, :], 0), axis=-1)
    return (base + rank[:, :TOP_K]).reshape(-1).astype(I32)


def kernel(x_prompt, x_sample, cache_attn_k, cache_attn_v, state_conv, state_ssm, norm_mix_w, w_in, conv_w, conv_b, dt_bias, a_log, d_skip, ssd_norm_w, w_out, norm_ffn_w, w_router, b_router, w_gate, b_gate, w_up, b_up, w_down, b_down, norm_final_w):
    depth = norm_mix_w.shape[0]
    assert depth == 1, "single-layer step"
    bp, seq, d = x_prompt.shape
    bs, dq, _ = x_sample.shape
    n_att_heads = cache_attn_k.shape[3]
    d_att = n_att_heads * HD_ATT
    n_pairs = d_att // LANES
    n_heads = a_log.shape[1]
    d_ssm = n_heads * SSM_HD
    conv_dim = conv_w.shape[2]
    n_exp = w_router.shape[2]
    n_prompt = bp * seq
    n_sample = bs * dq
    assert n_sample == CHUNK and seq % SUPER == 0 and n_heads <= LANES

    wl = w_in[0]
    o_q, o_k, o_v, o_z, o_x, o_dt = 0, d_att, 2 * d_att, 3 * d_att, 3 * d_att + d_ssm, 3 * d_att + d_ssm + conv_dim
    w_main = jnp.concatenate([wl[:, o_x:o_dt], wl[:, o_q:o_z], wl[:, o_z:o_x]], axis=1).astype(BF16)
    w_dt = jnp.pad(wl[:, o_dt:], ((0, 0), (0, LANES - n_heads))).astype(BF16)
    col_q = conv_dim // LANES
    col_k = col_q + n_pairs
    col_v = col_k + n_pairs
    w_att = w_out[0, :d_att].astype(BF16)
    w_ssd = w_out[0, d_att:].astype(BF16)
    w_r = jnp.pad(w_router[0], ((0, 0), (0, LANES - n_exp)))
    b_r = jnp.pad(b_router[0], (0, LANES - n_exp), constant_values=NEG).reshape(1, LANES)
    pad_h = (0, LANES - n_heads)
    head_of_col = jnp.arange(d_ssm, dtype=I32) // SSM_HD
    e_mat = (jnp.arange(LANES, dtype=I32)[:, None] == head_of_col[None, :]).astype(BF16)
    r_mat = (head_of_col[:, None] == jnp.arange(LANES, dtype=I32)[None, :]).astype(BF16)
    ssd_consts = (
        conv_w[0], conv_b[0].reshape(1, conv_dim),
        jnp.pad(dt_bias[0], pad_h).reshape(1, LANES), jnp.pad(a_log[0], pad_h).reshape(1, LANES),
        jnp.repeat(d_skip[0], SSM_HD).reshape(1, d_ssm), ssd_norm_w[0].reshape(1, d_ssm), e_mat, r_mat,
    )

    xp = x_prompt.reshape(n_prompt, d)
    xs = x_sample.reshape(n_sample, d)
    nmw = norm_mix_w[0].reshape(1, d)
    proj_p, dt_p = _inproj(xp, nmw, w_main, w_dt)
    proj_s, dt_s = _inproj(xs, nmw, w_main, w_dt)
    n_main = proj_p.shape[1]
    z_col = (conv_dim + 3 * d_att) // d_ssm
    assert z_col * d_ssm == conv_dim + 3 * d_att

    win_p = min(BRANCHES[-1][0], seq)
    y_att_p, k_p, v_p = _attn_prompt(proj_p, bp, seq, win_p, col_q, col_k, col_v, n_pairs)
    ck = cache_attn_k[0].reshape(bs, -1, d_att)
    cv = cache_attn_v[0].reshape(bs, -1, d_att)
    y_att_s = _attn_sample(proj_s, ck, cv, 0, col_q, col_k, col_v, n_pairs, bs, dq)

    n_chunks = seq // CHUNK
    hp = n_heads * SSM_HD
    y_ssd_p, cnew_p, h_p = _ssd(
        proj_p, proj_p, dt_p, jnp.zeros((bp, 8, conv_dim), F32), jnp.zeros((bp, hp, D_STATE), F32), ssd_consts,
        batch=bp, n_chunks=n_chunks, nvalid=CHUNK,
        xbc_spec=pl.BlockSpec((CHUNK, conv_dim), lambda b, c: (b * n_chunks + c, 0)),
        z_spec=pl.BlockSpec((CHUNK, d_ssm), lambda b, c: (b * n_chunks + c, z_col)),
        dt_spec=pl.BlockSpec((CHUNK, LANES), lambda b, c: (b * n_chunks + c, 0)),
        d_ssm=d_ssm, n_heads=n_heads)

    proj_s3 = proj_s.reshape(bs, dq, n_main)
    cprev_s = jnp.pad(state_conv[0], ((0, 0), (8 - (CONV_W - 1), 0), (0, 0)))
    y_ssd_s, cnew_s, h_s = _ssd(
        proj_s3, proj_s3, dt_s.reshape(bs, dq, LANES), cprev_s, state_ssm[0].reshape(bs, hp, D_STATE), ssd_consts,
        batch=bs, n_chunks=1, nvalid=dq,
        xbc_spec=pl.BlockSpec((1, dq, conv_dim), lambda b, c: (b, 0, 0)),
        z_spec=pl.BlockSpec((1, dq, d_ssm), lambda b, c: (b, 0, z_col)),
        dt_spec=pl.BlockSpec((1, dq, LANES), lambda b, c: (b, 0, 0)),
        d_ssm=d_ssm, n_heads=n_heads)

    nfw = norm_ffn_w[0].reshape(1, d)
    xmid_p, xpk_p, idx_p, gates_p = _outproj(xp, y_att_p, y_ssd_p, w_att, w_ssd, nfw, w_r, b_r)
    xmid_s, xpk_s, idx_s, gates_s = _outproj(xs, y_att_s.reshape(n_sample, d_att),
                                             y_ssd_s.reshape(n_sample, d_ssm), w_att, w_ssd, nfw, w_r, b_r)

    rank_p, cnt_p = _route(idx_p, jnp.zeros((1, LANES), F32))
    rank_s, cnt_all = _route(idx_s, cnt_p)
    counts = cnt_all[0, :n_exp].astype(I32)
    n_assign = (n_prompt + n_sample) * TOP_K
    row_start, tile_exp, n_used, cap = _tile_layout(counts, n_assign, n_exp)
    pos_p = _positions(idx_p, rank_p, row_start, n_exp)
    pos_s = _positions(idx_s, rank_s, row_start, n_exp)
    x_disp = jnp.zeros((cap * PACK_ROWS, LANES), U32)
    x_disp = _dispatch(pos_p, xpk_p, x_disp)
    x_disp = _dispatch(pos_s, xpk_s, x_disp)
    h_disp = _gmm1(tile_exp, n_used, x_disp, w_gate[0], w_up[0],
                   b_gate[0].reshape(n_exp, 1, -1), b_up[0].reshape(n_exp, 1, -1))
    y_disp = _gmm2(tile_exp, n_used, h_disp, w_down[0], b_down[0].reshape(n_exp, 1, -1))
    fw = norm_final_w.reshape(1, d)
    y_p = _combine(pos_p, y_disp, xmid_p, gates_p, fw)
    y_s = _combine(pos_s, y_disp, xmid_s, gates_s, fw)

    k_p = k_p.reshape(1, bp, win_p, n_att_heads, HD_ATT)
    v_p = v_p.reshape(1, bp, win_p, n_att_heads, HD_ATT)
    k_s = proj_s3[:, :, col_k * LANES:col_v * LANES].reshape(1, bs, dq, n_att_heads, HD_ATT)
    v_s = proj_s3[:, :, col_v * LANES:col_v * LANES + d_att].reshape(1, bs, dq, n_att_heads, HD_ATT)
    return (
        y_p.reshape(bp, seq, d), y_s.reshape(bs, dq, d), k_p, v_p,
        cnew_p[:, 8 - (CONV_W - 1):][None], h_p.reshape(1, bp, n_heads, SSM_HD, D_STATE),
        k_s, v_s,
        cnew_s[:, 8 - (CONV_W - 1):][None], h_s.reshape(1, bs, n_heads, SSM_HD, D_STATE),
    )
```

```python
import functools

import jax
import jax.numpy as jnp
from jax import lax
from jax.experimental import pallas as pl
from jax.experimental.pallas import tpu as pltpu

F32 = jnp.float32
BF16 = jnp.bfloat16
I32 = jnp.int32
U32 = jnp.uint32

EPS = 1e-5
NEG = -1e30
LANES = 128
HD_ATT = 64
BRANCHES = ((128, 1), (512, 4), (2048, 16))
RES = 16
ATT_BLK = 128
ATT_BLOCKS_PER_STEP = 2
SUPER = RES * ATT_BLK
SSM_HD = 64
SSM_GROUPS = 4
D_STATE = 128
CONV_W = 4
CHUNK = 128
TOP_K = 4
SWIGLU_LIMIT = 7.0
SWIGLU_ALPHA = 1.702

PACK_ROWS = 8

TM_PROJ = 512
OUTPROJ_PARTS = 2
TN_PROJ = 2048
TM_ROUTE = 512
TM_DISPATCH = 512
TM_GMM = 512
TN_GMM = 1024
TOK_COMBINE = 64


def _row_tile(m, cap):
    t = cap - cap % LANES
    while m % t:
        t -= LANES
    return t


def _cparams(sem, vmem_mb):
    return pltpu.CompilerParams(dimension_semantics=sem, vmem_limit_bytes=vmem_mb << 20)


def _dot(a, b):
    return jnp.dot(a, b, preferred_element_type=F32)


def _dot_nt(a, b):
    return lax.dot_general(a, b, (((1,), (1,)), ((), ())), preferred_element_type=F32)


def _split_dot(v, onehot_bf16):
    hi = v.astype(BF16)
    lo = (v - hi.astype(F32)).astype(BF16)
    return _dot(hi, onehot_bf16) + _dot(lo, onehot_bf16)


def _inproj_body(x_ref, nw_ref, w_ref, wdt_ref, o_ref, dt_ref, h_sc):
    @pl.when(pl.program_id(1) == 0)
    def _():
        x = x_ref[...]
        ms = jnp.mean(x * x, axis=-1, keepdims=True)
        h = (x * lax.rsqrt(ms + EPS) * nw_ref[...]).astype(BF16)
        h_sc[...] = h
        dt_ref[...] = _dot(h, wdt_ref[...])

    o_ref[...] = _dot(h_sc[...], w_ref[...])


def _inproj(x_all, norm_w, w_main, w_dt):
    m, d = x_all.shape
    n = w_main.shape[1]
    tm, tn = _row_tile(m, TM_PROJ), TN_PROJ
    return pl.pallas_call(
        _inproj_body,
        grid=(m // tm, n // tn),
        in_specs=[
            pl.BlockSpec((tm, d), lambda i, j: (i, 0)),
            pl.BlockSpec((1, d), lambda i, j: (0, 0)),
            pl.BlockSpec((d, tn), lambda i, j: (0, j)),
            pl.BlockSpec((d, LANES), lambda i, j: (0, 0)),
        ],
        out_specs=[
            pl.BlockSpec((tm, tn), lambda i, j: (i, j)),
            pl.BlockSpec((tm, LANES), lambda i, j: (i, 0)),
        ],
        out_shape=[jax.ShapeDtypeStruct((m, n), F32), jax.ShapeDtypeStruct((m, LANES), F32)],
        scratch_shapes=[pltpu.VMEM((tm, d), BF16)],
        compiler_params=_cparams(("parallel", "arbitrary"), 48),
        name="inproj",
    )(x_all, norm_w, w_main, w_dt)


def _attn_blocks(args):
    lane = lax.broadcasted_iota(I32, (ATT_BLK, LANES), 1)
    h0 = lane < HD_ATT
    hk = lax.broadcasted_iota(I32, (2 * ATT_BLK, LANES), 1) < HD_ATT
    n = len(args)
    qs = [jnp.concatenate([jnp.where(h0, a[0], 0.0), jnp.where(h0, 0.0, a[0])], axis=0).astype(BF16) for a in args]
    kk = [jnp.concatenate([a[1], a[2]], axis=0).astype(BF16) for a in args]
    s = [_dot_nt(qs[i], kk[i]) for i in range(n)]
    s = [s[i] + jnp.concatenate([args[i][8], args[i][8]], axis=0) for i in range(n)]
    mn = [jnp.maximum(args[i][5], jnp.max(s[i], axis=1, keepdims=True)) for i in range(n)]
    p = [jnp.exp(s[i] - mn[i]).astype(BF16) for i in range(n)]
    vv = [jnp.concatenate([a[3], a[4]], axis=0) for a in args]
    v0 = [jnp.where(hk, vv[i], 1.0).astype(BF16) for i in range(n)]
    v1 = [jnp.where(hk, 1.0, vv[i]).astype(BF16) for i in range(n)]
    o0 = [_dot(p[i][:ATT_BLK], v0[i]) for i in range(n)]
    o1 = [_dot(p[i][ATT_BLK:], v1[i]) for i in range(n)]
    al = [jnp.exp(args[i][5] - mn[i]) for i in range(n)]
    a0 = [jnp.broadcast_to(al[i][:ATT_BLK], (ATT_BLK, LANES)) for i in range(n)]
    a1 = [jnp.broadcast_to(al[i][ATT_BLK:], (ATT_BLK, LANES)) for i in range(n)]
    ln = [jnp.where(h0, a1[i], a0[i]) * args[i][6] + jnp.where(h0, o1[i], o0[i]) for i in range(n)]
    an = [jnp.where(h0, a0[i], a1[i]) * args[i][7] + jnp.where(h0, o0[i], o1[i]) for i in range(n)]
    return [(mn[i], ln[i], an[i]) for i in range(n)]


def _attn_prompt_body(q_ref, k_ref, v_ref, o_ref, kw_ref, vw_ref, qr, kr, vr, m_sc, l_sc, acc_sc, bias_sc, *, seq):
    ni = seq // RES
    scale = HD_ATT ** -0.5
    win = kw_ref.shape[0]
    kw_ref[...] = k_ref[seq - win:, :]
    vw_ref[...] = v_ref[seq - win:, :]
    for j in range(RES):
        qr[j] = q_ref[pl.ds(j, ni, stride=RES), :] * scale
        kr[j] = k_ref[pl.ds(j, ni, stride=RES), :]
        vr[j] = v_ref[pl.ds(j, ni, stride=RES), :]

    rq = lax.broadcasted_iota(I32, (ATT_BLK, 2 * ATT_BLK), 0)
    cc = lax.broadcasted_iota(I32, (ATT_BLK, 2 * ATT_BLK), 1)
    is_cur = cc >= ATT_BLK
    rk = jnp.where(is_cur, cc - ATT_BLK, cc)
    for br, (_, dil) in enumerate(BRANCHES):
        nseg = RES // dil
        seg = ATT_BLK // nseg
        oq = nseg * (rq % seg) + rq // seg
        ok = nseg * (rk % seg) + rk // seg
        allowed = (is_cur & (ok <= oq)) | (jnp.logical_not(is_cur) & (ok >= oq))
        bias_sc[2 * br + 1] = jnp.where(allowed, 0.0, NEG)
        bias_sc[2 * br] = jnp.where(allowed & is_cur, 0.0, NEG)

    def run_block(br, rows, qoff, poff, has_prev):
        seg = ATT_BLK // len(rows)
        q = jnp.concatenate([qr[r, pl.ds(qoff, seg), :] for r in rows], axis=0)
        kc = jnp.concatenate([kr[r, pl.ds(qoff, seg), :] for r in rows], axis=0)
        vc = jnp.concatenate([vr[r, pl.ds(qoff, seg), :] for r in rows], axis=0)
        kp = jnp.concatenate([kr[r, pl.ds(poff, seg), :] for r in rows], axis=0)
        vp = jnp.concatenate([vr[r, pl.ds(poff, seg), :] for r in rows], axis=0)
        return q, kp, kc, vp, vc

    def super_block(u, carry):
        base = pl.multiple_of(u * ATT_BLK, ATT_BLK)
        m_sc[...] = jnp.full(m_sc.shape, NEG, F32)
        l_sc[...] = jnp.zeros(l_sc.shape, F32)
        acc_sc[...] = jnp.zeros(acc_sc.shape, F32)

        def branch_steps(br, specs):
            loaded = []
            for rows, loc, seg in specs:
                qoff = pl.multiple_of(base + loc, 8)
                poff = pl.multiple_of(jnp.maximum(base + loc - seg, 0), 8)
                has_prev = (base + loc >= seg).astype(I32)
                q, kp, kc, vp, vc = run_block(br, rows, qoff, poff, has_prev)
                bias = bias_sc[2 * br + has_prev]
                sl = pl.ds(pl.multiple_of(loc, 8), seg)
                mo = jnp.concatenate([m_sc[h, r, sl, :] for h in range(2) for r in rows], axis=0)
                lo = jnp.concatenate([l_sc[r, sl, :] for r in rows], axis=0)
                ao = jnp.concatenate([acc_sc[r, sl, :] for r in rows], axis=0)
                loaded.append((q, kp, kc, vp, vc, mo, lo, ao, bias))
            results = _attn_blocks(loaded)
            for (rows, loc, seg), (mn, ln, an) in zip(specs, results):
                sl = pl.ds(pl.multiple_of(loc, 8), seg)
                for h in range(2):
                    for si, r in enumerate(rows):
                        lo_r = h * ATT_BLK + si * seg
                        m_sc[h, r, sl, :] = mn[lo_r:lo_r + seg]
                for si, r in enumerate(rows):
                    acc_sc[r, sl, :] = an[si * seg:(si + 1) * seg]
                    l_sc[r, sl, :] = ln[si * seg:(si + 1) * seg]

        nb = ATT_BLOCKS_PER_STEP

        def b1(n, c):
            branch_steps(0, [(list(range(RES)), (n * nb + uu) * 8, 8) for uu in range(nb)])
            return c
        lax.fori_loop(0, RES // nb, b1, 0)

        def b4(t, c):
            specs = []
            for uu in range(nb):
                tt = t * nb + uu
                specs.append(([tt // 4 + 4 * m for m in range(4)], (tt % 4) * 32, 32))
            branch_steps(1, specs)
            return c
        lax.fori_loop(0, RES // nb, b4, 0)

        def b16(j, c):
            branch_steps(2, [([j * nb + uu], 0, ATT_BLK) for uu in range(nb)])
            return c
        lax.fori_loop(0, RES // nb, b16, 0)

        for j in range(RES):
            den = pltpu.roll(l_sc[j], HD_ATT, axis=1)
            o_ref[pl.ds(u * SUPER + j, ATT_BLK, stride=RES), :] = acc_sc[j] / den
        return carry

    lax.fori_loop(0, seq // SUPER, super_block, 0)


def _attn_prompt(proj, batch, seq, win, col_q, col_k, col_v, n_pairs):
    ni = seq // RES
    body = functools.partial(_attn_prompt_body, seq=seq)

    def spec(col):
        return pl.BlockSpec((seq, LANES), lambda b, p: (b, col + p))

    wspec = pl.BlockSpec((win, LANES), lambda b, p: (b, p))
    wshape = jax.ShapeDtypeStruct((batch * win, n_pairs * LANES), F32)
    return pl.pallas_call(
        body,
        grid=(batch, n_pairs),
        in_specs=[spec(col_q), spec(col_k), spec(col_v)],
        out_specs=[pl.BlockSpec((seq, LANES), lambda b, p: (b, p)), wspec, wspec],
        out_shape=[jax.ShapeDtypeStruct((batch * seq, n_pairs * LANES), F32), wshape, wshape],
        scratch_shapes=[
            pltpu.VMEM((RES, ni, LANES), F32),
            pltpu.VMEM((RES, ni, LANES), F32),
            pltpu.VMEM((RES, ni, LANES), F32),
            pltpu.VMEM((2, RES, ATT_BLK, 1), F32),
            pltpu.VMEM((RES, ATT_BLK, LANES), F32),
            pltpu.VMEM((RES, ATT_BLK, LANES), F32),
            pltpu.VMEM((2 * len(BRANCHES), ATT_BLK, 2 * ATT_BLK), F32),
        ],
        compiler_params=_cparams(("parallel", "parallel"), 60),
        name="attn_prompt",
    )(proj, proj, proj)


def _attn_sample_body(q_ref, kn_ref, vn_ref, kc_ref, vc_ref, o_ref, *, dq, nbuf):
    b = pl.program_id(0)
    rows = pl.ds(b * dq, dq)
    q = q_ref[rows, :] * (HD_ATT ** -0.5)
    lane = lax.broadcasted_iota(I32, (dq, LANES), 1)
    h0 = lane < HD_ATT
    qs = jnp.concatenate([jnp.where(h0, q, 0.0), jnp.where(h0, 0.0, q)], axis=0).astype(BF16)
    zpad = jnp.zeros((8 - dq, LANES), F32)
    kn = jnp.concatenate([kn_ref[rows, :], zpad], axis=0).astype(BF16)
    vn = jnp.concatenate([vn_ref[rows, :], zpad], axis=0).astype(BF16)
    kc = kc_ref[0].astype(BF16)
    vc = vc_ref[0].astype(BF16)
    s_c = _dot_nt(qs, kc)
    s_n = _dot_nt(qs, kn)

    def counts(delta):
        cnt = jnp.zeros(delta.shape, F32)
        for win, dil in BRANCHES:
            hit = (delta >= 0) & (delta <= win) & (delta % dil == 0)
            cnt = cnt + hit.astype(F32)
        return cnt

    qi_c = lax.broadcasted_iota(I32, s_c.shape, 0) % dq
    cnt_c = counts(nbuf + qi_c - lax.broadcasted_iota(I32, s_c.shape, 1))
    qi_n = lax.broadcasted_iota(I32, s_n.shape, 0) % dq
    col_n = lax.broadcasted_iota(I32, s_n.shape, 1)
    cnt_n = jnp.where(col_n < dq, counts(qi_n - col_n), 0.0)
    s_c = jnp.where(cnt_c > 0, s_c, NEG)
    s_n = jnp.where(cnt_n > 0, s_n, NEG)
    m = jnp.maximum(jnp.max(s_c, axis=1, keepdims=True), jnp.max(s_n, axis=1, keepdims=True))
    e_c = cnt_c * jnp.exp(s_c - m)
    e_n = cnt_n * jnp.exp(s_n - m)
    den = jnp.sum(e_c, axis=1, keepdims=True) + jnp.sum(e_n, axis=1, keepdims=True)
    num = _dot(e_c.astype(BF16), vc) + _dot(e_n.astype(BF16), vn)
    out = num / den
    o_ref[0] = jnp.where(h0, out[:dq], out[dq:])


def _attn_sample(proj, cache_k, cache_v, row_blk, col_q, col_k, col_v, n_pairs, dec_batch, dq):
    nbuf = cache_k.shape[1]
    body = functools.partial(_attn_sample_body, dq=dq, nbuf=nbuf)
    nrows = dec_batch * dq

    def pspec(col):
        return pl.BlockSpec((nrows, LANES), lambda b, p: (row_blk, col + p))

    cspec = pl.BlockSpec((1, nbuf, LANES), lambda b, p: (b, 0, p))
    return pl.pallas_call(
        body,
        grid=(dec_batch, n_pairs),
        in_specs=[pspec(col_q), pspec(col_k), pspec(col_v), cspec, cspec],
        out_specs=pl.BlockSpec((1, dq, LANES), lambda b, p: (b, 0, p)),
        out_shape=jax.ShapeDtypeStruct((dec_batch, dq, n_pairs * LANES), F32),
        compiler_params=_cparams(("parallel", "parallel"), 32),
        name="attn_sample",
    )(proj, proj, proj, cache_k, cache_v)


def _ssd_body(xbc_ref, z_ref, dt_ref, cprev_ref, h0_ref, cw_ref, cb_ref, dtb_ref, alog_ref, dskx_ref,
              nw_ref, e_ref, r_ref, y_ref, cnew_ref, hout_ref, h_sc, xf_sc, *, nvalid, d_ssm, n_heads):
    L = CHUNK
    c = pl.program_id(1)
    gw = d_ssm // SSM_GROUPS
    hpg = n_heads // SSM_GROUPS

    @pl.when(c == 0)
    def _():
        h_sc[...] = h0_ref[0]
        xf_sc[0:8] = cprev_ref[0]

    if nvalid < L:
        xf_sc[8:8 + L] = jnp.zeros((L, xf_sc.shape[1]), F32)
        xf_sc[8:8 + nvalid] = xbc_ref[0]
        zpad = jnp.zeros((L - nvalid, LANES), F32)
        dt_raw = jnp.concatenate([dt_ref[0], zpad], axis=0)
    else:
        xf_sc[8:8 + L] = xbc_ref[...]
        dt_raw = dt_ref[...]
    conv = cb_ref[...] + cw_ref[0:1, :] * xf_sc[5:5 + L]
    for w in range(1, CONV_W):
        conv = conv + cw_ref[w:w + 1, :] * xf_sc[5 + w:5 + w + L]
    cnew_ref[0] = xf_sc[nvalid:nvalid + 8]
    xf_sc[0:8] = xf_sc[L:L + 8]

    row = lax.broadcasted_iota(I32, (L, LANES), 0)
    lane = lax.broadcasted_iota(I32, (L, LANES), 1)
    if nvalid < L:
        rowc = lax.broadcasted_iota(I32, conv.shape, 0)
        conv = jnp.where(rowc < nvalid, conv, 0.0)
    act = conv * jax.nn.sigmoid(conv)
    xs = act[:, :d_ssm]
    bm = act[:, d_ssm:d_ssm + SSM_GROUPS * D_STATE].astype(BF16)
    cm = act[:, d_ssm + SSM_GROUPS * D_STATE:].astype(BF16)

    dt = jax.nn.softplus(dt_raw + dtb_ref[...])
    live = lane < n_heads
    if nvalid < L:
        live = live & (row < nvalid)
    dt = jnp.where(live, dt, 0.0)
    dta = dt * (-jnp.exp(alog_ref[...]))
    tril = row >= lane
    acum = jnp.dot(tril.astype(F32), dta, precision=lax.Precision.HIGHEST, preferred_element_type=F32)
    last = acum[L - 1:L, :]
    acum_t = acum.T
    eb = e_ref[...]
    stacked = jnp.concatenate([jnp.exp(last - acum) * dt, jnp.exp(acum), dt], axis=0)
    ex = _split_dot(stacked, eb)
    xw = (xs * ex[0:L]).astype(BF16)
    eac_x = ex[L:2 * L]
    xd = xs * ex[2 * L:3 * L]

    cd_col = jnp.broadcast_to(jnp.exp(acum_t[:, L - 1:L]), (LANES, LANES))
    cd_rows = _split_dot_left(r_ref[...], cd_col)

    h0l = lane < SSM_HD
    y_parts = []
    yoff_parts = []
    for g in range(SSM_GROUPS):
        bg = bm[:, g * D_STATE:(g + 1) * D_STATE]
        cg = cm[:, g * D_STATE:(g + 1) * D_STATE]
        cbg = _dot_nt(cg, bg)
        rows = slice(g * gw, (g + 1) * gw)
        hg = h_sc[rows, :]
        yoff_parts.append(_dot_nt(cg, hg.astype(BF16)))
        st = lax.dot_general(xw[:, rows], bg, (((0,), (0,)), ((), ())), preferred_element_type=F32)
        h_sc[rows, :] = cd_rows[rows, :] * hg + st
        for jp in range(hpg // 2):
            ms = []
            for hh in range(2):
                h = g * hpg + 2 * jp + hh
                diff = acum[:, h:h + 1] - acum_t[h:h + 1, :]
                ms.append(cbg * jnp.exp(jnp.where(tril, diff, NEG)))
            lhs = jnp.concatenate(ms, axis=1).astype(BF16)
            col = (g * hpg + 2 * jp) * SSM_HD
            xp = xd[:, col:col + LANES]
            rhs = jnp.concatenate([jnp.where(h0l, xp, 0.0), jnp.where(h0l, 0.0, xp)], axis=0).astype(BF16)
            y_parts.append(_dot(lhs, rhs))
    y = jnp.concatenate(y_parts, axis=1) + jnp.concatenate(yoff_parts, axis=1) * eac_x + dskx_ref[...] * xs
    if nvalid < L:
        y = y[:8]
        z = jnp.concatenate([z_ref[0], jnp.zeros((8 - nvalid, d_ssm), F32)], axis=0)
    else:
        z = z_ref[...]
    y = y * (z * jax.nn.sigmoid(z))
    outs = []
    for g in range(SSM_GROUPS):
        yg = y[:, g * gw:(g + 1) * gw]
        outs.append(yg * lax.rsqrt(jnp.mean(yg * yg, axis=-1, keepdims=True) + EPS))
    y = jnp.concatenate(outs, axis=1) * nw_ref[...]
    if nvalid < L:
        y_ref[0] = y[:nvalid].astype(y_ref.dtype)
    else:
        y_ref[...] = y.astype(y_ref.dtype)
    hout_ref[0] = h_sc[...]


def _split_dot_left(onehot_bf16, v):
    p1 = v.astype(BF16)
    r1 = v - p1.astype(F32)
    p2 = r1.astype(BF16)
    p3 = (r1 - p2.astype(F32)).astype(BF16)
    return _dot(onehot_bf16, p1) + _dot(onehot_bf16, p2) + _dot(onehot_bf16, p3)


def _ssd(xbc_src, z_src, dt_src, conv_prev8, h0, consts, *, batch, n_chunks, nvalid, xbc_spec, z_spec, dt_spec,
         d_ssm, n_heads):
    conv_dim = conv_prev8.shape[-1]
    hp = h0.shape[1]
    body = functools.partial(_ssd_body, nvalid=nvalid, d_ssm=d_ssm, n_heads=n_heads)
    cw, cb, dtb, alog, dskx, nw, e_mat, r_mat = consts

    def const(a):
        return pl.BlockSpec(a.shape, lambda b, c: (0,) * a.ndim)

    if nvalid < CHUNK:
        assert n_chunks == 1 and nvalid <= 8
        y_spec = pl.BlockSpec((1, nvalid, d_ssm), lambda b, c: (b, 0, 0))
        y_shape = jax.ShapeDtypeStruct((batch, nvalid, d_ssm), BF16)
    else:
        y_spec = pl.BlockSpec((CHUNK, d_ssm), lambda b, c: (b * n_chunks + c, 0))
        y_shape = jax.ShapeDtypeStruct((batch * n_chunks * CHUNK, d_ssm), BF16)
    return pl.pallas_call(
        body,
        grid=(batch, n_chunks),
        in_specs=[
            xbc_spec, z_spec, dt_spec,
            pl.BlockSpec((1, 8, conv_dim), lambda b, c: (b, 0, 0)),
            pl.BlockSpec((1, hp, D_STATE), lambda b, c: (b, 0, 0)),
            const(cw), const(cb), const(dtb), const(alog), const(dskx), const(nw), const(e_mat), const(r_mat),
        ],
        out_specs=[
            y_spec,
            pl.BlockSpec((1, 8, conv_dim), lambda b, c: (b, 0, 0)),
            pl.BlockSpec((1, hp, D_STATE), lambda b, c: (b, 0, 0)),
        ],
        out_shape=[
            y_shape,
            jax.ShapeDtypeStruct((batch, 8, conv_dim), F32),
            jax.ShapeDtypeStruct((batch, hp, D_STATE), F32),
        ],
        scratch_shapes=[pltpu.VMEM((hp, D_STATE), F32), pltpu.VMEM((CHUNK + 8, conv_dim), F32)],
        compiler_params=_cparams(("parallel", "arbitrary"), 48),
        name="ssd",
    )(xbc_src, z_src, dt_src, conv_prev8, h0, cw, cb, dtb, alog, dskx, nw, e_mat, r_mat)


def _outproj_body(x_ref, ya_ref, ys_ref, wa_ref, ws_ref, nfw_ref, wr_ref, br_ref,
                  xmid_ref, xpk_ref, idx_ref, gate_ref):
    tm, d = x_ref.shape
    pr = tm // OUTPROJ_PARTS
    parts = range(OUTPROJ_PARTS)
    rs = [pl.ds(i * pr, pr) for i in parts]
    mix = [_dot(ya_ref[r, :].astype(BF16), wa_ref[...]) + _dot(ys_ref[r, :], ws_ref[...]) for r in rs]
    xm = [x_ref[rs[i], :] + mix[i] for i in parts]
    for i in parts:
        xmid_ref[rs[i], :] = xm[i]
    xn = [xm[i] * lax.rsqrt(jnp.mean(xm[i] * xm[i], axis=-1, keepdims=True) + EPS) * nfw_ref[...] for i in parts]
    xh = [xn[i].astype(BF16) for i in parts]
    xhf = [xh[i].astype(F32) for i in parts]
    for i in parts:
        bits = lax.bitcast_convert_type(xhf[i], U32)
        words = (bits[:, d // 2:] & jnp.uint32(0xFFFF0000)) | (bits[:, :d // 2] >> 16)
        for s in range(PACK_ROWS):
            xpk_ref[pl.ds(i * pr * PACK_ROWS + s, pr, stride=PACK_ROWS), :] = words[:, s * LANES:(s + 1) * LANES]
    xl = [(xn[i] - xhf[i]).astype(BF16) for i in parts]
    hl = [_dot(xh[i], wr_ref[...]) for i in parts]
    lh = [_dot(xl[i], wr_ref[:, :LANES]) for i in parts]
    logits = [hl[i][:, :LANES] + hl[i][:, LANES:] + lh[i] + br_ref[...] for i in parts]
    lane = lax.broadcasted_iota(I32, (pr, LANES), 1)
    idx_out = [jnp.zeros((pr, LANES), I32) for _ in parts]
    val_out = [jnp.full((pr, LANES), NEG, F32) for _ in parts]
    for k in range(TOP_K):
        mx = [jnp.max(logits[i], axis=1, keepdims=True) for i in parts]
        ix = [jnp.min(jnp.where(logits[i] == mx[i], lane, LANES), axis=1, keepdims=True) for i in parts]
        idx_out = [jnp.where(lane == k, ix[i], idx_out[i]) for i in parts]
        val_out = [jnp.where(lane == k, mx[i], val_out[i]) for i in parts]
        logits = [jnp.where(lane == ix[i], -jnp.inf, logits[i]) for i in parts]
    for i in parts:
        ev = jnp.exp(val_out[i] - jnp.max(val_out[i], axis=1, keepdims=True))
        gate_ref[rs[i], :] = ev / jnp.sum(ev, axis=1, keepdims=True)
        idx_ref[rs[i], :] = idx_out[i]


def _outproj(x_all, y_att, y_ssd, w_att, w_ssd, nfw, w_router, b_router):
    m, d = x_all.shape
    assert d == 2 * PACK_ROWS * LANES
    tm = _row_tile(m, TM_PROJ)

    def rows(width):
        return pl.BlockSpec((tm, width), lambda i: (i, 0))

    def const(a):
        return pl.BlockSpec(a.shape, lambda i: (0,) * a.ndim)

    return pl.pallas_call(
        _outproj_body,
        grid=(m // tm,),
        in_specs=[rows(d), rows(y_att.shape[1]), rows(y_ssd.shape[1]), const(w_att), const(w_ssd), const(nfw),
                  const(w_router), const(b_router)],
        out_specs=[rows(d), pl.BlockSpec((tm * PACK_ROWS, LANES), lambda i: (i, 0)), rows(LANES), rows(LANES)],
        out_shape=[jax.ShapeDtypeStruct((m, d), F32), jax.ShapeDtypeStruct((m * PACK_ROWS, LANES), U32),
                   jax.ShapeDtypeStruct((m, LANES), I32), jax.ShapeDtypeStruct((m, LANES), F32)],
        compiler_params=_cparams(("parallel",), 56),
        name="outproj",
    )(x_all, y_att, y_ssd, w_att, w_ssd, nfw, w_router, b_router)


def _route_body(idx_ref, cin_ref, rank_ref, cout_ref, carry):
    @pl.when(pl.program_id(0) == 0)
    def _():
        carry[...] = cin_ref[...]

    tr = idx_ref.shape[0]
    lane = lax.broadcasted_iota(I32, (tr, LANES), 1)
    tril = (lax.broadcasted_iota(I32, (tr, tr), 0) >= lax.broadcasted_iota(I32, (tr, tr), 1))
    tril = jnp.where(tril, 1.0, 0.0).astype(BF16)
    idx = idx_ref[...]
    cnt = carry[...]
    out = jnp.zeros((tr, LANES), I32)
    for k in range(TOP_K):
        hit = lane == idx[:, k:k + 1]
        pre = _dot(tril, jnp.where(hit, 1.0, 0.0).astype(BF16))
        rk = jnp.sum(jnp.where(hit, pre + cnt - 1.0, 0.0), axis=1, keepdims=True)
        out = jnp.where(lane == k, rk.astype(I32), out)
        cnt = cnt + pre[tr - 1:tr, :]
    carry[...] = cnt
    rank_ref[...] = out
    cout_ref[...] = cnt


def _route(top_idx, counts_in):
    m = top_idx.shape[0]
    tr = _row_tile(m, TM_ROUTE)
    return pl.pallas_call(
        _route_body,
        grid=(m // tr,),
        in_specs=[pl.BlockSpec((tr, LANES), lambda i: (i, 0)), pl.BlockSpec((1, LANES), lambda i: (0, 0))],
        out_specs=[pl.BlockSpec((tr, LANES), lambda i: (i, 0)), pl.BlockSpec((1, LANES), lambda i: (0, 0))],
        out_shape=[jax.ShapeDtypeStruct((m, LANES), I32), jax.ShapeDtypeStruct((1, LANES), F32)],
        scratch_shapes=[pltpu.VMEM((1, LANES), F32)],
        compiler_params=_cparams(("arbitrary",), 32),
        name="route",
    )(top_idx, counts_in)


def _dispatch_body(pos_ref, x_ref, xd_in, xd_hbm, sem):
    del xd_in
    i = pl.program_id(0)
    ts = x_ref.shape[0] // PACK_ROWS

    def row_copy(r, k):
        p = pos_ref[(i * ts + r) * TOP_K + k]
        src = x_ref.at[pl.ds(pl.multiple_of(r * PACK_ROWS, PACK_ROWS), PACK_ROWS), :]
        dst = xd_hbm.at[pl.ds(pl.multiple_of(p * PACK_ROWS, PACK_ROWS), PACK_ROWS), :]
        return pltpu.make_async_copy(src, dst, sem)

    def issue(r, c):
        for k in range(TOP_K):
            row_copy(r, k).start()
        return c
    lax.fori_loop(0, ts, issue, 0)

    def drain(r, c):
        for k in range(TOP_K):
            row_copy(r, k).wait()
        return c
    lax.fori_loop(0, ts, drain, 0)


def _dispatch(pos_flat, x_packed, x_disp):
    m = x_packed.shape[0] // PACK_ROWS
    ts = _row_tile(m, TM_DISPATCH)
    return pl.pallas_call(
        _dispatch_body,
        grid_spec=pltpu.PrefetchScalarGridSpec(
            num_scalar_prefetch=1,
            grid=(m // ts,),
            in_specs=[pl.BlockSpec((ts * PACK_ROWS, LANES), lambda i, pos: (i, 0)),
                      pl.BlockSpec(memory_space=pl.ANY)],
            out_specs=pl.BlockSpec(memory_space=pl.ANY),
            scratch_shapes=[pltpu.SemaphoreType.DMA(())],
        ),
        out_shape=jax.ShapeDtypeStruct(x_disp.shape, x_disp.dtype),
        input_output_aliases={2: 0},
        compiler_params=_cparams(("arbitrary",), 32),
        name="dispatch",
    )(pos_flat, x_packed, x_disp)


def _tile_changed(te_ref, i):
    return (i == 0) | (te_ref[i] != te_ref[jnp.maximum(i - 1, 0)])


def _gmm1_body(te_ref, nu_ref, x_ref, wg_ref, wu_ref, bg_ref, bu_ref, h_ref, wgb, wub):
    i = pl.program_id(1)

    @pl.when(_tile_changed(te_ref, i))
    def _():
        wgb[...] = wg_ref[0].astype(BF16)
        wub[...] = wu_ref[0].astype(BF16)

    @pl.when(i < nu_ref[0])
    def _():
        tm = x_ref.shape[0] // PACK_ROWS
        lo, hi = [], []
        for s in range(PACK_ROWS):
            w = x_ref[pl.ds(s, tm, stride=PACK_ROWS), :]
            lo.append(lax.bitcast_convert_type(w << 16, F32).astype(BF16))
            hi.append(lax.bitcast_convert_type(w & jnp.uint32(0xFFFF0000), F32).astype(BF16))
        x = jnp.concatenate(lo + hi, axis=1)
        gt = _dot(x, wgb[...]) + bg_ref[0]
        up = _dot(x, wub[...]) + bu_ref[0]
        gt = jnp.minimum(gt, SWIGLU_LIMIT)
        up = jnp.clip(up, -SWIGLU_LIMIT, SWIGLU_LIMIT)
        act = (up + 1.0) * gt * jax.nn.sigmoid(SWIGLU_ALPHA * gt)
        h_ref[...] = act.astype(h_ref.dtype)

    @pl.when(i >= nu_ref[0])
    def _():
        h_ref[...] = jnp.zeros(h_ref.shape, h_ref.dtype)


def _gmm2_body(te_ref, nu_ref, h_ref, wd_ref, bd_ref, y_ref, wdb):
    i = pl.program_id(1)

    @pl.when(_tile_changed(te_ref, i))
    def _():
        wdb[...] = wd_ref[0].astype(BF16)

    @pl.when(i < nu_ref[0])
    def _():
        y_ref[...] = _dot(h_ref[...], wdb[...]) + bd_ref[0]

    @pl.when(i >= nu_ref[0])
    def _():
        y_ref[...] = jnp.zeros(y_ref.shape, y_ref.dtype)


def _gmm_specs(d_in, tn):
    tm = TM_GMM

    def tile(i, nu):
        return jnp.minimum(i, nu[0] - 1)

    x_spec = pl.BlockSpec((tm, d_in), lambda j, i, te, nu: (tile(i, nu), 0))
    w_spec = pl.BlockSpec((1, d_in, tn), lambda j, i, te, nu: (te[tile(i, nu)], 0, j))
    b_spec = pl.BlockSpec((1, 1, tn), lambda j, i, te, nu: (te[tile(i, nu)], 0, j))
    o_spec = pl.BlockSpec((tm, tn), lambda j, i, te, nu: (i, j))
    return x_spec, w_spec, b_spec, o_spec


def _gmm1(tile_exp, n_used, x_disp, w_gate, w_up, b_gate, b_up):
    cap = x_disp.shape[0] // PACK_ROWS
    d, dff = w_gate.shape[1], w_gate.shape[2]
    tn = TN_GMM
    _, w_spec, b_spec, o_spec = _gmm_specs(d, tn)
    x_spec = pl.BlockSpec((TM_GMM * PACK_ROWS, LANES), lambda j, i, te, nu: (jnp.minimum(i, nu[0] - 1), 0))
    return pl.pallas_call(
        _gmm1_body,
        grid_spec=pltpu.PrefetchScalarGridSpec(
            num_scalar_prefetch=2,
            grid=(dff // tn, cap // TM_GMM),
            in_specs=[x_spec, w_spec, w_spec, b_spec, b_spec],
            out_specs=o_spec,
            scratch_shapes=[pltpu.VMEM((d, tn), BF16), pltpu.VMEM((d, tn), BF16)],
        ),
        out_shape=jax.ShapeDtypeStruct((cap, dff), BF16),
        compiler_params=_cparams(("arbitrary", "arbitrary"), 60),
        name="gmm1",
    )(tile_exp, n_used, x_disp, w_gate, w_up, b_gate, b_up)


def _gmm2(tile_exp, n_used, h_disp, w_down, b_down):
    cap, dff = h_disp.shape
    d = w_down.shape[2]
    tn = TN_GMM
    x_spec, w_spec, b_spec, o_spec = _gmm_specs(dff, tn)
    return pl.pallas_call(
        _gmm2_body,
        grid_spec=pltpu.PrefetchScalarGridSpec(
            num_scalar_prefetch=2,
            grid=(d // tn, cap // TM_GMM),
            in_specs=[x_spec, w_spec, b_spec],
            out_specs=o_spec,
            scratch_shapes=[pltpu.VMEM((dff, tn), BF16)],
        ),
        out_shape=jax.ShapeDtypeStruct((cap, d), F32),
        compiler_params=_cparams(("arbitrary", "arbitrary"), 48),
        name="gmm2",
    )(tile_exp, n_used, h_disp, w_down, b_down)


def _combine_body(pos_ref, y_hbm, xmid_ref, gate_ref, fw_ref, o_ref, buf, sem):
    i = pl.program_id(0)
    n = pl.num_programs(0)
    tc = TOK_COMBINE

    def row_copy(step, slot, r, k):
        p = pos_ref[(step * tc + r) * TOP_K + k]
        return pltpu.make_async_copy(y_hbm.at[pl.ds(p, 1), :], buf.at[slot, k, pl.ds(r, 1), :], sem.at[slot])

    def issue(step, slot):
        def body(r, c):
            for k in range(TOP_K):
                row_copy(step, slot, r, k).start()
            return c
        lax.fori_loop(0, tc, body, 0)

    @pl.when(i == 0)
    def _():
        issue(0, 0)

    @pl.when(i + 1 < n)
    def _():
        issue(i + 1, (i + 1) % 2)

    slot = i % 2

    def wait_body(r, c):
        for k in range(TOP_K):
            row_copy(i, slot, r, k).wait()
        return c
    lax.fori_loop(0, tc, wait_body, 0)

    gates = gate_ref[...]
    acc = xmid_ref[...]
    for k in range(TOP_K):
        acc = acc + gates[:, k:k + 1] * buf[slot, k]
    o_ref[...] = acc * lax.rsqrt(jnp.mean(acc * acc, axis=-1, keepdims=True) + EPS) * fw_ref[...]


def _combine(pos_flat, y_disp, x_mid, gates, final_w):
    m, d = x_mid.shape
    tc = TOK_COMBINE
    return pl.pallas_call(
        _combine_body,
        grid_spec=pltpu.PrefetchScalarGridSpec(
            num_scalar_prefetch=1,
            grid=(m // tc,),
            in_specs=[
                pl.BlockSpec(memory_space=pl.ANY),
                pl.BlockSpec((tc, d), lambda i, pos: (i, 0)),
                pl.BlockSpec((tc, LANES), lambda i, pos: (i, 0)),
                pl.BlockSpec((1, d), lambda i, pos: (0, 0)),
            ],
            out_specs=pl.BlockSpec((tc, d), lambda i, pos: (i, 0)),
            scratch_shapes=[pltpu.VMEM((2, TOP_K, tc, d), F32), pltpu.SemaphoreType.DMA((2,))],
        ),
        out_shape=jax.ShapeDtypeStruct((m, d), F32),
        compiler_params=_cparams(("arbitrary",), 32),
        name="combine",
    )(pos_flat, y_disp, x_mid, gates, final_w)


def _tile_layout(counts, n_assign, n_exp):
    tiles = (counts + TM_GMM - 1) // TM_GMM
    tile_end = jnp.cumsum(tiles)
    row_start = (tile_end - tiles) * TM_GMM
    n_tiles = -(-n_assign // TM_GMM) + n_exp
    n_used = tile_end[-1]
    tile_ids = jnp.minimum(jnp.arange(n_tiles, dtype=I32), n_used - 1)
    tile_exp = jnp.minimum(jnp.sum((tile_end[None, :] <= tile_ids[:, None]).astype(I32), axis=1), n_exp - 1)
    return row_start.astype(I32), tile_exp.astype(I32), n_used.reshape(1).astype(I32), n_tiles * TM_GMM


def _positions(top_idx, rank, row_start, n_exp):
    idx = top_idx[:, :TOP_K]
    onehot = idx[:, :, None] == jnp.arange(n_exp, dtype=I32)[None, None, :]
    base = jnp.sum(jnp.where(onehot, row_start[None, None, :], 0), axis=-1)
    return (base + rank[:, :TOP_K]).reshape(-1).astype(I32)


def kernel(x_prompt, x_sample, cache_attn_k, cache_attn_v, state_conv, state_ssm, norm_mix_w, w_in, conv_w, conv_b, dt_bias, a_log, d_skip, ssd_norm_w, w_out, norm_ffn_w, w_router, b_router, w_gate, b_gate, w_up, b_up, w_down, b_down, norm_final_w):
    depth = norm_mix_w.shape[0]
    assert depth == 1, "single-layer step"
    bp, seq, d = x_prompt.shape
    bs, dq, _ = x_sample.shape
    n_att_heads = cache_attn_k.shape[3]
    d_att = n_att_heads * HD_ATT
    n_pairs = d_att // LANES
    n_heads = a_log.shape[1]
    d_ssm = n_heads * SSM_HD
    conv_dim = conv_w.shape[2]
    n_exp = w_router.shape[2]
    n_prompt = bp * seq
    n_sample = bs * dq
    assert n_sample == CHUNK and seq % SUPER == 0 and n_heads <= LANES

    wl = w_in[0]
    o_q, o_k, o_v, o_z, o_x, o_dt = 0, d_att, 2 * d_att, 3 * d_att, 3 * d_att + d_ssm, 3 * d_att + d_ssm + conv_dim
    w_main = jnp.concatenate([wl[:, o_x:o_dt], wl[:, o_q:o_z], wl[:, o_z:o_x]], axis=1).astype(BF16)
    w_dt = jnp.pad(wl[:, o_dt:], ((0, 0), (0, LANES - n_heads))).astype(BF16)
    col_q = conv_dim // LANES
    col_k = col_q + n_pairs
    col_v = col_k + n_pairs
    w_att = w_out[0, :d_att].astype(BF16)
    w_ssd = w_out[0, d_att:].astype(BF16)
    w_rp = jnp.pad(w_router[0], ((0, 0), (0, LANES - n_exp)))
    w_rh = w_rp.astype(BF16)
    w_r = jnp.concatenate([w_rh, (w_rp - w_rh.astype(F32)).astype(BF16)], axis=1)
    b_r = jnp.pad(b_router[0], (0, LANES - n_exp), constant_values=NEG).reshape(1, LANES)
    pad_h = (0, LANES - n_heads)
    head_of_col = jnp.arange(d_ssm, dtype=I32) // SSM_HD
    e_mat = (jnp.arange(LANES, dtype=I32)[:, None] == head_of_col[None, :]).astype(BF16)
    r_mat = (head_of_col[:, None] == jnp.arange(LANES, dtype=I32)[None, :]).astype(BF16)
    ssd_consts = (
        conv_w[0], conv_b[0].reshape(1, conv_dim),
        jnp.pad(dt_bias[0], pad_h).reshape(1, LANES), jnp.pad(a_log[0], pad_h).reshape(1, LANES),
        jnp.repeat(d_skip[0], SSM_HD).reshape(1, d_ssm), ssd_norm_w[0].reshape(1, d_ssm), e_mat, r_mat,
    )

    xp = x_prompt.reshape(n_prompt, d)
    xs = x_sample.reshape(n_sample, d)
    nmw = norm_mix_w[0].reshape(1, d)
    proj_p, dt_p = _inproj(xp, nmw, w_main, w_dt)
    proj_s, dt_s = _inproj(xs, nmw, w_main, w_dt)
    n_main = proj_p.shape[1]
    z_col = (conv_dim + 3 * d_att) // d_ssm
    assert z_col * d_ssm == conv_dim + 3 * d_att

    win_p = min(BRANCHES[-1][0], seq)
    y_att_p, k_p, v_p = _attn_prompt(proj_p, bp, seq, win_p, col_q, col_k, col_v, n_pairs)
    ck = cache_attn_k[0].reshape(bs, -1, d_att)
    cv = cache_attn_v[0].reshape(bs, -1, d_att)
    y_att_s = _attn_sample(proj_s, ck, cv, 0, col_q, col_k, col_v, n_pairs, bs, dq)

    n_chunks = seq // CHUNK
    hp = n_heads * SSM_HD
    y_ssd_p, cnew_p, h_p = _ssd(
        proj_p, proj_p, dt_p, jnp.zeros((bp, 8, conv_dim), F32), jnp.zeros((bp, hp, D_STATE), F32), ssd_consts,
        batch=bp, n_chunks=n_chunks, nvalid=CHUNK,
        xbc_spec=pl.BlockSpec((CHUNK, conv_dim), lambda b, c: (b * n_chunks + c, 0)),
        z_spec=pl.BlockSpec((CHUNK, d_ssm), lambda b, c: (b * n_chunks + c, z_col)),
        dt_spec=pl.BlockSpec((CHUNK, LANES), lambda b, c: (b * n_chunks + c, 0)),
        d_ssm=d_ssm, n_heads=n_heads)

    proj_s3 = proj_s.reshape(bs, dq, n_main)
    cprev_s = jnp.pad(state_conv[0], ((0, 0), (8 - (CONV_W - 1), 0), (0, 0)))
    y_ssd_s, cnew_s, h_s = _ssd(
        proj_s3, proj_s3, dt_s.reshape(bs, dq, LANES), cprev_s, state_ssm[0].reshape(bs, hp, D_STATE), ssd_consts,
        batch=bs, n_chunks=1, nvalid=dq,
        xbc_spec=pl.BlockSpec((1, dq, conv_dim), lambda b, c: (b, 0, 0)),
        z_spec=pl.BlockSpec((1, dq, d_ssm), lambda b, c: (b, 0, z_col)),
        dt_spec=pl.BlockSpec((1, dq, LANES), lambda b, c: (b, 0, 0)),
        d_ssm=d_ssm, n_heads=n_heads)

    nfw = norm_ffn_w[0].reshape(1, d)
    xmid_p, xpk_p, idx_p, gates_p = _outproj(xp, y_att_p, y_ssd_p, w_att, w_ssd, nfw, w_r, b_r)
    xmid_s, xpk_s, idx_s, gates_s = _outproj(xs, y_att_s.reshape(n_sample, d_att),
                                             y_ssd_s.reshape(n_sample, d_ssm), w_att, w_ssd, nfw, w_r, b_r)

    rank_p, cnt_p = _route(idx_p, jnp.zeros((1, LANES), F32))
    rank_s, cnt_all = _route(idx_s, cnt_p)
    counts = cnt_all[0, :n_exp].astype(I32)
    n_assign = (n_prompt + n_sample) * TOP_K
    row_start, tile_exp, n_used, cap = _tile_layout(counts, n_assign, n_exp)
    pos_p = _positions(idx_p, rank_p, row_start, n_exp)
    pos_s = _positions(idx_s, rank_s, row_start, n_exp)
    x_disp = jnp.zeros((cap * PACK_ROWS, LANES), U32)
    x_disp = _dispatch(pos_p, xpk_p, x_disp)
    x_disp = _dispatch(pos_s, xpk_s, x_disp)
    h_disp = _gmm1(tile_exp, n_used, x_disp, w_gate[0], w_up[0],
                   b_gate[0].reshape(n_exp, 1, -1), b_up[0].reshape(n_exp, 1, -1))
    y_disp = _gmm2(tile_exp, n_used, h_disp, w_down[0], b_down[0].reshape(n_exp, 1, -1))
    fw = norm_final_w.reshape(1, d)
    y_p = _combine(pos_p, y_disp, xmid_p, gates_p, fw)
    y_s = _combine(pos_s, y_disp, xmid_s, gates_s, fw)

    k_p = k_p.reshape(1, bp, win_p, n_att_heads, HD_ATT)
    v_p = v_p.reshape(1, bp, win_p, n_att_heads, HD_ATT)
    k_s = proj_s3[:, :, col_k * LANES:col_v * LANES].reshape(1, bs, dq, n_att_heads, HD_ATT)
    v_s = proj_s3[:, :, col_v * LANES:col_v * LANES + d_att].reshape(1, bs, dq, n_att_heads, HD_ATT)
    return (
        y_p.reshape(bp, seq, d), y_s.reshape(bs, dq, d), k_p, v_p,
        cnew_p[:, 8 - (CONV_W - 1):][None], h_p.reshape(1, bp, n_heads, SSM_HD, D_STATE),
        k_s, v_s,
        cnew_s[:, 8 - (CONV_W - 1):][None], h_s.reshape(1, bs, n_heads, SSM_HD, D_STATE),
    )
```

```python
import functools

import jax
import jax.numpy as jnp
from jax import lax
from jax.experimental import pallas as pl
from jax.experimental.pallas import tpu as pltpu

F32 = jnp.float32
BF16 = jnp.bfloat16
I32 = jnp.int32
U32 = jnp.uint32

EPS = 1e-5
NEG = -1e30
LANES = 128
HD_ATT = 64
BRANCHES = ((128, 1), (512, 4), (2048, 16))
RES = 16
ATT_BLK = 128
ATT_BLOCKS_PER_STEP = 2
SUPER = RES * ATT_BLK
SSM_HD = 64
SSM_GROUPS = 4
D_STATE = 128
CONV_W = 4
CHUNK = 128
TOP_K = 4
SWIGLU_LIMIT = 7.0
SWIGLU_ALPHA = 1.702

PACK_ROWS = 8

TM_PROJ = 512
OUTPROJ_PARTS = 2
TN_PROJ = 2048
TM_ROUTE = 512
TM_DISPATCH = 512
TM_GMM = 512
TN_GMM = 1024
TOK_COMBINE = 128


def _row_tile(m, cap):
    t = cap - cap % LANES
    while m % t:
        t -= LANES
    return t


def _cparams(sem, vmem_mb):
    return pltpu.CompilerParams(dimension_semantics=sem, vmem_limit_bytes=vmem_mb << 20)


def _dot(a, b):
    return jnp.dot(a, b, preferred_element_type=F32)


def _dot_nt(a, b):
    return lax.dot_general(a, b, (((1,), (1,)), ((), ())), preferred_element_type=F32)


def _split_dot(v, onehot_bf16):
    hi = v.astype(BF16)
    lo = (v - hi.astype(F32)).astype(BF16)
    return _dot(hi, onehot_bf16) + _dot(lo, onehot_bf16)


def _pack_words(lo, hi):
    lo_bits = lax.bitcast_convert_type(lo.astype(F32), U32)
    hi_bits = lax.bitcast_convert_type(hi.astype(F32), U32)
    return (hi_bits & jnp.uint32(0xFFFF0000)) | (lo_bits >> 16)


def _store_packed(ref, rows, words, row0=0):
    for s in range(PACK_ROWS):
        ref[pl.ds(row0 * PACK_ROWS + s, rows, stride=PACK_ROWS), :] = words[:, s * LANES:(s + 1) * LANES]


def _load_packed(load_slab, dtype):
    lo, hi = [], []
    for s in range(PACK_ROWS):
        w = load_slab(s)
        lo.append(lax.bitcast_convert_type(w << 16, F32).astype(dtype))
        hi.append(lax.bitcast_convert_type(w & jnp.uint32(0xFFFF0000), F32).astype(dtype))
    return jnp.concatenate(lo + hi, axis=1)


def _inproj_body(x_ref, nw_ref, w_ref, wdt_ref, o_ref, dt_ref, h_sc):
    @pl.when(pl.program_id(1) == 0)
    def _():
        x = x_ref[...]
        ms = jnp.mean(x * x, axis=-1, keepdims=True)
        h = (x * lax.rsqrt(ms + EPS) * nw_ref[...]).astype(BF16)
        h_sc[...] = h
        dt_ref[...] = _dot(h, wdt_ref[...])

    o_ref[...] = _dot(h_sc[...], w_ref[...])


def _inproj(x_all, norm_w, w_main, w_dt):
    m, d = x_all.shape
    n = w_main.shape[1]
    tm, tn = _row_tile(m, TM_PROJ), TN_PROJ
    return pl.pallas_call(
        _inproj_body,
        grid=(m // tm, n // tn),
        in_specs=[
            pl.BlockSpec((tm, d), lambda i, j: (i, 0)),
            pl.BlockSpec((1, d), lambda i, j: (0, 0)),
            pl.BlockSpec((d, tn), lambda i, j: (0, j)),
            pl.BlockSpec((d, LANES), lambda i, j: (0, 0)),
        ],
        out_specs=[
            pl.BlockSpec((tm, tn), lambda i, j: (i, j)),
            pl.BlockSpec((tm, LANES), lambda i, j: (i, 0)),
        ],
        out_shape=[jax.ShapeDtypeStruct((m, n), F32), jax.ShapeDtypeStruct((m, LANES), F32)],
        scratch_shapes=[pltpu.VMEM((tm, d), BF16)],
        compiler_params=_cparams(("parallel", "arbitrary"), 48),
        name="inproj",
    )(x_all, norm_w, w_main, w_dt)


def _attn_blocks(args):
    lane = lax.broadcasted_iota(I32, (ATT_BLK, LANES), 1)
    h0 = lane < HD_ATT
    hk = lax.broadcasted_iota(I32, (2 * ATT_BLK, LANES), 1) < HD_ATT
    n = len(args)
    qs = [jnp.concatenate([jnp.where(h0, a[0], 0.0), jnp.where(h0, 0.0, a[0])], axis=0).astype(BF16) for a in args]
    kk = [jnp.concatenate([a[1], a[2]], axis=0).astype(BF16) for a in args]
    s = [_dot_nt(qs[i], kk[i]) for i in range(n)]
    s = [s[i] + jnp.concatenate([args[i][8], args[i][8]], axis=0) for i in range(n)]
    mn = [jnp.maximum(args[i][5], jnp.max(s[i], axis=1, keepdims=True)) for i in range(n)]
    p = [jnp.exp(s[i] - mn[i]).astype(BF16) for i in range(n)]
    vv = [jnp.concatenate([a[3], a[4]], axis=0) for a in args]
    v0 = [jnp.where(hk, vv[i], 1.0).astype(BF16) for i in range(n)]
    v1 = [jnp.where(hk, 1.0, vv[i]).astype(BF16) for i in range(n)]
    o0 = [_dot(p[i][:ATT_BLK], v0[i]) for i in range(n)]
    o1 = [_dot(p[i][ATT_BLK:], v1[i]) for i in range(n)]
    al = [jnp.exp(args[i][5] - mn[i]) for i in range(n)]
    a0 = [jnp.broadcast_to(al[i][:ATT_BLK], (ATT_BLK, LANES)) for i in range(n)]
    a1 = [jnp.broadcast_to(al[i][ATT_BLK:], (ATT_BLK, LANES)) for i in range(n)]
    ln = [jnp.where(h0, a1[i], a0[i]) * args[i][6] + jnp.where(h0, o1[i], o0[i]) for i in range(n)]
    an = [jnp.where(h0, a0[i], a1[i]) * args[i][7] + jnp.where(h0, o0[i], o1[i]) for i in range(n)]
    return [(mn[i], ln[i], an[i]) for i in range(n)]


def _attn_prompt_body(q_ref, k_ref, v_ref, o_ref, kw_ref, vw_ref, qr, kr, vr, m_sc, l_sc, acc_sc, bias_sc, *, seq):
    ni = seq // RES
    scale = HD_ATT ** -0.5
    win = kw_ref.shape[0]
    kw_ref[...] = k_ref[seq - win:, :]
    vw_ref[...] = v_ref[seq - win:, :]
    for j in range(RES):
        qr[j] = q_ref[pl.ds(j, ni, stride=RES), :] * scale
        kr[j] = k_ref[pl.ds(j, ni, stride=RES), :]
        vr[j] = v_ref[pl.ds(j, ni, stride=RES), :]

    rq = lax.broadcasted_iota(I32, (ATT_BLK, 2 * ATT_BLK), 0)
    cc = lax.broadcasted_iota(I32, (ATT_BLK, 2 * ATT_BLK), 1)
    is_cur = cc >= ATT_BLK
    rk = jnp.where(is_cur, cc - ATT_BLK, cc)
    for br, (_, dil) in enumerate(BRANCHES):
        nseg = RES // dil
        seg = ATT_BLK // nseg
        oq = nseg * (rq % seg) + rq // seg
        ok = nseg * (rk % seg) + rk // seg
        allowed = (is_cur & (ok <= oq)) | (jnp.logical_not(is_cur) & (ok >= oq))
        bias_sc[2 * br + 1] = jnp.where(allowed, 0.0, NEG)
        bias_sc[2 * br] = jnp.where(allowed & is_cur, 0.0, NEG)

    def run_block(br, rows, qoff, poff, has_prev):
        seg = ATT_BLK // len(rows)
        q = jnp.concatenate([qr[r, pl.ds(qoff, seg), :] for r in rows], axis=0)
        kc = jnp.concatenate([kr[r, pl.ds(qoff, seg), :] for r in rows], axis=0)
        vc = jnp.concatenate([vr[r, pl.ds(qoff, seg), :] for r in rows], axis=0)
        kp = jnp.concatenate([kr[r, pl.ds(poff, seg), :] for r in rows], axis=0)
        vp = jnp.concatenate([vr[r, pl.ds(poff, seg), :] for r in rows], axis=0)
        return q, kp, kc, vp, vc

    def super_block(u, carry):
        base = pl.multiple_of(u * ATT_BLK, ATT_BLK)
        m_sc[...] = jnp.full(m_sc.shape, NEG, F32)
        l_sc[...] = jnp.zeros(l_sc.shape, F32)
        acc_sc[...] = jnp.zeros(acc_sc.shape, F32)

        def branch_steps(br, specs):
            loaded = []
            for rows, loc, seg in specs:
                qoff = pl.multiple_of(base + loc, 8)
                poff = pl.multiple_of(jnp.maximum(base + loc - seg, 0), 8)
                has_prev = (base + loc >= seg).astype(I32)
                q, kp, kc, vp, vc = run_block(br, rows, qoff, poff, has_prev)
                bias = bias_sc[2 * br + has_prev]
                sl = pl.ds(pl.multiple_of(loc, 8), seg)
                mo = jnp.concatenate([m_sc[h, r, sl, :] for h in range(2) for r in rows], axis=0)
                lo = jnp.concatenate([l_sc[r, sl, :] for r in rows], axis=0)
                ao = jnp.concatenate([acc_sc[r, sl, :] for r in rows], axis=0)
                loaded.append((q, kp, kc, vp, vc, mo, lo, ao, bias))
            results = _attn_blocks(loaded)
            for (rows, loc, seg), (mn, ln, an) in zip(specs, results):
                sl = pl.ds(pl.multiple_of(loc, 8), seg)
                for h in range(2):
                    for si, r in enumerate(rows):
                        lo_r = h * ATT_BLK + si * seg
                        m_sc[h, r, sl, :] = mn[lo_r:lo_r + seg]
                for si, r in enumerate(rows):
                    acc_sc[r, sl, :] = an[si * seg:(si + 1) * seg]
                    l_sc[r, sl, :] = ln[si * seg:(si + 1) * seg]

        nb = ATT_BLOCKS_PER_STEP

        def b1(n, c):
            branch_steps(0, [(list(range(RES)), (n * nb + uu) * 8, 8) for uu in range(nb)])
            return c
        lax.fori_loop(0, RES // nb, b1, 0)

        def b4(t, c):
            specs = []
            for uu in range(nb):
                tt = t * nb + uu
                specs.append(([tt // 4 + 4 * m for m in range(4)], (tt % 4) * 32, 32))
            branch_steps(1, specs)
            return c
        lax.fori_loop(0, RES // nb, b4, 0)

        def b16(j, c):
            branch_steps(2, [([j * nb + uu], 0, ATT_BLK) for uu in range(nb)])
            return c
        lax.fori_loop(0, RES // nb, b16, 0)

        for j in range(RES):
            den = pltpu.roll(l_sc[j], HD_ATT, axis=1)
            o_ref[pl.ds(u * SUPER + j, ATT_BLK, stride=RES), :] = acc_sc[j] / den
        return carry

    lax.fori_loop(0, seq // SUPER, super_block, 0)


def _attn_prompt(proj, batch, seq, win, col_q, col_k, col_v, n_pairs):
    ni = seq // RES
    body = functools.partial(_attn_prompt_body, seq=seq)

    def spec(col):
        return pl.BlockSpec((seq, LANES), lambda b, p: (b, col + p))

    wspec = pl.BlockSpec((win, LANES), lambda b, p: (b, p))
    wshape = jax.ShapeDtypeStruct((batch * win, n_pairs * LANES), F32)
    return pl.pallas_call(
        body,
        grid=(batch, n_pairs),
        in_specs=[spec(col_q), spec(col_k), spec(col_v)],
        out_specs=[pl.BlockSpec((seq, LANES), lambda b, p: (b, p)), wspec, wspec],
        out_shape=[jax.ShapeDtypeStruct((batch * seq, n_pairs * LANES), F32), wshape, wshape],
        scratch_shapes=[
            pltpu.VMEM((RES, ni, LANES), F32),
            pltpu.VMEM((RES, ni, LANES), F32),
            pltpu.VMEM((RES, ni, LANES), F32),
            pltpu.VMEM((2, RES, ATT_BLK, 1), F32),
            pltpu.VMEM((RES, ATT_BLK, LANES), F32),
            pltpu.VMEM((RES, ATT_BLK, LANES), F32),
            pltpu.VMEM((2 * len(BRANCHES), ATT_BLK, 2 * ATT_BLK), F32),
        ],
        compiler_params=_cparams(("parallel", "parallel"), 60),
        name="attn_prompt",
    )(proj, proj, proj)


def _attn_sample_body(q_ref, kn_ref, vn_ref, kc_ref, vc_ref, o_ref, *, dq, nbuf):
    b = pl.program_id(0)
    rows = pl.ds(b * dq, dq)
    q = q_ref[rows, :] * (HD_ATT ** -0.5)
    lane = lax.broadcasted_iota(I32, (dq, LANES), 1)
    h0 = lane < HD_ATT
    qs = jnp.concatenate([jnp.where(h0, q, 0.0), jnp.where(h0, 0.0, q)], axis=0).astype(BF16)
    zpad = jnp.zeros((8 - dq, LANES), F32)
    kn = jnp.concatenate([kn_ref[rows, :], zpad], axis=0).astype(BF16)
    vn = jnp.concatenate([vn_ref[rows, :], zpad], axis=0).astype(BF16)
    kc = kc_ref[0].astype(BF16)
    vc = vc_ref[0].astype(BF16)
    s_c = _dot_nt(qs, kc)
    s_n = _dot_nt(qs, kn)

    def counts(delta):
        cnt = jnp.zeros(delta.shape, F32)
        for win, dil in BRANCHES:
            hit = (delta >= 0) & (delta <= win) & (delta % dil == 0)
            cnt = cnt + hit.astype(F32)
        return cnt

    qi_c = lax.broadcasted_iota(I32, s_c.shape, 0) % dq
    cnt_c = counts(nbuf + qi_c - lax.broadcasted_iota(I32, s_c.shape, 1))
    qi_n = lax.broadcasted_iota(I32, s_n.shape, 0) % dq
    col_n = lax.broadcasted_iota(I32, s_n.shape, 1)
    cnt_n = jnp.where(col_n < dq, counts(qi_n - col_n), 0.0)
    s_c = jnp.where(cnt_c > 0, s_c, NEG)
    s_n = jnp.where(cnt_n > 0, s_n, NEG)
    m = jnp.maximum(jnp.max(s_c, axis=1, keepdims=True), jnp.max(s_n, axis=1, keepdims=True))
    e_c = cnt_c * jnp.exp(s_c - m)
    e_n = cnt_n * jnp.exp(s_n - m)
    den = jnp.sum(e_c, axis=1, keepdims=True) + jnp.sum(e_n, axis=1, keepdims=True)
    num = _dot(e_c.astype(BF16), vc) + _dot(e_n.astype(BF16), vn)
    out = num / den
    o_ref[0] = jnp.where(h0, out[:dq], out[dq:])


def _attn_sample(proj, cache_k, cache_v, row_blk, col_q, col_k, col_v, n_pairs, dec_batch, dq):
    nbuf = cache_k.shape[1]
    body = functools.partial(_attn_sample_body, dq=dq, nbuf=nbuf)
    nrows = dec_batch * dq

    def pspec(col):
        return pl.BlockSpec((nrows, LANES), lambda b, p: (row_blk, col + p))

    cspec = pl.BlockSpec((1, nbuf, LANES), lambda b, p: (b, 0, p))
    return pl.pallas_call(
        body,
        grid=(dec_batch, n_pairs),
        in_specs=[pspec(col_q), pspec(col_k), pspec(col_v), cspec, cspec],
        out_specs=pl.BlockSpec((1, dq, LANES), lambda b, p: (b, 0, p)),
        out_shape=jax.ShapeDtypeStruct((dec_batch, dq, n_pairs * LANES), F32),
        compiler_params=_cparams(("parallel", "parallel"), 32),
        name="attn_sample",
    )(proj, proj, proj, cache_k, cache_v)


def _ssd_body(xbc_ref, z_ref, dt_ref, cprev_ref, h0_ref, cw_ref, cb_ref, dtb_ref, alog_ref, dskx_ref,
              nw_ref, e_ref, r_ref, y_ref, cnew_ref, hout_ref, h_sc, xf_sc, *, nvalid, d_ssm, n_heads):
    L = CHUNK
    c = pl.program_id(1)
    gw = d_ssm // SSM_GROUPS
    hpg = n_heads // SSM_GROUPS

    @pl.when(c == 0)
    def _():
        h_sc[...] = h0_ref[0]
        xf_sc[0:8] = cprev_ref[0]

    if nvalid < L:
        xf_sc[8:8 + L] = jnp.zeros((L, xf_sc.shape[1]), F32)
        xf_sc[8:8 + nvalid] = xbc_ref[0]
        zpad = jnp.zeros((L - nvalid, LANES), F32)
        dt_raw = jnp.concatenate([dt_ref[0], zpad], axis=0)
    else:
        xf_sc[8:8 + L] = xbc_ref[...]
        dt_raw = dt_ref[...]
    conv = cb_ref[...] + cw_ref[0:1, :] * xf_sc[5:5 + L]
    for w in range(1, CONV_W):
        conv = conv + cw_ref[w:w + 1, :] * xf_sc[5 + w:5 + w + L]
    cnew_ref[0] = xf_sc[nvalid:nvalid + 8]
    xf_sc[0:8] = xf_sc[L:L + 8]

    row = lax.broadcasted_iota(I32, (L, LANES), 0)
    lane = lax.broadcasted_iota(I32, (L, LANES), 1)
    if nvalid < L:
        rowc = lax.broadcasted_iota(I32, conv.shape, 0)
        conv = jnp.where(rowc < nvalid, conv, 0.0)
    act = conv * jax.nn.sigmoid(conv)
    xs = act[:, :d_ssm]
    bm = act[:, d_ssm:d_ssm + SSM_GROUPS * D_STATE].astype(BF16)
    cm = act[:, d_ssm + SSM_GROUPS * D_STATE:].astype(BF16)

    dt = jax.nn.softplus(dt_raw + dtb_ref[...])
    live = lane < n_heads
    if nvalid < L:
        live = live & (row < nvalid)
    dt = jnp.where(live, dt, 0.0)
    dta = dt * (-jnp.exp(alog_ref[...]))
    tril = row >= lane
    acum = jnp.dot(tril.astype(F32), dta, precision=lax.Precision.HIGHEST, preferred_element_type=F32)
    last = acum[L - 1:L, :]
    acum_t = acum.T
    eb = e_ref[...]
    stacked = jnp.concatenate([jnp.exp(last - acum) * dt, jnp.exp(acum), dt], axis=0)
    ex = _split_dot(stacked, eb)
    xw = (xs * ex[0:L]).astype(BF16)
    eac_x = ex[L:2 * L]
    xd = xs * ex[2 * L:3 * L]

    cd_col = jnp.broadcast_to(jnp.exp(acum_t[:, L - 1:L]), (LANES, LANES))
    cd_rows = _split_dot_left(r_ref[...], cd_col)

    h0l = lane < SSM_HD
    y_parts = []
    yoff_parts = []
    for g in range(SSM_GROUPS):
        bg = bm[:, g * D_STATE:(g + 1) * D_STATE]
        cg = cm[:, g * D_STATE:(g + 1) * D_STATE]
        cbg = _dot_nt(cg, bg)
        rows = slice(g * gw, (g + 1) * gw)
        hg = h_sc[rows, :]
        yoff_parts.append(_dot_nt(cg, hg.astype(BF16)))
        st = lax.dot_general(xw[:, rows], bg, (((0,), (0,)), ((), ())), preferred_element_type=F32)
        h_sc[rows, :] = cd_rows[rows, :] * hg + st
        for jp in range(hpg // 2):
            ms = []
            for hh in range(2):
                h = g * hpg + 2 * jp + hh
                diff = acum[:, h:h + 1] - acum_t[h:h + 1, :]
                ms.append(cbg * jnp.exp(jnp.where(tril, diff, NEG)))
            lhs = jnp.concatenate(ms, axis=1).astype(BF16)
            col = (g * hpg + 2 * jp) * SSM_HD
            xp = xd[:, col:col + LANES]
            rhs = jnp.concatenate([jnp.where(h0l, xp, 0.0), jnp.where(h0l, 0.0, xp)], axis=0).astype(BF16)
            y_parts.append(_dot(lhs, rhs))
    y = jnp.concatenate(y_parts, axis=1) + jnp.concatenate(yoff_parts, axis=1) * eac_x + dskx_ref[...] * xs
    if nvalid < L:
        y = y[:8]
        z = jnp.concatenate([z_ref[0], jnp.zeros((8 - nvalid, d_ssm), F32)], axis=0)
    else:
        z = z_ref[...]
    y = y * (z * jax.nn.sigmoid(z))
    outs = []
    for g in range(SSM_GROUPS):
        yg = y[:, g * gw:(g + 1) * gw]
        outs.append(yg * lax.rsqrt(jnp.mean(yg * yg, axis=-1, keepdims=True) + EPS))
    y = jnp.concatenate(outs, axis=1) * nw_ref[...]
    if nvalid < L:
        y_ref[0] = y[:nvalid].astype(y_ref.dtype)
    else:
        y_ref[...] = y.astype(y_ref.dtype)
    hout_ref[0] = h_sc[...]


def _split_dot_left(onehot_bf16, v):
    p1 = v.astype(BF16)
    r1 = v - p1.astype(F32)
    p2 = r1.astype(BF16)
    p3 = (r1 - p2.astype(F32)).astype(BF16)
    return _dot(onehot_bf16, p1) + _dot(onehot_bf16, p2) + _dot(onehot_bf16, p3)


def _ssd(xbc_src, z_src, dt_src, conv_prev8, h0, consts, *, batch, n_chunks, nvalid, xbc_spec, z_spec, dt_spec,
         d_ssm, n_heads):
    conv_dim = conv_prev8.shape[-1]
    hp = h0.shape[1]
    body = functools.partial(_ssd_body, nvalid=nvalid, d_ssm=d_ssm, n_heads=n_heads)
    cw, cb, dtb, alog, dskx, nw, e_mat, r_mat = consts

    def const(a):
        return pl.BlockSpec(a.shape, lambda b, c: (0,) * a.ndim)

    if nvalid < CHUNK:
        assert n_chunks == 1 and nvalid <= 8
        y_spec = pl.BlockSpec((1, nvalid, d_ssm), lambda b, c: (b, 0, 0))
        y_shape = jax.ShapeDtypeStruct((batch, nvalid, d_ssm), BF16)
    else:
        y_spec = pl.BlockSpec((CHUNK, d_ssm), lambda b, c: (b * n_chunks + c, 0))
        y_shape = jax.ShapeDtypeStruct((batch * n_chunks * CHUNK, d_ssm), BF16)
    return pl.pallas_call(
        body,
        grid=(batch, n_chunks),
        in_specs=[
            xbc_spec, z_spec, dt_spec,
            pl.BlockSpec((1, 8, conv_dim), lambda b, c: (b, 0, 0)),
            pl.BlockSpec((1, hp, D_STATE), lambda b, c: (b, 0, 0)),
            const(cw), const(cb), const(dtb), const(alog), const(dskx), const(nw), const(e_mat), const(r_mat),
        ],
        out_specs=[
            y_spec,
            pl.BlockSpec((1, 8, conv_dim), lambda b, c: (b, 0, 0)),
            pl.BlockSpec((1, hp, D_STATE), lambda b, c: (b, 0, 0)),
        ],
        out_shape=[
            y_shape,
            jax.ShapeDtypeStruct((batch, 8, conv_dim), F32),
            jax.ShapeDtypeStruct((batch, hp, D_STATE), F32),
        ],
        scratch_shapes=[pltpu.VMEM((hp, D_STATE), F32), pltpu.VMEM((CHUNK + 8, conv_dim), F32)],
        compiler_params=_cparams(("parallel", "arbitrary"), 48),
        name="ssd",
    )(xbc_src, z_src, dt_src, conv_prev8, h0, cw, cb, dtb, alog, dskx, nw, e_mat, r_mat)


def _outproj_body(x_ref, ya_ref, ys_ref, wa_ref, ws_ref, nfw_ref, wr_ref, br_ref,
                  xmid_ref, xpk_ref, idx_ref, gate_ref):
    tm, d = x_ref.shape
    pr = tm // OUTPROJ_PARTS
    parts = range(OUTPROJ_PARTS)
    rs = [pl.ds(i * pr, pr) for i in parts]
    mix = [_dot(ya_ref[r, :].astype(BF16), wa_ref[...]) + _dot(ys_ref[r, :], ws_ref[...]) for r in rs]
    xm = [x_ref[rs[i], :] + mix[i] for i in parts]
    for i in parts:
        xmid_ref[rs[i], :] = xm[i]
    xn = [xm[i] * lax.rsqrt(jnp.mean(xm[i] * xm[i], axis=-1, keepdims=True) + EPS) * nfw_ref[...] for i in parts]
    xh = [xn[i].astype(BF16) for i in parts]
    xhf = [xh[i].astype(F32) for i in parts]
    for i in parts:
        _store_packed(xpk_ref, pr, _pack_words(xh[i][:, :d // 2], xh[i][:, d // 2:]), row0=i * pr)
    xl = [(xn[i] - xhf[i]).astype(BF16) for i in parts]
    hl = [_dot(xh[i], wr_ref[...]) for i in parts]
    lh = [_dot(xl[i], wr_ref[:, :LANES]) for i in parts]
    logits = [hl[i][:, :LANES] + hl[i][:, LANES:] + lh[i] + br_ref[...] for i in parts]
    lane = lax.broadcasted_iota(I32, (pr, LANES), 1)
    idx_out = [jnp.zeros((pr, LANES), I32) for _ in parts]
    val_out = [jnp.full((pr, LANES), NEG, F32) for _ in parts]
    for k in range(TOP_K):
        mx = [jnp.max(logits[i], axis=1, keepdims=True) for i in parts]
        ix = [jnp.min(jnp.where(logits[i] == mx[i], lane, LANES), axis=1, keepdims=True) for i in parts]
        idx_out = [jnp.where(lane == k, ix[i], idx_out[i]) for i in parts]
        val_out = [jnp.where(lane == k, mx[i], val_out[i]) for i in parts]
        logits = [jnp.where(lane == ix[i], -jnp.inf, logits[i]) for i in parts]
    for i in parts:
        ev = jnp.exp(val_out[i] - jnp.max(val_out[i], axis=1, keepdims=True))
        gate_ref[rs[i], :] = ev / jnp.sum(ev, axis=1, keepdims=True)
        idx_ref[rs[i], :] = idx_out[i]


def _outproj(x_all, y_att, y_ssd, w_att, w_ssd, nfw, w_router, b_router):
    m, d = x_all.shape
    assert d == 2 * PACK_ROWS * LANES
    tm = _row_tile(m, TM_PROJ)

    def rows(width):
        return pl.BlockSpec((tm, width), lambda i: (i, 0))

    def const(a):
        return pl.BlockSpec(a.shape, lambda i: (0,) * a.ndim)

    return pl.pallas_call(
        _outproj_body,
        grid=(m // tm,),
        in_specs=[rows(d), rows(y_att.shape[1]), rows(y_ssd.shape[1]), const(w_att), const(w_ssd), const(nfw),
                  const(w_router), const(b_router)],
        out_specs=[rows(d), pl.BlockSpec((tm * PACK_ROWS, LANES), lambda i: (i, 0)), rows(LANES), rows(LANES)],
        out_shape=[jax.ShapeDtypeStruct((m, d), F32), jax.ShapeDtypeStruct((m * PACK_ROWS, LANES), U32),
                   jax.ShapeDtypeStruct((m, LANES), I32), jax.ShapeDtypeStruct((m, LANES), F32)],
        compiler_params=_cparams(("parallel",), 56),
        name="outproj",
    )(x_all, y_att, y_ssd, w_att, w_ssd, nfw, w_router, b_router)


def _route_body(idx_ref, cin_ref, rank_ref, cout_ref, carry):
    @pl.when(pl.program_id(0) == 0)
    def _():
        carry[...] = cin_ref[...]

    tr = idx_ref.shape[0]
    lane = lax.broadcasted_iota(I32, (tr, LANES), 1)
    tril = (lax.broadcasted_iota(I32, (tr, tr), 0) >= lax.broadcasted_iota(I32, (tr, tr), 1))
    tril = jnp.where(tril, 1.0, 0.0).astype(BF16)
    idx = idx_ref[...]
    cnt = carry[...]
    out = jnp.zeros((tr, LANES), I32)
    for k in range(TOP_K):
        hit = lane == idx[:, k:k + 1]
        pre = _dot(tril, jnp.where(hit, 1.0, 0.0).astype(BF16))
        rk = jnp.sum(jnp.where(hit, pre + cnt - 1.0, 0.0), axis=1, keepdims=True)
        out = jnp.where(lane == k, rk.astype(I32), out)
        cnt = cnt + pre[tr - 1:tr, :]
    carry[...] = cnt
    rank_ref[...] = out
    cout_ref[...] = cnt


def _route(top_idx, counts_in):
    m = top_idx.shape[0]
    tr = _row_tile(m, TM_ROUTE)
    return pl.pallas_call(
        _route_body,
        grid=(m // tr,),
        in_specs=[pl.BlockSpec((tr, LANES), lambda i: (i, 0)), pl.BlockSpec((1, LANES), lambda i: (0, 0))],
        out_specs=[pl.BlockSpec((tr, LANES), lambda i: (i, 0)), pl.BlockSpec((1, LANES), lambda i: (0, 0))],
        out_shape=[jax.ShapeDtypeStruct((m, LANES), I32), jax.ShapeDtypeStruct((1, LANES), F32)],
        scratch_shapes=[pltpu.VMEM((1, LANES), F32)],
        compiler_params=_cparams(("arbitrary",), 32),
        name="route",
    )(top_idx, counts_in)


def _dispatch_body(pos_ref, x_ref, xd_in, xd_hbm, sem):
    del xd_in
    i = pl.program_id(0)
    ts = x_ref.shape[0] // PACK_ROWS

    def row_copy(r, k):
        p = pos_ref[(i * ts + r) * TOP_K + k]
        src = x_ref.at[pl.ds(pl.multiple_of(r * PACK_ROWS, PACK_ROWS), PACK_ROWS), :]
        dst = xd_hbm.at[pl.ds(pl.multiple_of(p * PACK_ROWS, PACK_ROWS), PACK_ROWS), :]
        return pltpu.make_async_copy(src, dst, sem)

    def issue(r, c):
        for k in range(TOP_K):
            row_copy(r, k).start()
        return c
    lax.fori_loop(0, ts, issue, 0)

    def drain(r, c):
        for k in range(TOP_K):
            row_copy(r, k).wait()
        return c
    lax.fori_loop(0, ts, drain, 0)


def _dispatch(pos_flat, x_packed, x_disp):
    m = x_packed.shape[0] // PACK_ROWS
    ts = _row_tile(m, TM_DISPATCH)
    return pl.pallas_call(
        _dispatch_body,
        grid_spec=pltpu.PrefetchScalarGridSpec(
            num_scalar_prefetch=1,
            grid=(m // ts,),
            in_specs=[pl.BlockSpec((ts * PACK_ROWS, LANES), lambda i, pos: (i, 0)),
                      pl.BlockSpec(memory_space=pl.ANY)],
            out_specs=pl.BlockSpec(memory_space=pl.ANY),
            scratch_shapes=[pltpu.SemaphoreType.DMA(())],
        ),
        out_shape=jax.ShapeDtypeStruct(x_disp.shape, x_disp.dtype),
        input_output_aliases={2: 0},
        compiler_params=_cparams(("arbitrary",), 32),
        name="dispatch",
    )(pos_flat, x_packed, x_disp)


def _tile_changed(te_ref, i):
    return (i == 0) | (te_ref[i] != te_ref[jnp.maximum(i - 1, 0)])


def _gmm1_body(te_ref, nu_ref, x_ref, wg_ref, wu_ref, bg_ref, bu_ref, h_ref, wgb, wub):
    i = pl.program_id(1)

    @pl.when(_tile_changed(te_ref, i))
    def _():
        wgb[...] = wg_ref[0].astype(BF16)
        wub[...] = wu_ref[0].astype(BF16)

    @pl.when(i < nu_ref[0])
    def _():
        tm = x_ref.shape[0] // PACK_ROWS
        x = _load_packed(lambda s: x_ref[pl.ds(s, tm, stride=PACK_ROWS), :], BF16)
        gt = _dot(x, wgb[...]) + bg_ref[0]
        up = _dot(x, wub[...]) + bu_ref[0]
        gt = jnp.minimum(gt, SWIGLU_LIMIT)
        up = jnp.clip(up, -SWIGLU_LIMIT, SWIGLU_LIMIT)
        act = (up + 1.0) * gt * jax.nn.sigmoid(SWIGLU_ALPHA * gt)
        h_ref[...] = act.astype(h_ref.dtype)

    @pl.when(i >= nu_ref[0])
    def _():
        h_ref[...] = jnp.zeros(h_ref.shape, h_ref.dtype)


def _gmm2_body(te_ref, nu_ref, h_ref, wd_ref, bd_ref, y_ref, wdb):
    i = pl.program_id(0)

    @pl.when(_tile_changed(te_ref, i))
    def _():
        wdb[...] = wd_ref[0].astype(BF16)

    @pl.when(i < nu_ref[0])
    def _():
        tm, half = h_ref.shape[0], wdb.shape[1] // 2
        h = h_ref[...]
        lo = _dot(h, wdb[:, :half]) + bd_ref[0][:, :half]
        hi = _dot(h, wdb[:, half:]) + bd_ref[0][:, half:]
        _store_packed(y_ref, tm, _pack_words(lo.astype(BF16), hi.astype(BF16)))

    @pl.when(i >= nu_ref[0])
    def _():
        y_ref[...] = jnp.zeros(y_ref.shape, y_ref.dtype)


def _gmm_specs(d_in, tn):
    tm = TM_GMM

    def tile(i, nu):
        return jnp.minimum(i, nu[0] - 1)

    x_spec = pl.BlockSpec((tm, d_in), lambda j, i, te, nu: (tile(i, nu), 0))
    w_spec = pl.BlockSpec((1, d_in, tn), lambda j, i, te, nu: (te[tile(i, nu)], 0, j))
    b_spec = pl.BlockSpec((1, 1, tn), lambda j, i, te, nu: (te[tile(i, nu)], 0, j))
    o_spec = pl.BlockSpec((tm, tn), lambda j, i, te, nu: (i, j))
    return x_spec, w_spec, b_spec, o_spec


def _gmm1(tile_exp, n_used, x_disp, w_gate, w_up, b_gate, b_up):
    cap = x_disp.shape[0] // PACK_ROWS
    d, dff = w_gate.shape[1], w_gate.shape[2]
    tn = TN_GMM
    _, w_spec, b_spec, o_spec = _gmm_specs(d, tn)
    x_spec = pl.BlockSpec((TM_GMM * PACK_ROWS, LANES), lambda j, i, te, nu: (jnp.minimum(i, nu[0] - 1), 0))
    return pl.pallas_call(
        _gmm1_body,
        grid_spec=pltpu.PrefetchScalarGridSpec(
            num_scalar_prefetch=2,
            grid=(dff // tn, cap // TM_GMM),
            in_specs=[x_spec, w_spec, w_spec, b_spec, b_spec],
            out_specs=o_spec,
            scratch_shapes=[pltpu.VMEM((d, tn), BF16), pltpu.VMEM((d, tn), BF16)],
        ),
        out_shape=jax.ShapeDtypeStruct((cap, dff), BF16),
        compiler_params=_cparams(("arbitrary", "arbitrary"), 60),
        name="gmm1",
    )(tile_exp, n_used, x_disp, w_gate, w_up, b_gate, b_up)


def _gmm2(tile_exp, n_used, h_disp, w_down, b_down):
    cap, dff = h_disp.shape
    d = w_down.shape[2]
    assert d == 2 * PACK_ROWS * LANES
    tm = TM_GMM

    def tile(i, nu):
        return jnp.minimum(i, nu[0] - 1)

    return pl.pallas_call(
        _gmm2_body,
        grid_spec=pltpu.PrefetchScalarGridSpec(
            num_scalar_prefetch=2,
            grid=(cap // tm,),
            in_specs=[
                pl.BlockSpec((tm, dff), lambda i, te, nu: (tile(i, nu), 0)),
                pl.BlockSpec((1, dff, d), lambda i, te, nu: (te[tile(i, nu)], 0, 0)),
                pl.BlockSpec((1, 1, d), lambda i, te, nu: (te[tile(i, nu)], 0, 0)),
            ],
            out_specs=pl.BlockSpec((tm * PACK_ROWS, LANES), lambda i, te, nu: (i, 0)),
            scratch_shapes=[pltpu.VMEM((dff, d), BF16)],
        ),
        out_shape=jax.ShapeDtypeStruct((cap * PACK_ROWS, LANES), U32),
        compiler_params=_cparams(("arbitrary",), 60),
        name="gmm2",
    )(tile_exp, n_used, h_disp, w_down, b_down)


def _combine_body(pos_ref, y_hbm, xmid_ref, gate_ref, fw_ref, o_ref, buf, sem):
    i = pl.program_id(0)
    n = pl.num_programs(0)
    tc = TOK_COMBINE

    def row_copy(step, slot, r, k):
        p = pos_ref[(step * tc + r) * TOP_K + k]
        src = y_hbm.at[pl.ds(pl.multiple_of(p * PACK_ROWS, PACK_ROWS), PACK_ROWS), :]
        dst = buf.at[slot, k, pl.ds(pl.multiple_of(r * PACK_ROWS, PACK_ROWS), PACK_ROWS), :]
        return pltpu.make_async_copy(src, dst, sem.at[slot])

    def issue(step, slot):
        def body(r, c):
            for k in range(TOP_K):
                row_copy(step, slot, r, k).start()
            return c
        lax.fori_loop(0, tc, body, 0)

    @pl.when(i == 0)
    def _():
        issue(0, 0)

    @pl.when(i + 1 < n)
    def _():
        issue(i + 1, (i + 1) % 2)

    slot = i % 2

    def wait_body(r, c):
        for k in range(TOP_K):
            row_copy(i, slot, r, k).wait()
        return c
    lax.fori_loop(0, tc, wait_body, 0)

    gates = gate_ref[...]
    acc = xmid_ref[...]
    for k in range(TOP_K):
        y_k = _load_packed(lambda s, k=k: buf[slot, k, pl.ds(s, tc, stride=PACK_ROWS), :], F32)
        acc = acc + gates[:, k:k + 1] * y_k
    o_ref[...] = acc * lax.rsqrt(jnp.mean(acc * acc, axis=-1, keepdims=True) + EPS) * fw_ref[...]


def _combine(pos_flat, y_disp, x_mid, gates, final_w):
    m, d = x_mid.shape
    tc = TOK_COMBINE
    return pl.pallas_call(
        _combine_body,
        grid_spec=pltpu.PrefetchScalarGridSpec(
            num_scalar_prefetch=1,
            grid=(m // tc,),
            in_specs=[
                pl.BlockSpec(memory_space=pl.ANY),
                pl.BlockSpec((tc, d), lambda i, pos: (i, 0)),
                pl.BlockSpec((tc, LANES), lambda i, pos: (i, 0)),
                pl.BlockSpec((1, d), lambda i, pos: (0, 0)),
            ],
            out_specs=pl.BlockSpec((tc, d), lambda i, pos: (i, 0)),
            scratch_shapes=[pltpu.VMEM((2, TOP_K, tc * PACK_ROWS, LANES), U32), pltpu.SemaphoreType.DMA((2,))],
        ),
        out_shape=jax.ShapeDtypeStruct((m, d), F32),
        compiler_params=_cparams(("arbitrary",), 32),
        name="combine",
    )(pos_flat, y_disp, x_mid, gates, final_w)


def _tile_layout(counts, n_assign, n_exp):
    tiles = (counts + TM_GMM - 1) // TM_GMM
    tile_end = jnp.cumsum(tiles)
    row_start = (tile_end - tiles) * TM_GMM
    n_tiles = -(-n_assign // TM_GMM) + n_exp
    n_used = tile_end[-1]
    tile_ids = jnp.minimum(jnp.arange(n_tiles, dtype=I32), n_used - 1)
    tile_exp = jnp.minimum(jnp.sum((tile_end[None, :] <= tile_ids[:, None]).astype(I32), axis=1), n_exp - 1)
    return row_start.astype(I32), tile_exp.astype(I32), n_used.reshape(1).astype(I32), n_tiles * TM_GMM


def _positions(top_idx, rank, row_start, n_exp):
    idx = top_idx[:, :TOP_K]
    onehot = idx[:, :, None] == jnp.arange(n_exp, dtype=I32)[None, None, :]
    base = jnp.sum(jnp.where(onehot, row_start[None, None, :], 0), axis=-1)
    return (base + rank[:, :TOP_K]).reshape(-1).astype(I32)


def kernel(x_prompt, x_sample, cache_attn_k, cache_attn_v, state_conv, state_ssm, norm_mix_w, w_in, conv_w, conv_b, dt_bias, a_log, d_skip, ssd_norm_w, w_out, norm_ffn_w, w_router, b_router, w_gate, b_gate, w_up, b_up, w_down, b_down, norm_final_w):
    depth = norm_mix_w.shape[0]
    assert depth == 1, "single-layer step"
    bp, seq, d = x_prompt.shape
    bs, dq, _ = x_sample.shape
    n_att_heads = cache_attn_k.shape[3]
    d_att = n_att_heads * HD_ATT
    n_pairs = d_att // LANES
    n_heads = a_log.shape[1]
    d_ssm = n_heads * SSM_HD
    conv_dim = conv_w.shape[2]
    n_exp = w_router.shape[2]
    n_prompt = bp * seq
    n_sample = bs * dq
    assert n_sample == CHUNK and seq % SUPER == 0 and n_heads <= LANES

    wl = w_in[0]
    o_q, o_k, o_v, o_z, o_x, o_dt = 0, d_att, 2 * d_att, 3 * d_att, 3 * d_att + d_ssm, 3 * d_att + d_ssm + conv_dim
    w_main = jnp.concatenate([wl[:, o_x:o_dt], wl[:, o_q:o_z], wl[:, o_z:o_x]], axis=1).astype(BF16)
    w_dt = jnp.pad(wl[:, o_dt:], ((0, 0), (0, LANES - n_heads))).astype(BF16)
    col_q = conv_dim // LANES
    col_k = col_q + n_pairs
    col_v = col_k + n_pairs
    w_att = w_out[0, :d_att].astype(BF16)
    w_ssd = w_out[0, d_att:].astype(BF16)
    w_rp = jnp.pad(w_router[0], ((0, 0), (0, LANES - n_exp)))
    w_rh = w_rp.astype(BF16)
    w_r = jnp.concatenate([w_rh, (w_rp - w_rh.astype(F32)).astype(BF16)], axis=1)
    b_r = jnp.pad(b_router[0], (0, LANES - n_exp), constant_values=NEG).reshape(1, LANES)
    pad_h = (0, LANES - n_heads)
    head_of_col = jnp.arange(d_ssm, dtype=I32) // SSM_HD
    e_mat = (jnp.arange(LANES, dtype=I32)[:, None] == head_of_col[None, :]).astype(BF16)
    r_mat = (head_of_col[:, None] == jnp.arange(LANES, dtype=I32)[None, :]).astype(BF16)
    ssd_consts = (
        conv_w[0], conv_b[0].reshape(1, conv_dim),
        jnp.pad(dt_bias[0], pad_h).reshape(1, LANES), jnp.pad(a_log[0], pad_h).reshape(1, LANES),
        jnp.repeat(d_skip[0], SSM_HD).reshape(1, d_ssm), ssd_norm_w[0].reshape(1, d_ssm), e_mat, r_mat,
    )

    xp = x_prompt.reshape(n_prompt, d)
    xs = x_sample.reshape(n_sample, d)
    nmw = norm_mix_w[0].reshape(1, d)
    proj_p, dt_p = _inproj(xp, nmw, w_main, w_dt)
    proj_s, dt_s = _inproj(xs, nmw, w_main, w_dt)
    n_main = proj_p.shape[1]
    z_col = (conv_dim + 3 * d_att) // d_ssm
    assert z_col * d_ssm == conv_dim + 3 * d_att

    win_p = min(BRANCHES[-1][0], seq)
    y_att_p, k_p, v_p = _attn_prompt(proj_p, bp, seq, win_p, col_q, col_k, col_v, n_pairs)
    ck = cache_attn_k[0].reshape(bs, -1, d_att)
    cv = cache_attn_v[0].reshape(bs, -1, d_att)
    y_att_s = _attn_sample(proj_s, ck, cv, 0, col_q, col_k, col_v, n_pairs, bs, dq)

    n_chunks = seq // CHUNK
    hp = n_heads * SSM_HD
    y_ssd_p, cnew_p, h_p = _ssd(
        proj_p, proj_p, dt_p, jnp.zeros((bp, 8, conv_dim), F32), jnp.zeros((bp, hp, D_STATE), F32), ssd_consts,
        batch=bp, n_chunks=n_chunks, nvalid=CHUNK,
        xbc_spec=pl.BlockSpec((CHUNK, conv_dim), lambda b, c: (b * n_chunks + c, 0)),
        z_spec=pl.BlockSpec((CHUNK, d_ssm), lambda b, c: (b * n_chunks + c, z_col)),
        dt_spec=pl.BlockSpec((CHUNK, LANES), lambda b, c: (b * n_chunks + c, 0)),
        d_ssm=d_ssm, n_heads=n_heads)

    proj_s3 = proj_s.reshape(bs, dq, n_main)
    cprev_s = jnp.pad(state_conv[0], ((0, 0), (8 - (CONV_W - 1), 0), (0, 0)))
    y_ssd_s, cnew_s, h_s = _ssd(
        proj_s3, proj_s3, dt_s.reshape(bs, dq, LANES), cprev_s, state_ssm[0].reshape(bs, hp, D_STATE), ssd_consts,
        batch=bs, n_chunks=1, nvalid=dq,
        xbc_spec=pl.BlockSpec((1, dq, conv_dim), lambda b, c: (b, 0, 0)),
        z_spec=pl.BlockSpec((1, dq, d_ssm), lambda b, c: (b, 0, z_col)),
        dt_spec=pl.BlockSpec((1, dq, LANES), lambda b, c: (b, 0, 0)),
        d_ssm=d_ssm, n_heads=n_heads)

    nfw = norm_ffn_w[0].reshape(1, d)
    xmid_p, xpk_p, idx_p, gates_p = _outproj(xp, y_att_p, y_ssd_p, w_att, w_ssd, nfw, w_r, b_r)
    xmid_s, xpk_s, idx_s, gates_s = _outproj(xs, y_att_s.reshape(n_sample, d_att),
                                             y_ssd_s.reshape(n_sample, d_ssm), w_att, w_ssd, nfw, w_r, b_r)

    rank_p, cnt_p = _route(idx_p, jnp.zeros((1, LANES), F32))
    rank_s, cnt_all = _route(idx_s, cnt_p)
    counts = cnt_all[0, :n_exp].astype(I32)
    n_assign = (n_prompt + n_sample) * TOP_K
    row_start, tile_exp, n_used, cap = _tile_layout(counts, n_assign, n_exp)
    pos_p = _positions(idx_p, rank_p, row_start, n_exp)
    pos_s = _positions(idx_s, rank_s, row_start, n_exp)
    x_disp = jnp.zeros((cap * PACK_ROWS, LANES), U32)
    x_disp = _dispatch(pos_p, xpk_p, x_disp)
    x_disp = _dispatch(pos_s, xpk_s, x_disp)
    h_disp = _gmm1(tile_exp, n_used, x_disp, w_gate[0], w_up[0],
                   b_gate[0].reshape(n_exp, 1, -1), b_up[0].reshape(n_exp, 1, -1))
    y_disp = _gmm2(tile_exp, n_used, h_disp, w_down[0], b_down[0].reshape(n_exp, 1, -1))
    fw = norm_final_w.reshape(1, d)
    y_p = _combine(pos_p, y_disp, xmid_p, gates_p, fw)
    y_s = _combine(pos_s, y_disp, xmid_s, gates_s, fw)

    k_p = k_p.reshape(1, bp, win_p, n_att_heads, HD_ATT)
    v_p = v_p.reshape(1, bp, win_p, n_att_heads, HD_ATT)
    k_s = proj_s3[:, :, col_k * LANES:col_v * LANES].reshape(1, bs, dq, n_att_heads, HD_ATT)
    v_s = proj_s3[:, :, col_v * LANES:col_v * LANES + d_att].reshape(1, bs, dq, n_att_heads, HD_ATT)
    return (
        y_p.reshape(bp, seq, d), y_s.reshape(bs, dq, d), k_p, v_p,
        cnew_p[:, 8 - (CONV_W - 1):][None], h_p.reshape(1, bp, n_heads, SSM_HD, D_STATE),
        k_s, v_s,
        cnew_s[:, 8 - (CONV_W - 1):][None], h_s.reshape(1, bs, n_heads, SSM_HD, D_STATE),
    )
```

```python
import functools

import jax
import jax.numpy as jnp
from jax import lax
from jax.experimental import pallas as pl
from jax.experimental.pallas import tpu as pltpu

F32 = jnp.float32
BF16 = jnp.bfloat16
I32 = jnp.int32
U32 = jnp.uint32

EPS = 1e-5
NEG = -1e30
LANES = 128
HD_ATT = 64
BRANCHES = ((128, 1), (512, 4), (2048, 16))
RES = 16
ATT_BLK = 128
ATT_BLOCKS_PER_STEP = 2
SUPER = RES * ATT_BLK
SSM_HD = 64
SSM_GROUPS = 4
D_STATE = 128
CONV_W = 4
CHUNK = 128
TOP_K = 4
SWIGLU_LIMIT = 7.0
SWIGLU_ALPHA = 1.702

PACK_ROWS = 8

TM_PROJ = 512
OUTPROJ_PARTS = 2
TN_PROJ = 2048
TM_ROUTE = 512
TM_DISPATCH = 512
TM_GMM = 512
TN_GMM = 1024
TOK_COMBINE = 128


def _row_tile(m, cap):
    t = cap - cap % LANES
    while m % t:
        t -= LANES
    return t


def _cparams(sem, vmem_mb):
    return pltpu.CompilerParams(dimension_semantics=sem, vmem_limit_bytes=vmem_mb << 20)


def _dot(a, b):
    return jnp.dot(a, b, preferred_element_type=F32)


def _dot_nt(a, b):
    return lax.dot_general(a, b, (((1,), (1,)), ((), ())), preferred_element_type=F32)


def _split_dot(v, onehot_bf16):
    hi = v.astype(BF16)
    lo = (v - hi.astype(F32)).astype(BF16)
    return _dot(hi, onehot_bf16) + _dot(lo, onehot_bf16)


def _pack_words(lo, hi):
    lo_bits = lax.bitcast_convert_type(lo.astype(F32), U32)
    hi_bits = lax.bitcast_convert_type(hi.astype(F32), U32)
    return (hi_bits & jnp.uint32(0xFFFF0000)) | (lo_bits >> 16)


def _store_packed(ref, rows, words, row0=0):
    for s in range(PACK_ROWS):
        ref[pl.ds(row0 * PACK_ROWS + s, rows, stride=PACK_ROWS), :] = words[:, s * LANES:(s + 1) * LANES]


def _load_packed(load_slab, dtype):
    lo, hi = [], []
    for s in range(PACK_ROWS):
        w = load_slab(s)
        lo.append(lax.bitcast_convert_type(w << 16, F32).astype(dtype))
        hi.append(lax.bitcast_convert_type(w & jnp.uint32(0xFFFF0000), F32).astype(dtype))
    return jnp.concatenate(lo + hi, axis=1)


def _inproj_body(x_ref, nw_ref, w_ref, wdt_ref, o_ref, dt_ref, h_sc):
    @pl.when(pl.program_id(1) == 0)
    def _():
        x = x_ref[...]
        ms = jnp.mean(x * x, axis=-1, keepdims=True)
        h = (x * lax.rsqrt(ms + EPS) * nw_ref[...]).astype(BF16)
        h_sc[...] = h
        dt_ref[...] = _dot(h, wdt_ref[...])

    o_ref[...] = _dot(h_sc[...], w_ref[...])


def _inproj(x_all, norm_w, w_main, w_dt):
    m, d = x_all.shape
    n = w_main.shape[1]
    tm, tn = _row_tile(m, TM_PROJ), TN_PROJ
    return pl.pallas_call(
        _inproj_body,
        grid=(m // tm, n // tn),
        in_specs=[
            pl.BlockSpec((tm, d), lambda i, j: (i, 0)),
            pl.BlockSpec((1, d), lambda i, j: (0, 0)),
            pl.BlockSpec((d, tn), lambda i, j: (0, j)),
            pl.BlockSpec((d, LANES), lambda i, j: (0, 0)),
        ],
        out_specs=[
            pl.BlockSpec((tm, tn), lambda i, j: (i, j)),
            pl.BlockSpec((tm, LANES), lambda i, j: (i, 0)),
        ],
        out_shape=[jax.ShapeDtypeStruct((m, n), F32), jax.ShapeDtypeStruct((m, LANES), F32)],
        scratch_shapes=[pltpu.VMEM((tm, d), BF16)],
        compiler_params=_cparams(("parallel", "arbitrary"), 48),
        name="inproj",
    )(x_all, norm_w, w_main, w_dt)


def _attn_blocks(args):
    lane = lax.broadcasted_iota(I32, (ATT_BLK, LANES), 1)
    h0 = lane < HD_ATT
    hk = lax.broadcasted_iota(I32, (2 * ATT_BLK, LANES), 1) < HD_ATT
    n = len(args)
    qs = [jnp.concatenate([jnp.where(h0, a[0], 0.0), jnp.where(h0, 0.0, a[0])], axis=0).astype(BF16) for a in args]
    kk = [jnp.concatenate([a[1], a[2]], axis=0).astype(BF16) for a in args]
    s = [_dot_nt(qs[i], kk[i]) for i in range(n)]
    s = [s[i] + jnp.concatenate([args[i][8], args[i][8]], axis=0) for i in range(n)]
    mn = [jnp.maximum(args[i][5], jnp.max(s[i], axis=1, keepdims=True)) for i in range(n)]
    p = [jnp.exp(s[i] - mn[i]).astype(BF16) for i in range(n)]
    vv = [jnp.concatenate([a[3], a[4]], axis=0) for a in args]
    v0 = [jnp.where(hk, vv[i], 1.0).astype(BF16) for i in range(n)]
    v1 = [jnp.where(hk, 1.0, vv[i]).astype(BF16) for i in range(n)]
    o0 = [_dot(p[i][:ATT_BLK], v0[i]) for i in range(n)]
    o1 = [_dot(p[i][ATT_BLK:], v1[i]) for i in range(n)]
    al = [jnp.exp(args[i][5] - mn[i]) for i in range(n)]
    a0 = [jnp.broadcast_to(al[i][:ATT_BLK], (ATT_BLK, LANES)) for i in range(n)]
    a1 = [jnp.broadcast_to(al[i][ATT_BLK:], (ATT_BLK, LANES)) for i in range(n)]
    ln = [jnp.where(h0, a1[i], a0[i]) * args[i][6] + jnp.where(h0, o1[i], o0[i]) for i in range(n)]
    an = [jnp.where(h0, a0[i], a1[i]) * args[i][7] + jnp.where(h0, o0[i], o1[i]) for i in range(n)]
    return [(mn[i], ln[i], an[i]) for i in range(n)]


def _attn_prompt_body(q_ref, k_ref, v_ref, o_ref, kw_ref, vw_ref, qr, kr, vr, m_sc, l_sc, acc_sc, bias_sc, *, seq):
    ni = seq // RES
    scale = HD_ATT ** -0.5
    win = kw_ref.shape[0]
    kw_ref[...] = k_ref[seq - win:, :]
    vw_ref[...] = v_ref[seq - win:, :]
    for j in range(RES):
        qr[j] = q_ref[pl.ds(j, ni, stride=RES), :] * scale
        kr[j] = k_ref[pl.ds(j, ni, stride=RES), :]
        vr[j] = v_ref[pl.ds(j, ni, stride=RES), :]

    rq = lax.broadcasted_iota(I32, (ATT_BLK, 2 * ATT_BLK), 0)
    cc = lax.broadcasted_iota(I32, (ATT_BLK, 2 * ATT_BLK), 1)
    is_cur = cc >= ATT_BLK
    rk = jnp.where(is_cur, cc - ATT_BLK, cc)
    for br, (_, dil) in enumerate(BRANCHES):
        nseg = RES // dil
        seg = ATT_BLK // nseg
        oq = nseg * (rq % seg) + rq // seg
        ok = nseg * (rk % seg) + rk // seg
        allowed = (is_cur & (ok <= oq)) | (jnp.logical_not(is_cur) & (ok >= oq))
        bias_sc[2 * br + 1] = jnp.where(allowed, 0.0, NEG)
        bias_sc[2 * br] = jnp.where(allowed & is_cur, 0.0, NEG)

    def run_block(br, rows, qoff, poff, has_prev):
        seg = ATT_BLK // len(rows)
        q = jnp.concatenate([qr[r, pl.ds(qoff, seg), :] for r in rows], axis=0)
        kc = jnp.concatenate([kr[r, pl.ds(qoff, seg), :] for r in rows], axis=0)
        vc = jnp.concatenate([vr[r, pl.ds(qoff, seg), :] for r in rows], axis=0)
        kp = jnp.concatenate([kr[r, pl.ds(poff, seg), :] for r in rows], axis=0)
        vp = jnp.concatenate([vr[r, pl.ds(poff, seg), :] for r in rows], axis=0)
        return q, kp, kc, vp, vc

    def super_block(u, carry):
        base = pl.multiple_of(u * ATT_BLK, ATT_BLK)
        m_sc[...] = jnp.full(m_sc.shape, NEG, F32)
        l_sc[...] = jnp.zeros(l_sc.shape, F32)
        acc_sc[...] = jnp.zeros(acc_sc.shape, F32)

        def branch_steps(br, specs):
            loaded = []
            for rows, loc, seg in specs:
                qoff = pl.multiple_of(base + loc, 8)
                poff = pl.multiple_of(jnp.maximum(base + loc - seg, 0), 8)
                has_prev = (base + loc >= seg).astype(I32)
                q, kp, kc, vp, vc = run_block(br, rows, qoff, poff, has_prev)
                bias = bias_sc[2 * br + has_prev]
                sl = pl.ds(pl.multiple_of(loc, 8), seg)
                mo = jnp.concatenate([m_sc[h, r, sl, :] for h in range(2) for r in rows], axis=0)
                lo = jnp.concatenate([l_sc[r, sl, :] for r in rows], axis=0)
                ao = jnp.concatenate([acc_sc[r, sl, :] for r in rows], axis=0)
                loaded.append((q, kp, kc, vp, vc, mo, lo, ao, bias))
            results = _attn_blocks(loaded)
            for (rows, loc, seg), (mn, ln, an) in zip(specs, results):
                sl = pl.ds(pl.multiple_of(loc, 8), seg)
                for h in range(2):
                    for si, r in enumerate(rows):
                        lo_r = h * ATT_BLK + si * seg
                        m_sc[h, r, sl, :] = mn[lo_r:lo_r + seg]
                for si, r in enumerate(rows):
                    acc_sc[r, sl, :] = an[si * seg:(si + 1) * seg]
                    l_sc[r, sl, :] = ln[si * seg:(si + 1) * seg]

        nb = ATT_BLOCKS_PER_STEP

        def b1(n, c):
            branch_steps(0, [(list(range(RES)), (n * nb + uu) * 8, 8) for uu in range(nb)])
            return c
        lax.fori_loop(0, RES // nb, b1, 0)

        def b4(t, c):
            specs = []
            for uu in range(nb):
                tt = t * nb + uu
                specs.append(([tt // 4 + 4 * m for m in range(4)], (tt % 4) * 32, 32))
            branch_steps(1, specs)
            return c
        lax.fori_loop(0, RES // nb, b4, 0)

        def b16(j, c):
            branch_steps(2, [([j * nb + uu], 0, ATT_BLK) for uu in range(nb)])
            return c
        lax.fori_loop(0, RES // nb, b16, 0)

        for j in range(RES):
            den = pltpu.roll(l_sc[j], HD_ATT, axis=1)
            o_ref[pl.ds(u * SUPER + j, ATT_BLK, stride=RES), :] = acc_sc[j] / den
        return carry

    lax.fori_loop(0, seq // SUPER, super_block, 0)


def _attn_prompt(proj, batch, seq, win, col_q, col_k, col_v, n_pairs):
    ni = seq // RES
    body = functools.partial(_attn_prompt_body, seq=seq)

    def spec(col):
        return pl.BlockSpec((seq, LANES), lambda b, p: (b, col + p))

    wspec = pl.BlockSpec((win, LANES), lambda b, p: (b, p))
    wshape = jax.ShapeDtypeStruct((batch * win, n_pairs * LANES), F32)
    return pl.pallas_call(
        body,
        grid=(batch, n_pairs),
        in_specs=[spec(col_q), spec(col_k), spec(col_v)],
        out_specs=[pl.BlockSpec((seq, LANES), lambda b, p: (b, p)), wspec, wspec],
        out_shape=[jax.ShapeDtypeStruct((batch * seq, n_pairs * LANES), F32), wshape, wshape],
        scratch_shapes=[
            pltpu.VMEM((RES, ni, LANES), F32),
            pltpu.VMEM((RES, ni, LANES), F32),
            pltpu.VMEM((RES, ni, LANES), F32),
            pltpu.VMEM((2, RES, ATT_BLK, 1), F32),
            pltpu.VMEM((RES, ATT_BLK, LANES), F32),
            pltpu.VMEM((RES, ATT_BLK, LANES), F32),
            pltpu.VMEM((2 * len(BRANCHES), ATT_BLK, 2 * ATT_BLK), F32),
        ],
        compiler_params=_cparams(("parallel", "parallel"), 60),
        name="attn_prompt",
    )(proj, proj, proj)


def _sample_key_rows(nbuf, dq):
    tail = max(w for w, dil in BRANCHES if dil < RES)
    assert nbuf % RES == 0 and tail % RES == 0 and tail <= nbuf and dq <= RES
    return (nbuf - tail) // RES, nbuf - tail


def _attn_sample_body(q_ref, kn_ref, vn_ref, kc_ref, vc_ref, o_ref, *, dq, nbuf, n_pairs):
    b = pl.program_id(0)
    assert 8 % dq == 0
    groups, tail_start = _sample_key_rows(nbuf, dq)
    n_str = groups * dq
    nc = kc_ref.shape[1]
    lane = lax.broadcasted_iota(I32, (dq, LANES), 1)
    h0 = lane < HD_ATT
    zpad = jnp.zeros((8 - dq, LANES), F32)

    def counts(delta):
        cnt = jnp.zeros(delta.shape, F32)
        for win, dil in BRANCHES:
            hit = (delta >= 0) & (delta <= win) & (delta % dil == 0)
            cnt = cnt + hit.astype(F32)
        return cnt

    col_c = lax.broadcasted_iota(I32, (2 * dq, nc), 1)
    pos_c = jnp.where(col_c < n_str, RES * (col_c // dq) + col_c % dq, tail_start + col_c - n_str)
    qi_c = lax.broadcasted_iota(I32, (2 * dq, nc), 0) % dq
    cnt_c = counts(nbuf + qi_c - pos_c)
    qi_n = lax.broadcasted_iota(I32, (2 * dq, 8), 0) % dq
    col_n = lax.broadcasted_iota(I32, (2 * dq, 8), 1)
    cnt_n = jnp.where(col_n < dq, counts(qi_n - col_n), 0.0)

    row8 = pl.multiple_of((b * dq) // 8 * 8, 8)
    sub = (b * dq) % 8 // dq

    def new_rows(ref, cols):
        x8 = ref[pl.ds(row8, 8), cols]
        out = x8[0:dq]
        for j in range(1, 8 // dq):
            out = jnp.where(sub == j, x8[j * dq:(j + 1) * dq], out)
        return out

    for p in range(n_pairs):
        cols = slice(p * LANES, (p + 1) * LANES)
        q = new_rows(q_ref, cols) * (HD_ATT ** -0.5)
        qs = jnp.concatenate([jnp.where(h0, q, 0.0), jnp.where(h0, 0.0, q)], axis=0).astype(BF16)
        kn = jnp.concatenate([new_rows(kn_ref, cols), zpad], axis=0).astype(BF16)
        vn = jnp.concatenate([new_rows(vn_ref, cols), zpad], axis=0).astype(BF16)
        kc = kc_ref[0, :, cols].astype(BF16)
        vc = vc_ref[0, :, cols].astype(BF16)
        s_c = jnp.where(cnt_c > 0, _dot_nt(qs, kc), NEG)
        s_n = jnp.where(cnt_n > 0, _dot_nt(qs, kn), NEG)
        m = jnp.maximum(jnp.max(s_c, axis=1, keepdims=True), jnp.max(s_n, axis=1, keepdims=True))
        e_c = cnt_c * jnp.exp(s_c - m)
        e_n = cnt_n * jnp.exp(s_n - m)
        den = jnp.sum(e_c, axis=1, keepdims=True) + jnp.sum(e_n, axis=1, keepdims=True)
        num = _dot(e_c.astype(BF16), vc) + _dot(e_n.astype(BF16), vn)
        out = num / den
        o_ref[0, :, cols] = jnp.where(h0, out[:dq], out[dq:])


def _compact_cache(cache, dq):
    bs, nbuf, nh, hd = cache.shape
    groups, tail_start = _sample_key_rows(nbuf, dq)
    strided = cache[:, :tail_start].reshape(bs, groups, RES, nh, hd)[:, :, :dq].reshape(bs, groups * dq, nh, hd)
    return jnp.concatenate([strided, cache[:, tail_start:]], axis=1).reshape(bs, -1, nh * hd)


def _attn_sample(proj, cache_k, cache_v, col_q, col_k, col_v, n_pairs, dec_batch, dq, nbuf):
    nc = cache_k.shape[1]
    width = n_pairs * LANES
    assert col_q % n_pairs == 0 and col_k % n_pairs == 0 and col_v % n_pairs == 0
    body = functools.partial(_attn_sample_body, dq=dq, nbuf=nbuf, n_pairs=n_pairs)
    nrows = dec_batch * dq

    def pspec(col):
        return pl.BlockSpec((nrows, width), lambda b: (0, col // n_pairs))

    cspec = pl.BlockSpec((1, nc, width), lambda b: (b, 0, 0))
    return pl.pallas_call(
        body,
        grid=(dec_batch,),
        in_specs=[pspec(col_q), pspec(col_k), pspec(col_v), cspec, cspec],
        out_specs=pl.BlockSpec((1, dq, width), lambda b: (b, 0, 0)),
        out_shape=jax.ShapeDtypeStruct((dec_batch, dq, width), F32),
        compiler_params=_cparams(("parallel",), 48),
        name="attn_sample",
    )(proj, proj, proj, cache_k, cache_v)


def _ssd_body(xbc_ref, z_ref, dt_ref, cprev_ref, h0_ref, cw_ref, cb_ref, dtb_ref, alog_ref, dskx_ref,
              nw_ref, e_ref, r_ref, y_ref, cnew_ref, hout_ref, h_sc, xf_sc, *, nvalid, d_ssm, n_heads):
    L = CHUNK
    c = pl.program_id(1)
    gw = d_ssm // SSM_GROUPS
    hpg = n_heads // SSM_GROUPS

    @pl.when(c == 0)
    def _():
        h_sc[...] = h0_ref[0]
        xf_sc[0:8] = cprev_ref[0]

    if nvalid < L:
        xf_sc[8:8 + L] = jnp.zeros((L, xf_sc.shape[1]), F32)
        xf_sc[8:8 + nvalid] = xbc_ref[0]
        zpad = jnp.zeros((L - nvalid, LANES), F32)
        dt_raw = jnp.concatenate([dt_ref[0], zpad], axis=0)
    else:
        xf_sc[8:8 + L] = xbc_ref[...]
        dt_raw = dt_ref[...]
    conv = cb_ref[...] + cw_ref[0:1, :] * xf_sc[5:5 + L]
    for w in range(1, CONV_W):
        conv = conv + cw_ref[w:w + 1, :] * xf_sc[5 + w:5 + w + L]
    cnew_ref[0] = xf_sc[nvalid:nvalid + 8]
    xf_sc[0:8] = xf_sc[L:L + 8]

    row = lax.broadcasted_iota(I32, (L, LANES), 0)
    lane = lax.broadcasted_iota(I32, (L, LANES), 1)
    if nvalid < L:
        rowc = lax.broadcasted_iota(I32, conv.shape, 0)
        conv = jnp.where(rowc < nvalid, conv, 0.0)
    act = conv * jax.nn.sigmoid(conv)
    xs = act[:, :d_ssm]
    bm = act[:, d_ssm:d_ssm + SSM_GROUPS * D_STATE].astype(BF16)
    cm = act[:, d_ssm + SSM_GROUPS * D_STATE:].astype(BF16)

    dt = jax.nn.softplus(dt_raw + dtb_ref[...])
    live = lane < n_heads
    if nvalid < L:
        live = live & (row < nvalid)
    dt = jnp.where(live, dt, 0.0)
    dta = dt * (-jnp.exp(alog_ref[...]))
    tril = row >= lane
    acum = jnp.dot(tril.astype(F32), dta, precision=lax.Precision.HIGHEST, preferred_element_type=F32)
    last = acum[L - 1:L, :]
    acum_t = acum.T
    eb = e_ref[...]
    stacked = jnp.concatenate([jnp.exp(last - acum) * dt, jnp.exp(acum), dt], axis=0)
    ex = _split_dot(stacked, eb)
    xw = (xs * ex[0:L]).astype(BF16)
    eac_x = ex[L:2 * L]
    xd = xs * ex[2 * L:3 * L]

    cd_col = jnp.broadcast_to(jnp.exp(acum_t[:, L - 1:L]), (LANES, LANES))
    cd_rows = _split_dot_left(r_ref[...], cd_col)

    h0l = lane < SSM_HD
    y_parts = []
    yoff_parts = []
    for g in range(SSM_GROUPS):
        bg = bm[:, g * D_STATE:(g + 1) * D_STATE]
        cg = cm[:, g * D_STATE:(g + 1) * D_STATE]
        cbg = _dot_nt(cg, bg)
        rows = slice(g * gw, (g + 1) * gw)
        hg = h_sc[rows, :]
        yoff_parts.append(_dot_nt(cg, hg.astype(BF16)))
        st = lax.dot_general(xw[:, rows], bg, (((0,), (0,)), ((), ())), preferred_element_type=F32)
        h_sc[rows, :] = cd_rows[rows, :] * hg + st
        for jp in range(hpg // 2):
            ms = []
            for hh in range(2):
                h = g * hpg + 2 * jp + hh
                diff = acum[:, h:h + 1] - acum_t[h:h + 1, :]
                ms.append(cbg * jnp.exp(jnp.where(tril, diff, NEG)))
            lhs = jnp.concatenate(ms, axis=1).astype(BF16)
            col = (g * hpg + 2 * jp) * SSM_HD
            xp = xd[:, col:col + LANES]
            rhs = jnp.concatenate([jnp.where(h0l, xp, 0.0), jnp.where(h0l, 0.0, xp)], axis=0).astype(BF16)
            y_parts.append(_dot(lhs, rhs))
    y = jnp.concatenate(y_parts, axis=1) + jnp.concatenate(yoff_parts, axis=1) * eac_x + dskx_ref[...] * xs
    if nvalid < L:
        y = y[:8]
        z = jnp.concatenate([z_ref[0], jnp.zeros((8 - nvalid, d_ssm), F32)], axis=0)
    else:
        z = z_ref[...]
    y = y * (z * jax.nn.sigmoid(z))
    outs = []
    for g in range(SSM_GROUPS):
        yg = y[:, g * gw:(g + 1) * gw]
        outs.append(yg * lax.rsqrt(jnp.mean(yg * yg, axis=-1, keepdims=True) + EPS))
    y = jnp.concatenate(outs, axis=1) * nw_ref[...]
    if nvalid < L:
        y_ref[0] = y[:nvalid].astype(y_ref.dtype)
    else:
        y_ref[...] = y.astype(y_ref.dtype)
    hout_ref[0] = h_sc[...]


def _split_dot_left(onehot_bf16, v):
    p1 = v.astype(BF16)
    r1 = v - p1.astype(F32)
    p2 = r1.astype(BF16)
    p3 = (r1 - p2.astype(F32)).astype(BF16)
    return _dot(onehot_bf16, p1) + _dot(onehot_bf16, p2) + _dot(onehot_bf16, p3)


def _ssd(xbc_src, z_src, dt_src, conv_prev8, h0, consts, *, batch, n_chunks, nvalid, xbc_spec, z_spec, dt_spec,
         d_ssm, n_heads):
    conv_dim = conv_prev8.shape[-1]
    hp = h0.shape[1]
    body = functools.partial(_ssd_body, nvalid=nvalid, d_ssm=d_ssm, n_heads=n_heads)
    cw, cb, dtb, alog, dskx, nw, e_mat, r_mat = consts

    def const(a):
        return pl.BlockSpec(a.shape, lambda b, c: (0,) * a.ndim)

    if nvalid < CHUNK:
        assert n_chunks == 1 and nvalid <= 8
        y_spec = pl.BlockSpec((1, nvalid, d_ssm), lambda b, c: (b, 0, 0))
        y_shape = jax.ShapeDtypeStruct((batch, nvalid, d_ssm), BF16)
    else:
        y_spec = pl.BlockSpec((CHUNK, d_ssm), lambda b, c: (b * n_chunks + c, 0))
        y_shape = jax.ShapeDtypeStruct((batch * n_chunks * CHUNK, d_ssm), BF16)
    return pl.pallas_call(
        body,
        grid=(batch, n_chunks),
        in_specs=[
            xbc_spec, z_spec, dt_spec,
            pl.BlockSpec((1, 8, conv_dim), lambda b, c: (b, 0, 0)),
            pl.BlockSpec((1, hp, D_STATE), lambda b, c: (b, 0, 0)),
            const(cw), const(cb), const(dtb), const(alog), const(dskx), const(nw), const(e_mat), const(r_mat),
        ],
        out_specs=[
            y_spec,
            pl.BlockSpec((1, 8, conv_dim), lambda b, c: (b, 0, 0)),
            pl.BlockSpec((1, hp, D_STATE), lambda b, c: (b, 0, 0)),
        ],
        out_shape=[
            y_shape,
            jax.ShapeDtypeStruct((batch, 8, conv_dim), F32),
            jax.ShapeDtypeStruct((batch, hp, D_STATE), F32),
        ],
        scratch_shapes=[pltpu.VMEM((hp, D_STATE), F32), pltpu.VMEM((CHUNK + 8, conv_dim), F32)],
        compiler_params=_cparams(("parallel", "arbitrary"), 48),
        name="ssd",
    )(xbc_src, z_src, dt_src, conv_prev8, h0, cw, cb, dtb, alog, dskx, nw, e_mat, r_mat)


def _outproj_body(x_ref, ya_ref, ys_ref, wa_ref, ws_ref, nfw_ref, wr_ref, br_ref,
                  xmid_ref, xpk_ref, idx_ref, gate_ref):
    tm, d = x_ref.shape
    pr = tm // OUTPROJ_PARTS
    parts = range(OUTPROJ_PARTS)
    rs = [pl.ds(i * pr, pr) for i in parts]
    mix = [_dot(ya_ref[r, :].astype(BF16), wa_ref[...]) + _dot(ys_ref[r, :], ws_ref[...]) for r in rs]
    xm = [x_ref[rs[i], :] + mix[i] for i in parts]
    for i in parts:
        xmid_ref[rs[i], :] = xm[i]
    xn = [xm[i] * lax.rsqrt(jnp.mean(xm[i] * xm[i], axis=-1, keepdims=True) + EPS) * nfw_ref[...] for i in parts]
    xh = [xn[i].astype(BF16) for i in parts]
    xhf = [xh[i].astype(F32) for i in parts]
    for i in parts:
        _store_packed(xpk_ref, pr, _pack_words(xh[i][:, :d // 2], xh[i][:, d // 2:]), row0=i * pr)
    xl = [(xn[i] - xhf[i]).astype(BF16) for i in parts]
    hl = [_dot(xh[i], wr_ref[...]) for i in parts]
    lh = [_dot(xl[i], wr_ref[:, :LANES]) for i in parts]
    logits = [hl[i][:, :LANES] + hl[i][:, LANES:] + lh[i] + br_ref[...] for i in parts]
    lane = lax.broadcasted_iota(I32, (pr, LANES), 1)
    idx_out = [jnp.zeros((pr, LANES), I32) for _ in parts]
    val_out = [jnp.full((pr, LANES), NEG, F32) for _ in parts]
    for k in range(TOP_K):
        mx = [jnp.max(logits[i], axis=1, keepdims=True) for i in parts]
        ix = [jnp.min(jnp.where(logits[i] == mx[i], lane, LANES), axis=1, keepdims=True) for i in parts]
        idx_out = [jnp.where(lane == k, ix[i], idx_out[i]) for i in parts]
        val_out = [jnp.where(lane == k, mx[i], val_out[i]) for i in parts]
        logits = [jnp.where(lane == ix[i], -jnp.inf, logits[i]) for i in parts]
    for i in parts:
        ev = jnp.exp(val_out[i] - jnp.max(val_out[i], axis=1, keepdims=True))
        gate_ref[rs[i], :] = ev / jnp.sum(ev, axis=1, keepdims=True)
        idx_ref[rs[i], :] = idx_out[i]


def _outproj(x_all, y_att, y_ssd, w_att, w_ssd, nfw, w_router, b_router):
    m, d = x_all.shape
    assert d == 2 * PACK_ROWS * LANES
    tm = _row_tile(m, TM_PROJ)

    def rows(width):
        return pl.BlockSpec((tm, width), lambda i: (i, 0))

    def const(a):
        return pl.BlockSpec(a.shape, lambda i: (0,) * a.ndim)

    return pl.pallas_call(
        _outproj_body,
        grid=(m // tm,),
        in_specs=[rows(d), rows(y_att.shape[1]), rows(y_ssd.shape[1]), const(w_att), const(w_ssd), const(nfw),
                  const(w_router), const(b_router)],
        out_specs=[rows(d), pl.BlockSpec((tm * PACK_ROWS, LANES), lambda i: (i, 0)), rows(LANES), rows(LANES)],
        out_shape=[jax.ShapeDtypeStruct((m, d), F32), jax.ShapeDtypeStruct((m * PACK_ROWS, LANES), U32),
                   jax.ShapeDtypeStruct((m, LANES), I32), jax.ShapeDtypeStruct((m, LANES), F32)],
        compiler_params=_cparams(("parallel",), 56),
        name="outproj",
    )(x_all, y_att, y_ssd, w_att, w_ssd, nfw, w_router, b_router)


def _route_body(idx_ref, cin_ref, rank_ref, cout_ref, carry):
    @pl.when(pl.program_id(0) == 0)
    def _():
        carry[...] = cin_ref[...]

    tr = idx_ref.shape[0]
    lane = lax.broadcasted_iota(I32, (tr, LANES), 1)
    tril = (lax.broadcasted_iota(I32, (tr, tr), 0) >= lax.broadcasted_iota(I32, (tr, tr), 1))
    tril = jnp.where(tril, 1.0, 0.0).astype(BF16)
    idx = idx_ref[...]
    cnt = carry[...]
    out = jnp.zeros((tr, LANES), I32)
    for k in range(TOP_K):
        hit = lane == idx[:, k:k + 1]
        pre = _dot(tril, jnp.where(hit, 1.0, 0.0).astype(BF16))
        rk = jnp.sum(jnp.where(hit, pre + cnt - 1.0, 0.0), axis=1, keepdims=True)
        out = jnp.where(lane == k, rk.astype(I32), out)
        cnt = cnt + pre[tr - 1:tr, :]
    carry[...] = cnt
    rank_ref[...] = out
    cout_ref[...] = cnt


def _route(top_idx, counts_in):
    m = top_idx.shape[0]
    tr = _row_tile(m, TM_ROUTE)
    return pl.pallas_call(
        _route_body,
        grid=(m // tr,),
        in_specs=[pl.BlockSpec((tr, LANES), lambda i: (i, 0)), pl.BlockSpec((1, LANES), lambda i: (0, 0))],
        out_specs=[pl.BlockSpec((tr, LANES), lambda i: (i, 0)), pl.BlockSpec((1, LANES), lambda i: (0, 0))],
        out_shape=[jax.ShapeDtypeStruct((m, LANES), I32), jax.ShapeDtypeStruct((1, LANES), F32)],
        scratch_shapes=[pltpu.VMEM((1, LANES), F32)],
        compiler_params=_cparams(("arbitrary",), 32),
        name="route",
    )(top_idx, counts_in)


def _dispatch_body(pos_ref, x_ref, xd_in, xd_hbm, sem):
    del xd_in
    i = pl.program_id(0)
    ts = x_ref.shape[0] // PACK_ROWS

    def row_copy(r, k):
        p = pos_ref[(i * ts + r) * TOP_K + k]
        src = x_ref.at[pl.ds(pl.multiple_of(r * PACK_ROWS, PACK_ROWS), PACK_ROWS), :]
        dst = xd_hbm.at[pl.ds(pl.multiple_of(p * PACK_ROWS, PACK_ROWS), PACK_ROWS), :]
        return pltpu.make_async_copy(src, dst, sem)

    def issue(r, c):
        for k in range(TOP_K):
            row_copy(r, k).start()
        return c
    lax.fori_loop(0, ts, issue, 0)

    def drain(r, c):
        for k in range(TOP_K):
            row_copy(r, k).wait()
        return c
    lax.fori_loop(0, ts, drain, 0)


def _dispatch(pos_flat, x_packed, x_disp):
    m = x_packed.shape[0] // PACK_ROWS
    ts = _row_tile(m, TM_DISPATCH)
    return pl.pallas_call(
        _dispatch_body,
        grid_spec=pltpu.PrefetchScalarGridSpec(
            num_scalar_prefetch=1,
            grid=(m // ts,),
            in_specs=[pl.BlockSpec((ts * PACK_ROWS, LANES), lambda i, pos: (i, 0)),
                      pl.BlockSpec(memory_space=pl.ANY)],
            out_specs=pl.BlockSpec(memory_space=pl.ANY),
            scratch_shapes=[pltpu.SemaphoreType.DMA(())],
        ),
        out_shape=jax.ShapeDtypeStruct(x_disp.shape, x_disp.dtype),
        input_output_aliases={2: 0},
        compiler_params=_cparams(("arbitrary",), 32),
        name="dispatch",
    )(pos_flat, x_packed, x_disp)


def _tile_changed(te_ref, i):
    return (i == 0) | (te_ref[i] != te_ref[jnp.maximum(i - 1, 0)])


def _for_live_rows(i, nu_ref, tr_ref, tm, compute):
    live = i < nu_ref[0]
    small = tr_ref[i] <= tm // 2

    @pl.when(live & jnp.logical_not(small))
    def _():
        compute(tm)

    @pl.when(live & small)
    def _():
        compute(tm // 2)


def _gmm1_body(te_ref, nu_ref, tr_ref, x_ref, wg_ref, wu_ref, bg_ref, bu_ref, h_ref, wgb, wub):
    i = pl.program_id(1)
    tm = h_ref.shape[0]

    @pl.when(_tile_changed(te_ref, i))
    def _():
        wgb[...] = wg_ref[0].astype(BF16)
        wub[...] = wu_ref[0].astype(BF16)

    def compute(rows):
        x = _load_packed(lambda s: x_ref[pl.ds(s, rows, stride=PACK_ROWS), :], BF16)
        gt = _dot(x, wgb[...]) + bg_ref[0]
        up = _dot(x, wub[...]) + bu_ref[0]
        gt = jnp.minimum(gt, SWIGLU_LIMIT)
        up = jnp.clip(up, -SWIGLU_LIMIT, SWIGLU_LIMIT)
        act = (up + 1.0) * gt * jax.nn.sigmoid(SWIGLU_ALPHA * gt)
        h_ref[0:rows, :] = act.astype(h_ref.dtype)
        if rows < tm:
            h_ref[rows:, :] = jnp.zeros((tm - rows, h_ref.shape[1]), h_ref.dtype)

    _for_live_rows(i, nu_ref, tr_ref, tm, compute)

    @pl.when(i >= nu_ref[0])
    def _():
        h_ref[...] = jnp.zeros(h_ref.shape, h_ref.dtype)


def _gmm2_body(te_ref, nu_ref, tr_ref, h_ref, wd_ref, bd_ref, y_ref, wdb):
    i = pl.program_id(0)
    tm, half = h_ref.shape[0], wdb.shape[1] // 2

    @pl.when(_tile_changed(te_ref, i))
    def _():
        wdb[...] = wd_ref[0].astype(BF16)

    def compute(rows):
        h = h_ref[0:rows, :]
        lo = _dot(h, wdb[:, :half]) + bd_ref[0][:, :half]
        hi = _dot(h, wdb[:, half:]) + bd_ref[0][:, half:]
        _store_packed(y_ref, rows, _pack_words(lo.astype(BF16), hi.astype(BF16)))
        if rows < tm:
            y_ref[rows * PACK_ROWS:, :] = jnp.zeros(((tm - rows) * PACK_ROWS, LANES), y_ref.dtype)

    _for_live_rows(i, nu_ref, tr_ref, tm, compute)

    @pl.when(i >= nu_ref[0])
    def _():
        y_ref[...] = jnp.zeros(y_ref.shape, y_ref.dtype)


def _used_tile(i, nu):
    return jnp.minimum(i, nu[0] - 1)


def _gmm1(tile_exp, n_used, tile_rows, x_disp, w_gate, w_up, b_gate, b_up):
    cap = x_disp.shape[0] // PACK_ROWS
    d, dff = w_gate.shape[1], w_gate.shape[2]
    tm, tn = TM_GMM, TN_GMM
    x_spec = pl.BlockSpec((tm * PACK_ROWS, LANES), lambda j, i, te, nu, tr: (_used_tile(i, nu), 0))
    w_spec = pl.BlockSpec((1, d, tn), lambda j, i, te, nu, tr: (te[_used_tile(i, nu)], 0, j))
    b_spec = pl.BlockSpec((1, 1, tn), lambda j, i, te, nu, tr: (te[_used_tile(i, nu)], 0, j))
    return pl.pallas_call(
        _gmm1_body,
        grid_spec=pltpu.PrefetchScalarGridSpec(
            num_scalar_prefetch=3,
            grid=(dff // tn, cap // tm),
            in_specs=[x_spec, w_spec, w_spec, b_spec, b_spec],
            out_specs=pl.BlockSpec((tm, tn), lambda j, i, te, nu, tr: (i, j)),
            scratch_shapes=[pltpu.VMEM((d, tn), BF16), pltpu.VMEM((d, tn), BF16)],
        ),
        out_shape=jax.ShapeDtypeStruct((cap, dff), BF16),
        compiler_params=_cparams(("arbitrary", "arbitrary"), 60),
        name="gmm1",
    )(tile_exp, n_used, tile_rows, x_disp, w_gate, w_up, b_gate, b_up)


def _gmm2(tile_exp, n_used, tile_rows, h_disp, w_down, b_down):
    cap, dff = h_disp.shape
    d = w_down.shape[2]
    assert d == 2 * PACK_ROWS * LANES
    tm = TM_GMM
    return pl.pallas_call(
        _gmm2_body,
        grid_spec=pltpu.PrefetchScalarGridSpec(
            num_scalar_prefetch=3,
            grid=(cap // tm,),
            in_specs=[
                pl.BlockSpec((tm, dff), lambda i, te, nu, tr: (_used_tile(i, nu), 0)),
                pl.BlockSpec((1, dff, d), lambda i, te, nu, tr: (te[_used_tile(i, nu)], 0, 0)),
                pl.BlockSpec((1, 1, d), lambda i, te, nu, tr: (te[_used_tile(i, nu)], 0, 0)),
            ],
            out_specs=pl.BlockSpec((tm * PACK_ROWS, LANES), lambda i, te, nu, tr: (i, 0)),
            scratch_shapes=[pltpu.VMEM((dff, d), BF16)],
        ),
        out_shape=jax.ShapeDtypeStruct((cap * PACK_ROWS, LANES), U32),
        compiler_params=_cparams(("arbitrary",), 60),
        name="gmm2",
    )(tile_exp, n_used, tile_rows, h_disp, w_down, b_down)


def _combine_body(pos_ref, y_hbm, xmid_ref, gate_ref, fw_ref, o_ref, buf, sem):
    i = pl.program_id(0)
    n = pl.num_programs(0)
    tc = TOK_COMBINE

    def row_copy(step, slot, r, k):
        p = pos_ref[(step * tc + r) * TOP_K + k]
        src = y_hbm.at[pl.ds(pl.multiple_of(p * PACK_ROWS, PACK_ROWS), PACK_ROWS), :]
        dst = buf.at[slot, k, pl.ds(pl.multiple_of(r * PACK_ROWS, PACK_ROWS), PACK_ROWS), :]
        return pltpu.make_async_copy(src, dst, sem.at[slot])

    def issue(step, slot):
        def body(r, c):
            for k in range(TOP_K):
                row_copy(step, slot, r, k).start()
            return c
        lax.fori_loop(0, tc, body, 0)

    @pl.when(i == 0)
    def _():
        issue(0, 0)

    @pl.when(i + 1 < n)
    def _():
        issue(i + 1, (i + 1) % 2)

    slot = i % 2

    def wait_body(r, c):
        for k in range(TOP_K):
            row_copy(i, slot, r, k).wait()
        return c
    lax.fori_loop(0, tc, wait_body, 0)

    gates = gate_ref[...]
    acc = xmid_ref[...]
    for k in range(TOP_K):
        y_k = _load_packed(lambda s, k=k: buf[slot, k, pl.ds(s, tc, stride=PACK_ROWS), :], F32)
        acc = acc + gates[:, k:k + 1] * y_k
    o_ref[...] = acc * lax.rsqrt(jnp.mean(acc * acc, axis=-1, keepdims=True) + EPS) * fw_ref[...]


def _combine(pos_flat, y_disp, x_mid, gates, final_w):
    m, d = x_mid.shape
    tc = TOK_COMBINE
    return pl.pallas_call(
        _combine_body,
        grid_spec=pltpu.PrefetchScalarGridSpec(
            num_scalar_prefetch=1,
            grid=(m // tc,),
            in_specs=[
                pl.BlockSpec(memory_space=pl.ANY),
                pl.BlockSpec((tc, d), lambda i, pos: (i, 0)),
                pl.BlockSpec((tc, LANES), lambda i, pos: (i, 0)),
                pl.BlockSpec((1, d), lambda i, pos: (0, 0)),
            ],
            out_specs=pl.BlockSpec((tc, d), lambda i, pos: (i, 0)),
            scratch_shapes=[pltpu.VMEM((2, TOP_K, tc * PACK_ROWS, LANES), U32), pltpu.SemaphoreType.DMA((2,))],
        ),
        out_shape=jax.ShapeDtypeStruct((m, d), F32),
        compiler_params=_cparams(("arbitrary",), 32),
        name="combine",
    )(pos_flat, y_disp, x_mid, gates, final_w)


def _tile_layout(counts, n_assign, n_exp):
    tiles = (counts + TM_GMM - 1) // TM_GMM
    tile_end = jnp.cumsum(tiles)
    row_start = (tile_end - tiles) * TM_GMM
    n_tiles = -(-n_assign // TM_GMM) + n_exp
    n_used = tile_end[-1]
    tile_ids = jnp.minimum(jnp.arange(n_tiles, dtype=I32), n_used - 1)
    tile_exp = jnp.minimum(jnp.sum((tile_end[None, :] <= tile_ids[:, None]).astype(I32), axis=1), n_exp - 1)
    onehot = tile_exp[:, None] == jnp.arange(n_exp, dtype=I32)[None, :]
    first_tile = jnp.sum(jnp.where(onehot, (tile_end - tiles)[None, :], 0), axis=1)
    count_e = jnp.sum(jnp.where(onehot, counts[None, :], 0), axis=1)
    tile_rows = jnp.clip(count_e - (tile_ids - first_tile) * TM_GMM, 0, TM_GMM)
    return (row_start.astype(I32), tile_exp.astype(I32), n_used.reshape(1).astype(I32), tile_rows.astype(I32),
            n_tiles * TM_GMM)


def _positions(top_idx, rank, row_start, n_exp):
    idx = top_idx[:, :TOP_K]
    onehot = idx[:, :, None] == jnp.arange(n_exp, dtype=I32)[None, None, :]
    base = jnp.sum(jnp.where(onehot, row_start[None, None, :], 0), axis=-1)
    return (base + rank[:, :TOP_K]).reshape(-1).astype(I32)


def kernel(x_prompt, x_sample, cache_attn_k, cache_attn_v, state_conv, state_ssm, norm_mix_w, w_in, conv_w, conv_b, dt_bias, a_log, d_skip, ssd_norm_w, w_out, norm_ffn_w, w_router, b_router, w_gate, b_gate, w_up, b_up, w_down, b_down, norm_final_w):
    depth = norm_mix_w.shape[0]
    assert depth == 1, "single-layer step"
    bp, seq, d = x_prompt.shape
    bs, dq, _ = x_sample.shape
    n_att_heads = cache_attn_k.shape[3]
    d_att = n_att_heads * HD_ATT
    n_pairs = d_att // LANES
    n_heads = a_log.shape[1]
    d_ssm = n_heads * SSM_HD
    conv_dim = conv_w.shape[2]
    n_exp = w_router.shape[2]
    n_prompt = bp * seq
    n_sample = bs * dq
    assert n_sample == CHUNK and seq % SUPER == 0 and n_heads <= LANES

    wl = w_in[0]
    o_q, o_k, o_v, o_z, o_x, o_dt = 0, d_att, 2 * d_att, 3 * d_att, 3 * d_att + d_ssm, 3 * d_att + d_ssm + conv_dim
    w_main = jnp.concatenate([wl[:, o_x:o_dt], wl[:, o_q:o_z], wl[:, o_z:o_x]], axis=1).astype(BF16)
    w_dt = jnp.pad(wl[:, o_dt:], ((0, 0), (0, LANES - n_heads))).astype(BF16)
    col_q = conv_dim // LANES
    col_k = col_q + n_pairs
    col_v = col_k + n_pairs
    w_att = w_out[0, :d_att].astype(BF16)
    w_ssd = w_out[0, d_att:].astype(BF16)
    w_rp = jnp.pad(w_router[0], ((0, 0), (0, LANES - n_exp)))
    w_rh = w_rp.astype(BF16)
    w_r = jnp.concatenate([w_rh, (w_rp - w_rh.astype(F32)).astype(BF16)], axis=1)
    b_r = jnp.pad(b_router[0], (0, LANES - n_exp), constant_values=NEG).reshape(1, LANES)
    pad_h = (0, LANES - n_heads)
    head_of_col = jnp.arange(d_ssm, dtype=I32) // SSM_HD
    e_mat = (jnp.arange(LANES, dtype=I32)[:, None] == head_of_col[None, :]).astype(BF16)
    r_mat = (head_of_col[:, None] == jnp.arange(LANES, dtype=I32)[None, :]).astype(BF16)
    ssd_consts = (
        conv_w[0], conv_b[0].reshape(1, conv_dim),
        jnp.pad(dt_bias[0], pad_h).reshape(1, LANES), jnp.pad(a_log[0], pad_h).reshape(1, LANES),
        jnp.repeat(d_skip[0], SSM_HD).reshape(1, d_ssm), ssd_norm_w[0].reshape(1, d_ssm), e_mat, r_mat,
    )

    xp = x_prompt.reshape(n_prompt, d)
    xs = x_sample.reshape(n_sample, d)
    nmw = norm_mix_w[0].reshape(1, d)
    proj_p, dt_p = _inproj(xp, nmw, w_main, w_dt)
    proj_s, dt_s = _inproj(xs, nmw, w_main, w_dt)
    n_main = proj_p.shape[1]
    z_col = (conv_dim + 3 * d_att) // d_ssm
    assert z_col * d_ssm == conv_dim + 3 * d_att

    win_p = min(BRANCHES[-1][0], seq)
    y_att_p, k_p, v_p = _attn_prompt(proj_p, bp, seq, win_p, col_q, col_k, col_v, n_pairs)
    nbuf = cache_attn_k.shape[2]
    y_att_s = _attn_sample(proj_s, _compact_cache(cache_attn_k[0], dq), _compact_cache(cache_attn_v[0], dq),
                           col_q, col_k, col_v, n_pairs, bs, dq, nbuf)

    n_chunks = seq // CHUNK
    hp = n_heads * SSM_HD
    y_ssd_p, cnew_p, h_p = _ssd(
        proj_p, proj_p, dt_p, jnp.zeros((bp, 8, conv_dim), F32), jnp.zeros((bp, hp, D_STATE), F32), ssd_consts,
        batch=bp, n_chunks=n_chunks, nvalid=CHUNK,
        xbc_spec=pl.BlockSpec((CHUNK, conv_dim), lambda b, c: (b * n_chunks + c, 0)),
        z_spec=pl.BlockSpec((CHUNK, d_ssm), lambda b, c: (b * n_chunks + c, z_col)),
        dt_spec=pl.BlockSpec((CHUNK, LANES), lambda b, c: (b * n_chunks + c, 0)),
        d_ssm=d_ssm, n_heads=n_heads)

    proj_s3 = proj_s.reshape(bs, dq, n_main)
    cprev_s = jnp.pad(state_conv[0], ((0, 0), (8 - (CONV_W - 1), 0), (0, 0)))
    y_ssd_s, cnew_s, h_s = _ssd(
        proj_s3, proj_s3, dt_s.reshape(bs, dq, LANES), cprev_s, state_ssm[0].reshape(bs, hp, D_STATE), ssd_consts,
        batch=bs, n_chunks=1, nvalid=dq,
        xbc_spec=pl.BlockSpec((1, dq, conv_dim), lambda b, c: (b, 0, 0)),
        z_spec=pl.BlockSpec((1, dq, d_ssm), lambda b, c: (b, 0, z_col)),
        dt_spec=pl.BlockSpec((1, dq, LANES), lambda b, c: (b, 0, 0)),
        d_ssm=d_ssm, n_heads=n_heads)

    nfw = norm_ffn_w[0].reshape(1, d)
    xmid_p, xpk_p, idx_p, gates_p = _outproj(xp, y_att_p, y_ssd_p, w_att, w_ssd, nfw, w_r, b_r)
    xmid_s, xpk_s, idx_s, gates_s = _outproj(xs, y_att_s.reshape(n_sample, d_att),
                                             y_ssd_s.reshape(n_sample, d_ssm), w_att, w_ssd, nfw, w_r, b_r)

    rank_p, cnt_p = _route(idx_p, jnp.zeros((1, LANES), F32))
    rank_s, cnt_all = _route(idx_s, cnt_p)
    counts = cnt_all[0, :n_exp].astype(I32)
    n_assign = (n_prompt + n_sample) * TOP_K
    row_start, tile_exp, n_used, tile_rows, cap = _tile_layout(counts, n_assign, n_exp)
    pos_p = _positions(idx_p, rank_p, row_start, n_exp)
    pos_s = _positions(idx_s, rank_s, row_start, n_exp)
    x_disp = jnp.zeros((cap * PACK_ROWS, LANES), U32)
    x_disp = _dispatch(pos_p, xpk_p, x_disp)
    x_disp = _dispatch(pos_s, xpk_s, x_disp)
    h_disp = _gmm1(tile_exp, n_used, tile_rows, x_disp, w_gate[0], w_up[0],
                   b_gate[0].reshape(n_exp, 1, -1), b_up[0].reshape(n_exp, 1, -1))
    y_disp = _gmm2(tile_exp, n_used, tile_rows, h_disp, w_down[0], b_down[0].reshape(n_exp, 1, -1))
    fw = norm_final_w.reshape(1, d)
    y_p = _combine(pos_p, y_disp, xmid_p, gates_p, fw)
    y_s = _combine(pos_s, y_disp, xmid_s, gates_s, fw)

    k_p = k_p.reshape(1, bp, win_p, n_att_heads, HD_ATT)
    v_p = v_p.reshape(1, bp, win_p, n_att_heads, HD_ATT)
    k_s = proj_s3[:, :, col_k * LANES:col_v * LANES].reshape(1, bs, dq, n_att_heads, HD_ATT)
    v_s = proj_s3[:, :, col_v * LANES:col_v * LANES + d_att].reshape(1, bs, dq, n_att_heads, HD_ATT)
    return (
        y_p.reshape(bp, seq, d), y_s.reshape(bs, dq, d), k_p, v_p,
        cnew_p[:, 8 - (CONV_W - 1):][None], h_p.reshape(1, bp, n_heads, SSM_HD, D_STATE),
        k_s, v_s,
        cnew_s[:, 8 - (CONV_W - 1):][None], h_s.reshape(1, bs, n_heads, SSM_HD, D_STATE),
    )
```

```python
import functools

import jax
import jax.numpy as jnp
from jax import lax
from jax.experimental import pallas as pl
from jax.experimental.pallas import tpu as pltpu

F32 = jnp.float32
BF16 = jnp.bfloat16
I32 = jnp.int32
U32 = jnp.uint32

EPS = 1e-5
NEG = -1e30
LANES = 128
HD_ATT = 64
BRANCHES = ((128, 1), (512, 4), (2048, 16))
RES = 16
ATT_BLK = 128
ATT_BLOCKS_PER_STEP = 2
SUPER = RES * ATT_BLK
SSM_HD = 64
SSM_GROUPS = 4
D_STATE = 128
CONV_W = 4
CHUNK = 128
TOP_K = 4
SWIGLU_LIMIT = 7.0
SWIGLU_ALPHA = 1.702

PACK_ROWS = 8

TM_PROJ = 512
OUTPROJ_PARTS = 2
TN_PROJ = 2048
TM_ROUTE = 512
TM_DISPATCH = 512
TM_GMM = 512
TN_GMM = 1024
TOK_COMBINE = 128


def _row_tile(m, cap):
    t = cap - cap % LANES
    while m % t:
        t -= LANES
    return t


def _cparams(sem, vmem_mb):
    return pltpu.CompilerParams(dimension_semantics=sem, vmem_limit_bytes=vmem_mb << 20)


def _dot(a, b):
    return jnp.dot(a, b, preferred_element_type=F32)


def _dot_nt(a, b):
    return lax.dot_general(a, b, (((1,), (1,)), ((), ())), preferred_element_type=F32)


def _split_dot(v, onehot_bf16):
    hi = v.astype(BF16)
    lo = (v - hi.astype(F32)).astype(BF16)
    return _dot(hi, onehot_bf16) + _dot(lo, onehot_bf16)


def _pack_words(lo, hi):
    lo_bits = lax.bitcast_convert_type(lo.astype(F32), U32)
    hi_bits = lax.bitcast_convert_type(hi.astype(F32), U32)
    return (hi_bits & jnp.uint32(0xFFFF0000)) | (lo_bits >> 16)


def _store_packed(ref, rows, words, row0=0):
    for s in range(PACK_ROWS):
        ref[pl.ds(row0 * PACK_ROWS + s, rows, stride=PACK_ROWS), :] = words[:, s * LANES:(s + 1) * LANES]


def _load_packed(load_slab, dtype):
    lo, hi = [], []
    for s in range(PACK_ROWS):
        w = load_slab(s)
        lo.append(lax.bitcast_convert_type(w << 16, F32).astype(dtype))
        hi.append(lax.bitcast_convert_type(w & jnp.uint32(0xFFFF0000), F32).astype(dtype))
    return jnp.concatenate(lo + hi, axis=1)


def _inproj_body(x_ref, nw_ref, w_ref, wdt_ref, o_ref, dt_ref, h_sc):
    @pl.when(pl.program_id(1) == 0)
    def _():
        x = x_ref[...]
        ms = jnp.mean(x * x, axis=-1, keepdims=True)
        h = (x * lax.rsqrt(ms + EPS) * nw_ref[...]).astype(BF16)
        h_sc[...] = h
        dt_ref[...] = _dot(h, wdt_ref[...])

    o_ref[...] = _dot(h_sc[...], w_ref[...])


def _inproj(x_all, norm_w, w_main, w_dt):
    m, d = x_all.shape
    n = w_main.shape[1]
    tm, tn = _row_tile(m, TM_PROJ), TN_PROJ
    return pl.pallas_call(
        _inproj_body,
        grid=(m // tm, n // tn),
        in_specs=[
            pl.BlockSpec((tm, d), lambda i, j: (i, 0)),
            pl.BlockSpec((1, d), lambda i, j: (0, 0)),
            pl.BlockSpec((d, tn), lambda i, j: (0, j)),
            pl.BlockSpec((d, LANES), lambda i, j: (0, 0)),
        ],
        out_specs=[
            pl.BlockSpec((tm, tn), lambda i, j: (i, j)),
            pl.BlockSpec((tm, LANES), lambda i, j: (i, 0)),
        ],
        out_shape=[jax.ShapeDtypeStruct((m, n), F32), jax.ShapeDtypeStruct((m, LANES), F32)],
        scratch_shapes=[pltpu.VMEM((tm, d), BF16)],
        compiler_params=_cparams(("parallel", "arbitrary"), 48),
        name="inproj",
    )(x_all, norm_w, w_main, w_dt)


def _attn_blocks(args):
    lane = lax.broadcasted_iota(I32, (ATT_BLK, LANES), 1)
    h0 = lane < HD_ATT
    hk = lax.broadcasted_iota(I32, (2 * ATT_BLK, LANES), 1) < HD_ATT
    n = len(args)
    qs = [jnp.concatenate([jnp.where(h0, a[0], 0.0), jnp.where(h0, 0.0, a[0])], axis=0).astype(BF16) for a in args]
    kk = [jnp.concatenate([a[1], a[2]], axis=0).astype(BF16) for a in args]
    s = [_dot_nt(qs[i], kk[i]) for i in range(n)]
    s = [s[i] + jnp.concatenate([args[i][8], args[i][8]], axis=0) for i in range(n)]
    mn = [jnp.maximum(args[i][5], jnp.max(s[i], axis=1, keepdims=True)) for i in range(n)]
    p = [jnp.exp(s[i] - mn[i]).astype(BF16) for i in range(n)]
    vv = [jnp.concatenate([a[3], a[4]], axis=0) for a in args]
    v0 = [jnp.where(hk, vv[i], 1.0).astype(BF16) for i in range(n)]
    v1 = [jnp.where(hk, 1.0, vv[i]).astype(BF16) for i in range(n)]
    o0 = [_dot(p[i][:ATT_BLK], v0[i]) for i in range(n)]
    o1 = [_dot(p[i][ATT_BLK:], v1[i]) for i in range(n)]
    al = [jnp.exp(args[i][5] - mn[i]) for i in range(n)]
    a0 = [jnp.broadcast_to(al[i][:ATT_BLK], (ATT_BLK, LANES)) for i in range(n)]
    a1 = [jnp.broadcast_to(al[i][ATT_BLK:], (ATT_BLK, LANES)) for i in range(n)]
    ln = [jnp.where(h0, a1[i], a0[i]) * args[i][6] + jnp.where(h0, o1[i], o0[i]) for i in range(n)]
    an = [jnp.where(h0, a0[i], a1[i]) * args[i][7] + jnp.where(h0, o0[i], o1[i]) for i in range(n)]
    return [(mn[i], ln[i], an[i]) for i in range(n)]


def _attn_prompt_body(q_ref, k_ref, v_ref, o_ref, kw_ref, vw_ref, qr, kr, vr, m_sc, l_sc, acc_sc, bias_sc, *, seq):
    ni = seq // RES
    scale = HD_ATT ** -0.5
    win = kw_ref.shape[0]
    kw_ref[...] = k_ref[seq - win:, :]
    vw_ref[...] = v_ref[seq - win:, :]
    for j in range(RES):
        qr[j] = q_ref[pl.ds(j, ni, stride=RES), :] * scale
        kr[j] = k_ref[pl.ds(j, ni, stride=RES), :]
        vr[j] = v_ref[pl.ds(j, ni, stride=RES), :]

    rq = lax.broadcasted_iota(I32, (ATT_BLK, 2 * ATT_BLK), 0)
    cc = lax.broadcasted_iota(I32, (ATT_BLK, 2 * ATT_BLK), 1)
    is_cur = cc >= ATT_BLK
    rk = jnp.where(is_cur, cc - ATT_BLK, cc)
    for br, (_, dil) in enumerate(BRANCHES):
        nseg = RES // dil
        seg = ATT_BLK // nseg
        oq = nseg * (rq % seg) + rq // seg
        ok = nseg * (rk % seg) + rk // seg
        allowed = (is_cur & (ok <= oq)) | (jnp.logical_not(is_cur) & (ok >= oq))
        bias_sc[2 * br + 1] = jnp.where(allowed, 0.0, NEG)
        bias_sc[2 * br] = jnp.where(allowed & is_cur, 0.0, NEG)

    def run_block(br, rows, qoff, poff, has_prev):
        seg = ATT_BLK // len(rows)
        q = jnp.concatenate([qr[r, pl.ds(qoff, seg), :] for r in rows], axis=0)
        kc = jnp.concatenate([kr[r, pl.ds(qoff, seg), :] for r in rows], axis=0)
        vc = jnp.concatenate([vr[r, pl.ds(qoff, seg), :] for r in rows], axis=0)
        kp = jnp.concatenate([kr[r, pl.ds(poff, seg), :] for r in rows], axis=0)
        vp = jnp.concatenate([vr[r, pl.ds(poff, seg), :] for r in rows], axis=0)
        return q, kp, kc, vp, vc

    def super_block(u, carry):
        base = pl.multiple_of(u * ATT_BLK, ATT_BLK)
        m_sc[...] = jnp.full(m_sc.shape, NEG, F32)
        l_sc[...] = jnp.zeros(l_sc.shape, F32)
        acc_sc[...] = jnp.zeros(acc_sc.shape, F32)

        def branch_steps(br, specs):
            loaded = []
            for rows, loc, seg in specs:
                qoff = pl.multiple_of(base + loc, 8)
                poff = pl.multiple_of(jnp.maximum(base + loc - seg, 0), 8)
                has_prev = (base + loc >= seg).astype(I32)
                q, kp, kc, vp, vc = run_block(br, rows, qoff, poff, has_prev)
                bias = bias_sc[2 * br + has_prev]
                sl = pl.ds(pl.multiple_of(loc, 8), seg)
                mo = jnp.concatenate([m_sc[h, r, sl, :] for h in range(2) for r in rows], axis=0)
                lo = jnp.concatenate([l_sc[r, sl, :] for r in rows], axis=0)
                ao = jnp.concatenate([acc_sc[r, sl, :] for r in rows], axis=0)
                loaded.append((q, kp, kc, vp, vc, mo, lo, ao, bias))
            results = _attn_blocks(loaded)
            for (rows, loc, seg), (mn, ln, an) in zip(specs, results):
                sl = pl.ds(pl.multiple_of(loc, 8), seg)
                for h in range(2):
                    for si, r in enumerate(rows):
                        lo_r = h * ATT_BLK + si * seg
                        m_sc[h, r, sl, :] = mn[lo_r:lo_r + seg]
                for si, r in enumerate(rows):
                    acc_sc[r, sl, :] = an[si * seg:(si + 1) * seg]
                    l_sc[r, sl, :] = ln[si * seg:(si + 1) * seg]

        nb = ATT_BLOCKS_PER_STEP

        def b1(n, c):
            branch_steps(0, [(list(range(RES)), (n * nb + uu) * 8, 8) for uu in range(nb)])
            return c
        lax.fori_loop(0, RES // nb, b1, 0)

        def b4(t, c):
            specs = []
            for uu in range(nb):
                tt = t * nb + uu
                specs.append(([tt // 4 + 4 * m for m in range(4)], (tt % 4) * 32, 32))
            branch_steps(1, specs)
            return c
        lax.fori_loop(0, RES // nb, b4, 0)

        def b16(j, c):
            branch_steps(2, [([j * nb + uu], 0, ATT_BLK) for uu in range(nb)])
            return c
        lax.fori_loop(0, RES // nb, b16, 0)

        for j in range(RES):
            den = pltpu.roll(l_sc[j], HD_ATT, axis=1)
            o_ref[pl.ds(u * SUPER + j, ATT_BLK, stride=RES), :] = acc_sc[j] / den
        return carry

    lax.fori_loop(0, seq // SUPER, super_block, 0)


def _attn_prompt(proj, batch, seq, win, col_q, col_k, col_v, n_pairs):
    ni = seq // RES
    body = functools.partial(_attn_prompt_body, seq=seq)

    def spec(col):
        return pl.BlockSpec((seq, LANES), lambda b, p: (b, col + p))

    wspec = pl.BlockSpec((win, LANES), lambda b, p: (b, p))
    wshape = jax.ShapeDtypeStruct((batch * win, n_pairs * LANES), F32)
    return pl.pallas_call(
        body,
        grid=(batch, n_pairs),
        in_specs=[spec(col_q), spec(col_k), spec(col_v)],
        out_specs=[pl.BlockSpec((seq, LANES), lambda b, p: (b, p)), wspec, wspec],
        out_shape=[jax.ShapeDtypeStruct((batch * seq, n_pairs * LANES), F32), wshape, wshape],
        scratch_shapes=[
            pltpu.VMEM((RES, ni, LANES), F32),
            pltpu.VMEM((RES, ni, LANES), F32),
            pltpu.VMEM((RES, ni, LANES), F32),
            pltpu.VMEM((2, RES, ATT_BLK, 1), F32),
            pltpu.VMEM((RES, ATT_BLK, LANES), F32),
            pltpu.VMEM((RES, ATT_BLK, LANES), F32),
            pltpu.VMEM((2 * len(BRANCHES), ATT_BLK, 2 * ATT_BLK), F32),
        ],
        compiler_params=_cparams(("parallel", "parallel"), 60),
        name="attn_prompt",
    )(proj, proj, proj)


def _sample_key_rows(nbuf, dq):
    tail = max(w for w, dil in BRANCHES if dil < RES)
    assert nbuf % RES == 0 and tail % RES == 0 and tail <= nbuf and dq <= RES
    return (nbuf - tail) // RES, nbuf - tail


def _attn_sample_body(q_ref, kn_ref, vn_ref, ks_ref, kt_ref, vs_ref, vt_ref, o_ref, *, dq, nbuf, n_pairs):
    b = pl.program_id(0)
    assert 8 % dq == 0
    groups, tail_start = _sample_key_rows(nbuf, dq)
    n_str = groups * 8
    n_tail = kt_ref.shape[1]
    lane = lax.broadcasted_iota(I32, (dq, LANES), 1)
    h0 = lane < HD_ATT
    zpad = jnp.zeros((8 - dq, LANES), F32)

    def counts(delta):
        cnt = jnp.zeros(delta.shape, F32)
        for win, dil in BRANCHES:
            hit = (delta >= 0) & (delta <= win) & (delta % dil == 0)
            cnt = cnt + hit.astype(F32)
        return cnt

    def key_counts(n, pos_of_col):
        col = lax.broadcasted_iota(I32, (2 * dq, n), 1)
        qi = lax.broadcasted_iota(I32, (2 * dq, n), 0) % dq
        return counts(nbuf + qi - pos_of_col(col))

    cnt_s = key_counts(n_str, lambda c: RES * (c // 8) + c % 8)
    cnt_t = key_counts(n_tail, lambda c: tail_start + c)
    qi_n = lax.broadcasted_iota(I32, (2 * dq, 8), 0) % dq
    col_n = lax.broadcasted_iota(I32, (2 * dq, 8), 1)
    cnt_n = jnp.where(col_n < dq, counts(qi_n - col_n), 0.0)

    row8 = pl.multiple_of((b * dq) // 8 * 8, 8)
    sub = (b * dq) % 8 // dq

    def new_rows(ref, cols):
        x8 = ref[pl.ds(row8, 8), cols]
        out = x8[0:dq]
        for j in range(1, 8 // dq):
            out = jnp.where(sub == j, x8[j * dq:(j + 1) * dq], out)
        return out

    for p in range(n_pairs):
        cols = slice(p * LANES, (p + 1) * LANES)
        q = new_rows(q_ref, cols) * (HD_ATT ** -0.5)
        qs = jnp.concatenate([jnp.where(h0, q, 0.0), jnp.where(h0, 0.0, q)], axis=0).astype(BF16)
        kn = jnp.concatenate([new_rows(kn_ref, cols), zpad], axis=0).astype(BF16)
        vn = jnp.concatenate([new_rows(vn_ref, cols), zpad], axis=0).astype(BF16)
        pieces = [
            (ks_ref[0, :, :, cols].reshape(n_str, LANES), vs_ref[0, :, :, cols].reshape(n_str, LANES), cnt_s),
            (kt_ref[0, :, cols], vt_ref[0, :, cols], cnt_t),
            (kn, vn, cnt_n),
        ]
        s = [jnp.where(cnt > 0, _dot_nt(qs, k.astype(BF16)), NEG) for k, _, cnt in pieces]
        m = functools.reduce(jnp.maximum, [jnp.max(si, axis=1, keepdims=True) for si in s])
        e = [cnt * jnp.exp(si - m) for si, (_, _, cnt) in zip(s, pieces)]
        den = sum(jnp.sum(ei, axis=1, keepdims=True) for ei in e)
        num = sum(_dot(ei.astype(BF16), v.astype(BF16)) for ei, (_, v, _) in zip(e, pieces))
        out = num / den
        o_ref[0, :, cols] = jnp.where(h0, out[:dq], out[dq:])


def _attn_sample(proj, cache_k, cache_v, col_q, col_k, col_v, n_pairs, dec_batch, dq):
    nbuf = cache_k.shape[1]
    width = n_pairs * LANES
    groups, tail_start = _sample_key_rows(nbuf, dq)
    tail = nbuf - tail_start
    assert col_q % n_pairs == 0 and col_k % n_pairs == 0 and col_v % n_pairs == 0 and tail_start % tail == 0
    body = functools.partial(_attn_sample_body, dq=dq, nbuf=nbuf, n_pairs=n_pairs)
    nrows = dec_batch * dq

    def pspec(col):
        return pl.BlockSpec((nrows, width), lambda b: (0, col // n_pairs))

    sspec = pl.BlockSpec((1, groups, 8, width), lambda b: (b, 0, 0, 0))
    tspec = pl.BlockSpec((1, tail, width), lambda b: (b, tail_start // tail, 0))
    k4 = cache_k.reshape(dec_batch, nbuf // RES, RES, width)
    v4 = cache_v.reshape(dec_batch, nbuf // RES, RES, width)
    return pl.pallas_call(
        body,
        grid=(dec_batch,),
        in_specs=[pspec(col_q), pspec(col_k), pspec(col_v), sspec, tspec, sspec, tspec],
        out_specs=pl.BlockSpec((1, dq, width), lambda b: (b, 0, 0)),
        out_shape=jax.ShapeDtypeStruct((dec_batch, dq, width), F32),
        compiler_params=_cparams(("parallel",), 48),
        name="attn_sample",
    )(proj, proj, proj, k4, cache_k, v4, cache_v)


def _ssd_body(xbc_ref, z_ref, dt_ref, cprev_ref, h0_ref, cw_ref, cb_ref, dtb_ref, alog_ref, dskx_ref,
              nw_ref, e_ref, r_ref, y_ref, cnew_ref, hout_ref, h_sc, xf_sc, *, nvalid, d_ssm, n_heads):
    L = CHUNK
    c = pl.program_id(1)
    gw = d_ssm // SSM_GROUPS
    hpg = n_heads // SSM_GROUPS

    @pl.when(c == 0)
    def _():
        h_sc[...] = h0_ref[0]
        xf_sc[0:8] = cprev_ref[0]

    if nvalid < L:
        xf_sc[8:8 + L] = jnp.zeros((L, xf_sc.shape[1]), F32)
        xf_sc[8:8 + nvalid] = xbc_ref[0]
        zpad = jnp.zeros((L - nvalid, LANES), F32)
        dt_raw = jnp.concatenate([dt_ref[0], zpad], axis=0)
    else:
        xf_sc[8:8 + L] = xbc_ref[...]
        dt_raw = dt_ref[...]
    conv = cb_ref[...] + cw_ref[0:1, :] * xf_sc[5:5 + L]
    for w in range(1, CONV_W):
        conv = conv + cw_ref[w:w + 1, :] * xf_sc[5 + w:5 + w + L]
    cnew_ref[0] = xf_sc[nvalid:nvalid + 8]
    xf_sc[0:8] = xf_sc[L:L + 8]

    row = lax.broadcasted_iota(I32, (L, LANES), 0)
    lane = lax.broadcasted_iota(I32, (L, LANES), 1)
    if nvalid < L:
        rowc = lax.broadcasted_iota(I32, conv.shape, 0)
        conv = jnp.where(rowc < nvalid, conv, 0.0)
    act = conv * jax.nn.sigmoid(conv)
    xs = act[:, :d_ssm]
    bm = act[:, d_ssm:d_ssm + SSM_GROUPS * D_STATE].astype(BF16)
    cm = act[:, d_ssm + SSM_GROUPS * D_STATE:].astype(BF16)

    dt = jax.nn.softplus(dt_raw + dtb_ref[...])
    live = lane < n_heads
    if nvalid < L:
        live = live & (row < nvalid)
    dt = jnp.where(live, dt, 0.0)
    dta = dt * (-jnp.exp(alog_ref[...]))
    tril = row >= lane
    acum = jnp.dot(tril.astype(F32), dta, precision=lax.Precision.HIGHEST, preferred_element_type=F32)
    last = acum[L - 1:L, :]
    acum_t = acum.T
    eb = e_ref[...]
    stacked = jnp.concatenate([jnp.exp(last - acum) * dt, jnp.exp(acum), dt], axis=0)
    ex = _split_dot(stacked, eb)
    xw = (xs * ex[0:L]).astype(BF16)
    eac_x = ex[L:2 * L]
    xd = xs * ex[2 * L:3 * L]

    cd_col = jnp.broadcast_to(jnp.exp(acum_t[:, L - 1:L]), (LANES, LANES))
    cd_rows = _split_dot_left(r_ref[...], cd_col)

    h0l = lane < SSM_HD
    y_parts = []
    yoff_parts = []
    for g in range(SSM_GROUPS):
        bg = bm[:, g * D_STATE:(g + 1) * D_STATE]
        cg = cm[:, g * D_STATE:(g + 1) * D_STATE]
        cbg = _dot_nt(cg, bg)
        rows = slice(g * gw, (g + 1) * gw)
        hg = h_sc[rows, :]
        yoff_parts.append(_dot_nt(cg, hg.astype(BF16)))
        st = lax.dot_general(xw[:, rows], bg, (((0,), (0,)), ((), ())), preferred_element_type=F32)
        h_sc[rows, :] = cd_rows[rows, :] * hg + st
        for jp in range(hpg // 2):
            ms = []
            for hh in range(2):
                h = g * hpg + 2 * jp + hh
                diff = acum[:, h:h + 1] - acum_t[h:h + 1, :]
                ms.append(cbg * jnp.exp(jnp.where(tril, diff, NEG)))
            lhs = jnp.concatenate(ms, axis=1).astype(BF16)
            col = (g * hpg + 2 * jp) * SSM_HD
            xp = xd[:, col:col + LANES]
            rhs = jnp.concatenate([jnp.where(h0l, xp, 0.0), jnp.where(h0l, 0.0, xp)], axis=0).astype(BF16)
            y_parts.append(_dot(lhs, rhs))
    y = jnp.concatenate(y_parts, axis=1) + jnp.concatenate(yoff_parts, axis=1) * eac_x + dskx_ref[...] * xs
    if nvalid < L:
        y = y[:8]
        z = jnp.concatenate([z_ref[0], jnp.zeros((8 - nvalid, d_ssm), F32)], axis=0)
    else:
        z = z_ref[...]
    y = y * (z * jax.nn.sigmoid(z))
    outs = []
    for g in range(SSM_GROUPS):
        yg = y[:, g * gw:(g + 1) * gw]
        outs.append(yg * lax.rsqrt(jnp.mean(yg * yg, axis=-1, keepdims=True) + EPS))
    y = jnp.concatenate(outs, axis=1) * nw_ref[...]
    if nvalid < L:
        y_ref[0] = y[:nvalid].astype(y_ref.dtype)
    else:
        y_ref[...] = y.astype(y_ref.dtype)
    hout_ref[0] = h_sc[...]


def _split_dot_left(onehot_bf16, v):
    p1 = v.astype(BF16)
    r1 = v - p1.astype(F32)
    p2 = r1.astype(BF16)
    p3 = (r1 - p2.astype(F32)).astype(BF16)
    return _dot(onehot_bf16, p1) + _dot(onehot_bf16, p2) + _dot(onehot_bf16, p3)


def _ssd(xbc_src, z_src, dt_src, conv_prev8, h0, consts, *, batch, n_chunks, nvalid, xbc_spec, z_spec, dt_spec,
         d_ssm, n_heads):
    conv_dim = conv_prev8.shape[-1]
    hp = h0.shape[1]
    body = functools.partial(_ssd_body, nvalid=nvalid, d_ssm=d_ssm, n_heads=n_heads)
    cw, cb, dtb, alog, dskx, nw, e_mat, r_mat = consts

    def const(a):
        return pl.BlockSpec(a.shape, lambda b, c: (0,) * a.ndim)

    if nvalid < CHUNK:
        assert n_chunks == 1 and nvalid <= 8
        y_spec = pl.BlockSpec((1, nvalid, d_ssm), lambda b, c: (b, 0, 0))
        y_shape = jax.ShapeDtypeStruct((batch, nvalid, d_ssm), BF16)
    else:
        y_spec = pl.BlockSpec((CHUNK, d_ssm), lambda b, c: (b * n_chunks + c, 0))
        y_shape = jax.ShapeDtypeStruct((batch * n_chunks * CHUNK, d_ssm), BF16)
    return pl.pallas_call(
        body,
        grid=(batch, n_chunks),
        in_specs=[
            xbc_spec, z_spec, dt_spec,
            pl.BlockSpec((1, 8, conv_dim), lambda b, c: (b, 0, 0)),
            pl.BlockSpec((1, hp, D_STATE), lambda b, c: (b, 0, 0)),
            const(cw), const(cb), const(dtb), const(alog), const(dskx), const(nw), const(e_mat), const(r_mat),
        ],
        out_specs=[
            y_spec,
            pl.BlockSpec((1, 8, conv_dim), lambda b, c: (b, 0, 0)),
            pl.BlockSpec((1, hp, D_STATE), lambda b, c: (b, 0, 0)),
        ],
        out_shape=[
            y_shape,
            jax.ShapeDtypeStruct((batch, 8, conv_dim), F32),
            jax.ShapeDtypeStruct((batch, hp, D_STATE), F32),
        ],
        scratch_shapes=[pltpu.VMEM((hp, D_STATE), F32), pltpu.VMEM((CHUNK + 8, conv_dim), F32)],
        compiler_params=_cparams(("parallel", "arbitrary"), 48),
        name="ssd",
    )(xbc_src, z_src, dt_src, conv_prev8, h0, cw, cb, dtb, alog, dskx, nw, e_mat, r_mat)


def _outproj_body(x_ref, ya_ref, ys_ref, wa_ref, ws_ref, nfw_ref, wr_ref, br_ref,
                  xmid_ref, xpk_ref, idx_ref, gate_ref):
    tm, d = x_ref.shape
    pr = tm // OUTPROJ_PARTS
    parts = range(OUTPROJ_PARTS)
    rs = [pl.ds(i * pr, pr) for i in parts]
    mix = [_dot(ya_ref[r, :].astype(BF16), wa_ref[...]) + _dot(ys_ref[r, :], ws_ref[...]) for r in rs]
    xm = [x_ref[rs[i], :] + mix[i] for i in parts]
    for i in parts:
        xmid_ref[rs[i], :] = xm[i]
    xn = [xm[i] * lax.rsqrt(jnp.mean(xm[i] * xm[i], axis=-1, keepdims=True) + EPS) * nfw_ref[...] for i in parts]
    xh = [xn[i].astype(BF16) for i in parts]
    xhf = [xh[i].astype(F32) for i in parts]
    for i in parts:
        _store_packed(xpk_ref, pr, _pack_words(xh[i][:, :d // 2], xh[i][:, d // 2:]), row0=i * pr)
    xl = [(xn[i] - xhf[i]).astype(BF16) for i in parts]
    hl = [_dot(xh[i], wr_ref[...]) for i in parts]
    lh = [_dot(xl[i], wr_ref[:, :LANES]) for i in parts]
    logits = [hl[i][:, :LANES] + hl[i][:, LANES:] + lh[i] + br_ref[...] for i in parts]
    lane = lax.broadcasted_iota(I32, (pr, LANES), 1)
    idx_out = [jnp.zeros((pr, LANES), I32) for _ in parts]
    val_out = [jnp.full((pr, LANES), NEG, F32) for _ in parts]
    for k in range(TOP_K):
        mx = [jnp.max(logits[i], axis=1, keepdims=True) for i in parts]
        ix = [jnp.min(jnp.where(logits[i] == mx[i], lane, LANES), axis=1, keepdims=True) for i in parts]
        idx_out = [jnp.where(lane == k, ix[i], idx_out[i]) for i in parts]
        val_out = [jnp.where(lane == k, mx[i], val_out[i]) for i in parts]
        logits = [jnp.where(lane == ix[i], -jnp.inf, logits[i]) for i in parts]
    for i in parts:
        ev = jnp.exp(val_out[i] - jnp.max(val_out[i], axis=1, keepdims=True))
        gate_ref[rs[i], :] = ev / jnp.sum(ev, axis=1, keepdims=True)
        idx_ref[rs[i], :] = idx_out[i]


def _outproj(x_all, y_att, y_ssd, w_att, w_ssd, nfw, w_router, b_router):
    m, d = x_all.shape
    assert d == 2 * PACK_ROWS * LANES
    tm = _row_tile(m, TM_PROJ)

    def rows(width):
        return pl.BlockSpec((tm, width), lambda i: (i, 0))

    def const(a):
        return pl.BlockSpec(a.shape, lambda i: (0,) * a.ndim)

    return pl.pallas_call(
        _outproj_body,
        grid=(m // tm,),
        in_specs=[rows(d), rows(y_att.shape[1]), rows(y_ssd.shape[1]), const(w_att), const(w_ssd), const(nfw),
                  const(w_router), const(b_router)],
        out_specs=[rows(d), pl.BlockSpec((tm * PACK_ROWS, LANES), lambda i: (i, 0)), rows(LANES), rows(LANES)],
        out_shape=[jax.ShapeDtypeStruct((m, d), F32), jax.ShapeDtypeStruct((m * PACK_ROWS, LANES), U32),
                   jax.ShapeDtypeStruct((m, LANES), I32), jax.ShapeDtypeStruct((m, LANES), F32)],
        compiler_params=_cparams(("parallel",), 56),
        name="outproj",
    )(x_all, y_att, y_ssd, w_att, w_ssd, nfw, w_router, b_router)


def _route_body(idx_ref, cin_ref, rank_ref, cout_ref, carry):
    @pl.when(pl.program_id(0) == 0)
    def _():
        carry[...] = cin_ref[...]

    tr = idx_ref.shape[0]
    lane = lax.broadcasted_iota(I32, (tr, LANES), 1)
    tril = (lax.broadcasted_iota(I32, (tr, tr), 0) >= lax.broadcasted_iota(I32, (tr, tr), 1))
    tril = jnp.where(tril, 1.0, 0.0).astype(BF16)
    idx = idx_ref[...]
    cnt = carry[...]
    out = jnp.zeros((tr, LANES), I32)
    for k in range(TOP_K):
        hit = lane == idx[:, k:k + 1]
        pre = _dot(tril, jnp.where(hit, 1.0, 0.0).astype(BF16))
        rk = jnp.sum(jnp.where(hit, pre + cnt - 1.0, 0.0), axis=1, keepdims=True)
        out = jnp.where(lane == k, rk.astype(I32), out)
        cnt = cnt + pre[tr - 1:tr, :]
    carry[...] = cnt
    rank_ref[...] = out
    cout_ref[...] = cnt


def _route(top_idx, counts_in):
    m = top_idx.shape[0]
    tr = _row_tile(m, TM_ROUTE)
    return pl.pallas_call(
        _route_body,
        grid=(m // tr,),
        in_specs=[pl.BlockSpec((tr, LANES), lambda i: (i, 0)), pl.BlockSpec((1, LANES), lambda i: (0, 0))],
        out_specs=[pl.BlockSpec((tr, LANES), lambda i: (i, 0)), pl.BlockSpec((1, LANES), lambda i: (0, 0))],
        out_shape=[jax.ShapeDtypeStruct((m, LANES), I32), jax.ShapeDtypeStruct((1, LANES), F32)],
        scratch_shapes=[pltpu.VMEM((1, LANES), F32)],
        compiler_params=_cparams(("arbitrary",), 32),
        name="route",
    )(top_idx, counts_in)


def _dispatch_body(pos_ref, x_ref, xd_in, xd_hbm, sem):
    del xd_in
    i = pl.program_id(0)
    ts = x_ref.shape[0] // PACK_ROWS

    def row_copy(r, k):
        p = pos_ref[(i * ts + r) * TOP_K + k]
        src = x_ref.at[pl.ds(pl.multiple_of(r * PACK_ROWS, PACK_ROWS), PACK_ROWS), :]
        dst = xd_hbm.at[pl.ds(pl.multiple_of(p * PACK_ROWS, PACK_ROWS), PACK_ROWS), :]
        return pltpu.make_async_copy(src, dst, sem)

    def issue(r, c):
        for k in range(TOP_K):
            row_copy(r, k).start()
        return c
    lax.fori_loop(0, ts, issue, 0)

    def drain(r, c):
        for k in range(TOP_K):
            row_copy(r, k).wait()
        return c
    lax.fori_loop(0, ts, drain, 0)


def _dispatch(pos_flat, x_packed, x_disp):
    m = x_packed.shape[0] // PACK_ROWS
    ts = _row_tile(m, TM_DISPATCH)
    return pl.pallas_call(
        _dispatch_body,
        grid_spec=pltpu.PrefetchScalarGridSpec(
            num_scalar_prefetch=1,
            grid=(m // ts,),
            in_specs=[pl.BlockSpec((ts * PACK_ROWS, LANES), lambda i, pos: (i, 0)),
                      pl.BlockSpec(memory_space=pl.ANY)],
            out_specs=pl.BlockSpec(memory_space=pl.ANY),
            scratch_shapes=[pltpu.SemaphoreType.DMA(())],
        ),
        out_shape=jax.ShapeDtypeStruct(x_disp.shape, x_disp.dtype),
        input_output_aliases={2: 0},
        compiler_params=_cparams(("arbitrary",), 32),
        name="dispatch",
    )(pos_flat, x_packed, x_disp)


def _tile_changed(te_ref, i):
    return (i == 0) | (te_ref[i] != te_ref[jnp.maximum(i - 1, 0)])


def _gmm1_body(te_ref, nu_ref, x_ref, wg_ref, wu_ref, bg_ref, bu_ref, h_ref, wgb, wub):
    i = pl.program_id(1)

    @pl.when(_tile_changed(te_ref, i))
    def _():
        wgb[...] = wg_ref[0].astype(BF16)
        wub[...] = wu_ref[0].astype(BF16)

    @pl.when(i < nu_ref[0])
    def _():
        tm = h_ref.shape[0]
        x = _load_packed(lambda s: x_ref[pl.ds(s, tm, stride=PACK_ROWS), :], BF16)
        gt = _dot(x, wgb[...]) + bg_ref[0]
        up = _dot(x, wub[...]) + bu_ref[0]
        gt = jnp.minimum(gt, SWIGLU_LIMIT)
        up = jnp.clip(up, -SWIGLU_LIMIT, SWIGLU_LIMIT)
        act = (up + 1.0) * gt * jax.nn.sigmoid(SWIGLU_ALPHA * gt)
        h_ref[...] = act.astype(h_ref.dtype)

    @pl.when(i >= nu_ref[0])
    def _():
        h_ref[...] = jnp.zeros(h_ref.shape, h_ref.dtype)


def _gmm2_body(te_ref, nu_ref, h_ref, wd_ref, bd_ref, y_ref, wdb):
    i = pl.program_id(0)

    @pl.when(_tile_changed(te_ref, i))
    def _():
        wdb[...] = wd_ref[0].astype(BF16)

    @pl.when(i < nu_ref[0])
    def _():
        tm, half = h_ref.shape[0], wdb.shape[1] // 2
        h = h_ref[...]
        lo = _dot(h, wdb[:, :half]) + bd_ref[0][:, :half]
        hi = _dot(h, wdb[:, half:]) + bd_ref[0][:, half:]
        _store_packed(y_ref, tm, _pack_words(lo.astype(BF16), hi.astype(BF16)))

    @pl.when(i >= nu_ref[0])
    def _():
        y_ref[...] = jnp.zeros(y_ref.shape, y_ref.dtype)


def _used_tile(i, nu):
    return jnp.minimum(i, nu[0] - 1)


def _gmm1(tile_exp, n_used, x_disp, w_gate, w_up, b_gate, b_up):
    cap = x_disp.shape[0] // PACK_ROWS
    d, dff = w_gate.shape[1], w_gate.shape[2]
    tm, tn = TM_GMM, TN_GMM
    x_spec = pl.BlockSpec((tm * PACK_ROWS, LANES), lambda j, i, te, nu: (_used_tile(i, nu), 0))
    w_spec = pl.BlockSpec((1, d, tn), lambda j, i, te, nu: (te[_used_tile(i, nu)], 0, j))
    b_spec = pl.BlockSpec((1, 1, tn), lambda j, i, te, nu: (te[_used_tile(i, nu)], 0, j))
    return pl.pallas_call(
        _gmm1_body,
        grid_spec=pltpu.PrefetchScalarGridSpec(
            num_scalar_prefetch=2,
            grid=(dff // tn, cap // tm),
            in_specs=[x_spec, w_spec, w_spec, b_spec, b_spec],
            out_specs=pl.BlockSpec((tm, tn), lambda j, i, te, nu: (i, j)),
            scratch_shapes=[pltpu.VMEM((d, tn), BF16), pltpu.VMEM((d, tn), BF16)],
        ),
        out_shape=jax.ShapeDtypeStruct((cap, dff), BF16),
        compiler_params=_cparams(("arbitrary", "arbitrary"), 60),
        name="gmm1",
    )(tile_exp, n_used, x_disp, w_gate, w_up, b_gate, b_up)


def _gmm2(tile_exp, n_used, h_disp, w_down, b_down):
    cap, dff = h_disp.shape
    d = w_down.shape[2]
    assert d == 2 * PACK_ROWS * LANES
    tm = TM_GMM
    return pl.pallas_call(
        _gmm2_body,
        grid_spec=pltpu.PrefetchScalarGridSpec(
            num_scalar_prefetch=2,
            grid=(cap // tm,),
            in_specs=[
                pl.BlockSpec((tm, dff), lambda i, te, nu: (_used_tile(i, nu), 0)),
                pl.BlockSpec((1, dff, d), lambda i, te, nu: (te[_used_tile(i, nu)], 0, 0)),
                pl.BlockSpec((1, 1, d), lambda i, te, nu: (te[_used_tile(i, nu)], 0, 0)),
            ],
            out_specs=pl.BlockSpec((tm * PACK_ROWS, LANES), lambda i, te, nu: (i, 0)),
            scratch_shapes=[pltpu.VMEM((dff, d), BF16)],
        ),
        out_shape=jax.ShapeDtypeStruct((cap * PACK_ROWS, LANES), U32),
        compiler_params=_cparams(("arbitrary",), 60),
        name="gmm2",
    )(tile_exp, n_used, h_disp, w_down, b_down)


def _combine_body(pos_ref, y_hbm, xmid_ref, gate_ref, fw_ref, o_ref, buf, sem):
    i = pl.program_id(0)
    n = pl.num_programs(0)
    tc = TOK_COMBINE

    def row_copy(step, slot, r, k):
        p = pos_ref[(step * tc + r) * TOP_K + k]
        src = y_hbm.at[pl.ds(pl.multiple_of(p * PACK_ROWS, PACK_ROWS), PACK_ROWS), :]
        dst = buf.at[slot, k, pl.ds(pl.multiple_of(r * PACK_ROWS, PACK_ROWS), PACK_ROWS), :]
        return pltpu.make_async_copy(src, dst, sem.at[slot])

    def issue(step, slot):
        def body(r, c):
            for k in range(TOP_K):
                row_copy(step, slot, r, k).start()
            return c
        lax.fori_loop(0, tc, body, 0)

    @pl.when(i == 0)
    def _():
        issue(0, 0)

    @pl.when(i + 1 < n)
    def _():
        issue(i + 1, (i + 1) % 2)

    slot = i % 2

    def wait_body(r, c):
        for k in range(TOP_K):
            row_copy(i, slot, r, k).wait()
        return c
    lax.fori_loop(0, tc, wait_body, 0)

    gates = gate_ref[...]
    acc = xmid_ref[...]
    for k in range(TOP_K):
        y_k = _load_packed(lambda s, k=k: buf[slot, k, pl.ds(s, tc, stride=PACK_ROWS), :], F32)
        acc = acc + gates[:, k:k + 1] * y_k
    o_ref[...] = acc * lax.rsqrt(jnp.mean(acc * acc, axis=-1, keepdims=True) + EPS) * fw_ref[...]


def _combine(pos_flat, y_disp, x_mid, gates, final_w):
    m, d = x_mid.shape
    tc = TOK_COMBINE
    return pl.pallas_call(
        _combine_body,
        grid_spec=pltpu.PrefetchScalarGridSpec(
            num_scalar_prefetch=1,
            grid=(m // tc,),
            in_specs=[
                pl.BlockSpec(memory_space=pl.ANY),
                pl.BlockSpec((tc, d), lambda i, pos: (i, 0)),
                pl.BlockSpec((tc, LANES), lambda i, pos: (i, 0)),
                pl.BlockSpec((1, d), lambda i, pos: (0, 0)),
            ],
            out_specs=pl.BlockSpec((tc, d), lambda i, pos: (i, 0)),
            scratch_shapes=[pltpu.VMEM((2, TOP_K, tc * PACK_ROWS, LANES), U32), pltpu.SemaphoreType.DMA((2,))],
        ),
        out_shape=jax.ShapeDtypeStruct((m, d), F32),
        compiler_params=_cparams(("arbitrary",), 32),
        name="combine",
    )(pos_flat, y_disp, x_mid, gates, final_w)


def _tile_layout(counts, n_assign, n_exp):
    tiles = (counts + TM_GMM - 1) // TM_GMM
    tile_end = jnp.cumsum(tiles)
    row_start = tile_end * TM_GMM - counts
    n_tiles = -(-n_assign // TM_GMM) + n_exp
    n_used = tile_end[-1]
    tile_ids = jnp.minimum(jnp.arange(n_tiles, dtype=I32), n_used - 1)
    tile_exp = jnp.minimum(jnp.sum((tile_end[None, :] <= tile_ids[:, None]).astype(I32), axis=1), n_exp - 1)
    return row_start.astype(I32), tile_exp.astype(I32), n_used.reshape(1).astype(I32), n_tiles * TM_GMM


def _positions(top_idx, rank, row_start, n_exp):
    idx = top_idx[:, :TOP_K]
    onehot = idx[:, :, None] == jnp.arange(n_exp, dtype=I32)[None, None, :]
    base = jnp.sum(jnp.where(onehot, row_start[None, None, :], 0), axis=-1)
    return (base + rank[:, :TOP_K]).reshape(-1).astype(I32)


def kernel(x_prompt, x_sample, cache_attn_k, cache_attn_v, state_conv, state_ssm, norm_mix_w, w_in, conv_w, conv_b, dt_bias, a_log, d_skip, ssd_norm_w, w_out, norm_ffn_w, w_router, b_router, w_gate, b_gate, w_up, b_up, w_down, b_down, norm_final_w):
    depth = norm_mix_w.shape[0]
    assert depth == 1, "single-layer step"
    bp, seq, d = x_prompt.shape
    bs, dq, _ = x_sample.shape
    n_att_heads = cache_attn_k.shape[3]
    d_att = n_att_heads * HD_ATT
    n_pairs = d_att // LANES
    n_heads = a_log.shape[1]
    d_ssm = n_heads * SSM_HD
    conv_dim = conv_w.shape[2]
    n_exp = w_router.shape[2]
    n_prompt = bp * seq
    n_sample = bs * dq
    assert n_sample == CHUNK and seq % SUPER == 0 and n_heads <= LANES

    wl = w_in[0]
    o_q, o_k, o_v, o_z, o_x, o_dt = 0, d_att, 2 * d_att, 3 * d_att, 3 * d_att + d_ssm, 3 * d_att + d_ssm + conv_dim
    w_main = jnp.concatenate([wl[:, o_x:o_dt], wl[:, o_q:o_z], wl[:, o_z:o_x]], axis=1).astype(BF16)
    w_dt = jnp.pad(wl[:, o_dt:], ((0, 0), (0, LANES - n_heads))).astype(BF16)
    col_q = conv_dim // LANES
    col_k = col_q + n_pairs
    col_v = col_k + n_pairs
    w_att = w_out[0, :d_att].astype(BF16)
    w_ssd = w_out[0, d_att:].astype(BF16)
    w_rp = jnp.pad(w_router[0], ((0, 0), (0, LANES - n_exp)))
    w_rh = w_rp.astype(BF16)
    w_r = jnp.concatenate([w_rh, (w_rp - w_rh.astype(F32)).astype(BF16)], axis=1)
    b_r = jnp.pad(b_router[0], (0, LANES - n_exp), constant_values=NEG).reshape(1, LANES)
    pad_h = (0, LANES - n_heads)
    head_of_col = jnp.arange(d_ssm, dtype=I32) // SSM_HD
    e_mat = (jnp.arange(LANES, dtype=I32)[:, None] == head_of_col[None, :]).astype(BF16)
    r_mat = (head_of_col[:, None] == jnp.arange(LANES, dtype=I32)[None, :]).astype(BF16)
    ssd_consts = (
        conv_w[0], conv_b[0].reshape(1, conv_dim),
        jnp.pad(dt_bias[0], pad_h).reshape(1, LANES), jnp.pad(a_log[0], pad_h).reshape(1, LANES),
        jnp.repeat(d_skip[0], SSM_HD).reshape(1, d_ssm), ssd_norm_w[0].reshape(1, d_ssm), e_mat, r_mat,
    )

    xp = x_prompt.reshape(n_prompt, d)
    xs = x_sample.reshape(n_sample, d)
    nmw = norm_mix_w[0].reshape(1, d)
    proj_p, dt_p = _inproj(xp, nmw, w_main, w_dt)
    proj_s, dt_s = _inproj(xs, nmw, w_main, w_dt)
    n_main = proj_p.shape[1]
    z_col = (conv_dim + 3 * d_att) // d_ssm
    assert z_col * d_ssm == conv_dim + 3 * d_att

    win_p = min(BRANCHES[-1][0], seq)
    y_att_p, k_p, v_p = _attn_prompt(proj_p, bp, seq, win_p, col_q, col_k, col_v, n_pairs)
    ck = cache_attn_k[0].reshape(bs, -1, d_att)
    cv = cache_attn_v[0].reshape(bs, -1, d_att)
    y_att_s = _attn_sample(proj_s, ck, cv, col_q, col_k, col_v, n_pairs, bs, dq)

    n_chunks = seq // CHUNK
    hp = n_heads * SSM_HD
    y_ssd_p, cnew_p, h_p = _ssd(
        proj_p, proj_p, dt_p, jnp.zeros((bp, 8, conv_dim), F32), jnp.zeros((bp, hp, D_STATE), F32), ssd_consts,
        batch=bp, n_chunks=n_chunks, nvalid=CHUNK,
        xbc_spec=pl.BlockSpec((CHUNK, conv_dim), lambda b, c: (b * n_chunks + c, 0)),
        z_spec=pl.BlockSpec((CHUNK, d_ssm), lambda b, c: (b * n_chunks + c, z_col)),
        dt_spec=pl.BlockSpec((CHUNK, LANES), lambda b, c: (b * n_chunks + c, 0)),
        d_ssm=d_ssm, n_heads=n_heads)

    proj_s3 = proj_s.reshape(bs, dq, n_main)
    cprev_s = jnp.pad(state_conv[0], ((0, 0), (8 - (CONV_W - 1), 0), (0, 0)))
    y_ssd_s, cnew_s, h_s = _ssd(
        proj_s3, proj_s3, dt_s.reshape(bs, dq, LANES), cprev_s, state_ssm[0].reshape(bs, hp, D_STATE), ssd_consts,
        batch=bs, n_chunks=1, nvalid=dq,
        xbc_spec=pl.BlockSpec((1, dq, conv_dim), lambda b, c: (b, 0, 0)),
        z_spec=pl.BlockSpec((1, dq, d_ssm), lambda b, c: (b, 0, z_col)),
        dt_spec=pl.BlockSpec((1, dq, LANES), lambda b, c: (b, 0, 0)),
        d_ssm=d_ssm, n_heads=n_heads)

    nfw = norm_ffn_w[0].reshape(1, d)
    xmid_p, xpk_p, idx_p, gates_p = _outproj(xp, y_att_p, y_ssd_p, w_att, w_ssd, nfw, w_r, b_r)
    xmid_s, xpk_s, idx_s, gates_s = _outproj(xs, y_att_s.reshape(n_sample, d_att),
                                             y_ssd_s.reshape(n_sample, d_ssm), w_att, w_ssd, nfw, w_r, b_r)

    rank_p, cnt_p = _route(idx_p, jnp.zeros((1, LANES), F32))
    rank_s, cnt_all = _route(idx_s, cnt_p)
    counts = cnt_all[0, :n_exp].astype(I32)
    n_assign = (n_prompt + n_sample) * TOP_K
    row_start, tile_exp, n_used, cap = _tile_layout(counts, n_assign, n_exp)
    pos_p = _positions(idx_p, rank_p, row_start, n_exp)
    pos_s = _positions(idx_s, rank_s, row_start, n_exp)
    x_disp = jnp.zeros((cap * PACK_ROWS, LANES), U32)
    x_disp = _dispatch(pos_p, xpk_p, x_disp)
    x_disp = _dispatch(pos_s, xpk_s, x_disp)
    h_disp = _gmm1(tile_exp, n_used, x_disp, w_gate[0], w_up[0],
                   b_gate[0].reshape(n_exp, 1, -1), b_up[0].reshape(n_exp, 1, -1))
    y_disp = _gmm2(tile_exp, n_used, h_disp, w_down[0], b_down[0].reshape(n_exp, 1, -1))
    fw = norm_final_w.reshape(1, d)
    y_p = _combine(pos_p, y_disp, xmid_p, gates_p, fw)
    y_s = _combine(pos_s, y_disp, xmid_s, gates_s, fw)

    k_p = k_p.reshape(1, bp, win_p, n_att_heads, HD_ATT)
    v_p = v_p.reshape(1, bp, win_p, n_att_heads, HD_ATT)
    k_s = proj_s3[:, :, col_k * LANES:col_v * LANES].reshape(1, bs, dq, n_att_heads, HD_ATT)
    v_s = proj_s3[:, :, col_v * LANES:col_v * LANES + d_att].reshape(1, bs, dq, n_att_heads, HD_ATT)
    return (
        y_p.reshape(bp, seq, d), y_s.reshape(bs, dq, d), k_p, v_p,
        cnew_p[:, 8 - (CONV_W - 1):][None], h_p.reshape(1, bp, n_heads, SSM_HD, D_STATE),
        k_s, v_s,
        cnew_s[:, 8 - (CONV_W - 1):][None], h_s.reshape(1, bs, n_heads, SSM_HD, D_STATE),
    )
```

```python
import functools

import jax
import jax.numpy as jnp
from jax import lax
from jax.experimental import pallas as pl
from jax.experimental.pallas import tpu as pltpu

F32 = jnp.float32
BF16 = jnp.bfloat16
I32 = jnp.int32
U32 = jnp.uint32

EPS = 1e-5
NEG = -1e30
LANES = 128
HD_ATT = 64
BRANCHES = ((128, 1), (512, 4), (2048, 16))
RES = 16
ATT_BLK = 128
ATT_BLOCKS_PER_STEP = 2
SUPER = RES * ATT_BLK
SSM_HD = 64
SSM_GROUPS = 4
D_STATE = 128
CONV_W = 4
CHUNK = 128
TOP_K = 4
SWIGLU_LIMIT = 7.0
SWIGLU_ALPHA = 1.702

PACK_ROWS = 8

TM_PROJ = 512
OUTPROJ_PARTS = 2
TN_PROJ = 2048
TM_ROUTE = 512
TM_DISPATCH = 512
TM_GMM = 512
TN_GMM = 1024
TOK_COMBINE = 128


def _row_tile(m, cap):
    t = cap - cap % LANES
    while m % t:
        t -= LANES
    return t


def _cparams(sem, vmem_mb):
    return pltpu.CompilerParams(dimension_semantics=sem, vmem_limit_bytes=vmem_mb << 20)


def _dot(a, b):
    return jnp.dot(a, b, preferred_element_type=F32)


def _dot_nt(a, b):
    return lax.dot_general(a, b, (((1,), (1,)), ((), ())), preferred_element_type=F32)


def _split_dot(v, onehot_bf16):
    hi = v.astype(BF16)
    lo = (v - hi.astype(F32)).astype(BF16)
    return _dot(hi, onehot_bf16) + _dot(lo, onehot_bf16)


def _pack_words(lo, hi):
    lo_bits = lax.bitcast_convert_type(lo.astype(F32), U32)
    hi_bits = lax.bitcast_convert_type(hi.astype(F32), U32)
    return (hi_bits & jnp.uint32(0xFFFF0000)) | (lo_bits >> 16)


def _store_packed(ref, rows, words, row0=0):
    for s in range(PACK_ROWS):
        ref[pl.ds(row0 * PACK_ROWS + s, rows, stride=PACK_ROWS), :] = words[:, s * LANES:(s + 1) * LANES]


def _load_packed(load_slab, dtype):
    lo, hi = [], []
    for s in range(PACK_ROWS):
        w = load_slab(s)
        lo.append(lax.bitcast_convert_type(w << 16, F32).astype(dtype))
        hi.append(lax.bitcast_convert_type(w & jnp.uint32(0xFFFF0000), F32).astype(dtype))
    return jnp.concatenate(lo + hi, axis=1)


def _inproj_body(x_ref, nw_ref, w_ref, wdt_ref, o_ref, dt_ref, h_sc):
    @pl.when(pl.program_id(1) == 0)
    def _():
        x = x_ref[...]
        ms = jnp.mean(x * x, axis=-1, keepdims=True)
        h = (x * lax.rsqrt(ms + EPS) * nw_ref[...]).astype(BF16)
        h_sc[...] = h
        dt_ref[...] = _dot(h, wdt_ref[...])

    o_ref[...] = _dot(h_sc[...], w_ref[...])


def _inproj(x_all, norm_w, w_main, w_dt):
    m, d = x_all.shape
    n = w_main.shape[1]
    tm, tn = _row_tile(m, TM_PROJ), TN_PROJ
    return pl.pallas_call(
        _inproj_body,
        grid=(m // tm, n // tn),
        in_specs=[
            pl.BlockSpec((tm, d), lambda i, j: (i, 0)),
            pl.BlockSpec((1, d), lambda i, j: (0, 0)),
            pl.BlockSpec((d, tn), lambda i, j: (0, j)),
            pl.BlockSpec((d, LANES), lambda i, j: (0, 0)),
        ],
        out_specs=[
            pl.BlockSpec((tm, tn), lambda i, j: (i, j)),
            pl.BlockSpec((tm, LANES), lambda i, j: (i, 0)),
        ],
        out_shape=[jax.ShapeDtypeStruct((m, n), F32), jax.ShapeDtypeStruct((m, LANES), F32)],
        scratch_shapes=[pltpu.VMEM((tm, d), BF16)],
        compiler_params=_cparams(("parallel", "arbitrary"), 48),
        name="inproj",
    )(x_all, norm_w, w_main, w_dt)


def _attn_blocks(args):
    lane = lax.broadcasted_iota(I32, (ATT_BLK, LANES), 1)
    h0 = lane < HD_ATT
    hk = lax.broadcasted_iota(I32, (2 * ATT_BLK, LANES), 1) < HD_ATT
    n = len(args)
    qs = [jnp.concatenate([jnp.where(h0, a[0], 0.0), jnp.where(h0, 0.0, a[0])], axis=0).astype(BF16) for a in args]
    kk = [jnp.concatenate([a[1], a[2]], axis=0).astype(BF16) for a in args]
    s = [_dot_nt(qs[i], kk[i]) for i in range(n)]
    s = [s[i] + jnp.concatenate([args[i][8], args[i][8]], axis=0) for i in range(n)]
    mn = [jnp.maximum(args[i][5], jnp.max(s[i], axis=1, keepdims=True)) for i in range(n)]
    p = [jnp.exp(s[i] - mn[i]).astype(BF16) for i in range(n)]
    vv = [jnp.concatenate([a[3], a[4]], axis=0) for a in args]
    v0 = [jnp.where(hk, vv[i], 1.0).astype(BF16) for i in range(n)]
    v1 = [jnp.where(hk, 1.0, vv[i]).astype(BF16) for i in range(n)]
    o0 = [_dot(p[i][:ATT_BLK], v0[i]) for i in range(n)]
    o1 = [_dot(p[i][ATT_BLK:], v1[i]) for i in range(n)]
    al = [jnp.exp(args[i][5] - mn[i]) for i in range(n)]
    a0 = [jnp.broadcast_to(al[i][:ATT_BLK], (ATT_BLK, LANES)) for i in range(n)]
    a1 = [jnp.broadcast_to(al[i][ATT_BLK:], (ATT_BLK, LANES)) for i in range(n)]
    ln = [jnp.where(h0, a1[i], a0[i]) * args[i][6] + jnp.where(h0, o1[i], o0[i]) for i in range(n)]
    an = [jnp.where(h0, a0[i], a1[i]) * args[i][7] + jnp.where(h0, o0[i], o1[i]) for i in range(n)]
    return [(mn[i], ln[i], an[i]) for i in range(n)]


def _attn_prompt_body(q_ref, k_ref, v_ref, o_ref, kw_ref, vw_ref, qr, kr, vr, m_sc, l_sc, acc_sc, bias_sc, *, seq):
    ni = seq // RES
    scale = HD_ATT ** -0.5
    win = kw_ref.shape[0]
    kw_ref[...] = k_ref[seq - win:, :]
    vw_ref[...] = v_ref[seq - win:, :]
    for j in range(RES):
        qr[j] = q_ref[pl.ds(j, ni, stride=RES), :] * scale
        kr[j] = k_ref[pl.ds(j, ni, stride=RES), :]
        vr[j] = v_ref[pl.ds(j, ni, stride=RES), :]

    rq = lax.broadcasted_iota(I32, (ATT_BLK, 2 * ATT_BLK), 0)
    cc = lax.broadcasted_iota(I32, (ATT_BLK, 2 * ATT_BLK), 1)
    is_cur = cc >= ATT_BLK
    rk = jnp.where(is_cur, cc - ATT_BLK, cc)
    for br, (_, dil) in enumerate(BRANCHES):
        nseg = RES // dil
        seg = ATT_BLK // nseg
        oq = nseg * (rq % seg) + rq // seg
        ok = nseg * (rk % seg) + rk // seg
        allowed = (is_cur & (ok <= oq)) | (jnp.logical_not(is_cur) & (ok >= oq))
        bias_sc[2 * br + 1] = jnp.where(allowed, 0.0, NEG)
        bias_sc[2 * br] = jnp.where(allowed & is_cur, 0.0, NEG)

    def run_block(br, rows, qoff, poff, has_prev):
        seg = ATT_BLK // len(rows)
        q = jnp.concatenate([qr[r, pl.ds(qoff, seg), :] for r in rows], axis=0)
        kc = jnp.concatenate([kr[r, pl.ds(qoff, seg), :] for r in rows], axis=0)
        vc = jnp.concatenate([vr[r, pl.ds(qoff, seg), :] for r in rows], axis=0)
        kp = jnp.concatenate([kr[r, pl.ds(poff, seg), :] for r in rows], axis=0)
        vp = jnp.concatenate([vr[r, pl.ds(poff, seg), :] for r in rows], axis=0)
        return q, kp, kc, vp, vc

    def super_block(u, carry):
        base = pl.multiple_of(u * ATT_BLK, ATT_BLK)
        m_sc[...] = jnp.full(m_sc.shape, NEG, F32)
        l_sc[...] = jnp.zeros(l_sc.shape, F32)
        acc_sc[...] = jnp.zeros(acc_sc.shape, F32)

        def branch_steps(br, specs):
            loaded = []
            for rows, loc, seg in specs:
                qoff = pl.multiple_of(base + loc, 8)
                poff = pl.multiple_of(jnp.maximum(base + loc - seg, 0), 8)
                has_prev = (base + loc >= seg).astype(I32)
                q, kp, kc, vp, vc = run_block(br, rows, qoff, poff, has_prev)
                bias = bias_sc[2 * br + has_prev]
                sl = pl.ds(pl.multiple_of(loc, 8), seg)
                mo = jnp.concatenate([m_sc[h, r, sl, :] for h in range(2) for r in rows], axis=0)
                lo = jnp.concatenate([l_sc[r, sl, :] for r in rows], axis=0)
                ao = jnp.concatenate([acc_sc[r, sl, :] for r in rows], axis=0)
                loaded.append((q, kp, kc, vp, vc, mo, lo, ao, bias))
            results = _attn_blocks(loaded)
            for (rows, loc, seg), (mn, ln, an) in zip(specs, results):
                sl = pl.ds(pl.multiple_of(loc, 8), seg)
                for h in range(2):
                    for si, r in enumerate(rows):
                        lo_r = h * ATT_BLK + si * seg
                        m_sc[h, r, sl, :] = mn[lo_r:lo_r + seg]
                for si, r in enumerate(rows):
                    acc_sc[r, sl, :] = an[si * seg:(si + 1) * seg]
                    l_sc[r, sl, :] = ln[si * seg:(si + 1) * seg]

        nb = ATT_BLOCKS_PER_STEP

        def b1(n, c):
            branch_steps(0, [(list(range(RES)), (n * nb + uu) * 8, 8) for uu in range(nb)])
            return c
        lax.fori_loop(0, RES // nb, b1, 0)

        def b4(t, c):
            specs = []
            for uu in range(nb):
                tt = t * nb + uu
                specs.append(([tt // 4 + 4 * m for m in range(4)], (tt % 4) * 32, 32))
            branch_steps(1, specs)
            return c
        lax.fori_loop(0, RES // nb, b4, 0)

        def b16(j, c):
            branch_steps(2, [([j * nb + uu], 0, ATT_BLK) for uu in range(nb)])
            return c
        lax.fori_loop(0, RES // nb, b16, 0)

        for j in range(RES):
            den = pltpu.roll(l_sc[j], HD_ATT, axis=1)
            o_ref[pl.ds(u * SUPER + j, ATT_BLK, stride=RES), :] = acc_sc[j] / den
        return carry

    lax.fori_loop(0, seq // SUPER, super_block, 0)


def _attn_prompt(proj, batch, seq, win, col_q, col_k, col_v, n_pairs):
    ni = seq // RES
    body = functools.partial(_attn_prompt_body, seq=seq)

    def spec(col):
        return pl.BlockSpec((seq, LANES), lambda b, p: (b, col + p))

    wspec = pl.BlockSpec((win, LANES), lambda b, p: (b, p))
    wshape = jax.ShapeDtypeStruct((batch * win, n_pairs * LANES), F32)
    return pl.pallas_call(
        body,
        grid=(batch, n_pairs),
        in_specs=[spec(col_q), spec(col_k), spec(col_v)],
        out_specs=[pl.BlockSpec((seq, LANES), lambda b, p: (b, p)), wspec, wspec],
        out_shape=[jax.ShapeDtypeStruct((batch * seq, n_pairs * LANES), F32), wshape, wshape],
        scratch_shapes=[
            pltpu.VMEM((RES, ni, LANES), F32),
            pltpu.VMEM((RES, ni, LANES), F32),
            pltpu.VMEM((RES, ni, LANES), F32),
            pltpu.VMEM((2, RES, ATT_BLK, 1), F32),
            pltpu.VMEM((RES, ATT_BLK, LANES), F32),
            pltpu.VMEM((RES, ATT_BLK, LANES), F32),
            pltpu.VMEM((2 * len(BRANCHES), ATT_BLK, 2 * ATT_BLK), F32),
        ],
        compiler_params=_cparams(("parallel", "parallel"), 60),
        name="attn_prompt",
    )(proj, proj, proj)


def _sample_key_rows(nbuf, dq):
    tail = max(w for w, dil in BRANCHES if dil < RES)
    assert nbuf % RES == 0 and tail % RES == 0 and tail <= nbuf and dq <= RES
    return (nbuf - tail) // RES, nbuf - tail


def _attn_sample_body(q_ref, kn_ref, vn_ref, kc_ref, vc_ref, o_ref, *, dq, nbuf, n_pairs):
    b = pl.program_id(0)
    assert 8 % dq == 0
    groups, tail_start = _sample_key_rows(nbuf, dq)
    n_str = groups * 8
    n_tail = nbuf - tail_start

    def strided(ref, cols):
        return ref[0, 0:tail_start, cols].reshape(groups, RES, LANES)[:, :8, :].reshape(n_str, LANES)
    lane = lax.broadcasted_iota(I32, (dq, LANES), 1)
    h0 = lane < HD_ATT
    zpad = jnp.zeros((8 - dq, LANES), F32)

    def counts(delta):
        cnt = jnp.zeros(delta.shape, F32)
        for win, dil in BRANCHES:
            hit = (delta >= 0) & (delta <= win) & (delta % dil == 0)
            cnt = cnt + hit.astype(F32)
        return cnt

    def key_counts(n, pos_of_col):
        col = lax.broadcasted_iota(I32, (2 * dq, n), 1)
        qi = lax.broadcasted_iota(I32, (2 * dq, n), 0) % dq
        return counts(nbuf + qi - pos_of_col(col))

    cnt_s = key_counts(n_str, lambda c: RES * (c // 8) + c % 8)
    cnt_t = key_counts(n_tail, lambda c: tail_start + c)
    qi_n = lax.broadcasted_iota(I32, (2 * dq, 8), 0) % dq
    col_n = lax.broadcasted_iota(I32, (2 * dq, 8), 1)
    cnt_n = jnp.where(col_n < dq, counts(qi_n - col_n), 0.0)

    row8 = pl.multiple_of((b * dq) // 8 * 8, 8)
    sub = (b * dq) % 8 // dq

    def new_rows(ref, cols):
        x8 = ref[pl.ds(row8, 8), cols]
        out = x8[0:dq]
        for j in range(1, 8 // dq):
            out = jnp.where(sub == j, x8[j * dq:(j + 1) * dq], out)
        return out

    for p in range(n_pairs):
        cols = slice(p * LANES, (p + 1) * LANES)
        q = new_rows(q_ref, cols) * (HD_ATT ** -0.5)
        qs = jnp.concatenate([jnp.where(h0, q, 0.0), jnp.where(h0, 0.0, q)], axis=0).astype(BF16)
        kn = jnp.concatenate([new_rows(kn_ref, cols), zpad], axis=0).astype(BF16)
        vn = jnp.concatenate([new_rows(vn_ref, cols), zpad], axis=0).astype(BF16)
        pieces = [
            (strided(kc_ref, cols), strided(vc_ref, cols), cnt_s),
            (kc_ref[0, tail_start:, cols], vc_ref[0, tail_start:, cols], cnt_t),
            (kn, vn, cnt_n),
        ]
        s = [jnp.where(cnt > 0, _dot_nt(qs, k.astype(BF16)), NEG) for k, _, cnt in pieces]
        m = functools.reduce(jnp.maximum, [jnp.max(si, axis=1, keepdims=True) for si in s])
        e = [cnt * jnp.exp(si - m) for si, (_, _, cnt) in zip(s, pieces)]
        den = sum(jnp.sum(ei, axis=1, keepdims=True) for ei in e)
        num = sum(_dot(ei.astype(BF16), v.astype(BF16)) for ei, (_, v, _) in zip(e, pieces))
        out = num / den
        o_ref[0, :, cols] = jnp.where(h0, out[:dq], out[dq:])


def _attn_sample(proj, cache_k, cache_v, col_q, col_k, col_v, n_pairs, dec_batch, dq):
    nbuf = cache_k.shape[1]
    width = n_pairs * LANES
    assert col_q % n_pairs == 0 and col_k % n_pairs == 0 and col_v % n_pairs == 0
    body = functools.partial(_attn_sample_body, dq=dq, nbuf=nbuf, n_pairs=n_pairs)
    nrows = dec_batch * dq

    def pspec(col):
        return pl.BlockSpec((nrows, width), lambda b: (0, col // n_pairs))

    cspec = pl.BlockSpec((1, nbuf, width), lambda b: (b, 0, 0))
    return pl.pallas_call(
        body,
        grid=(dec_batch,),
        in_specs=[pspec(col_q), pspec(col_k), pspec(col_v), cspec, cspec],
        out_specs=pl.BlockSpec((1, dq, width), lambda b: (b, 0, 0)),
        out_shape=jax.ShapeDtypeStruct((dec_batch, dq, width), F32),
        compiler_params=_cparams(("parallel",), 48),
        name="attn_sample",
    )(proj, proj, proj, cache_k, cache_v)


def _ssd_body(xbc_ref, z_ref, dt_ref, cprev_ref, h0_ref, cw_ref, cb_ref, dtb_ref, alog_ref, dskx_ref,
              nw_ref, e_ref, r_ref, y_ref, cnew_ref, hout_ref, h_sc, xf_sc, *, nvalid, d_ssm, n_heads):
    L = CHUNK
    c = pl.program_id(1)
    gw = d_ssm // SSM_GROUPS
    hpg = n_heads // SSM_GROUPS

    @pl.when(c == 0)
    def _():
        h_sc[...] = h0_ref[0]
        xf_sc[0:8] = cprev_ref[0]

    if nvalid < L:
        xf_sc[8:8 + L] = jnp.zeros((L, xf_sc.shape[1]), F32)
        xf_sc[8:8 + nvalid] = xbc_ref[0]
        zpad = jnp.zeros((L - nvalid, LANES), F32)
        dt_raw = jnp.concatenate([dt_ref[0], zpad], axis=0)
    else:
        xf_sc[8:8 + L] = xbc_ref[...]
        dt_raw = dt_ref[...]
    conv = cb_ref[...] + cw_ref[0:1, :] * xf_sc[5:5 + L]
    for w in range(1, CONV_W):
        conv = conv + cw_ref[w:w + 1, :] * xf_sc[5 + w:5 + w + L]
    cnew_ref[0] = xf_sc[nvalid:nvalid + 8]
    xf_sc[0:8] = xf_sc[L:L + 8]

    row = lax.broadcasted_iota(I32, (L, LANES), 0)
    lane = lax.broadcasted_iota(I32, (L, LANES), 1)
    if nvalid < L:
        rowc = lax.broadcasted_iota(I32, conv.shape, 0)
        conv = jnp.where(rowc < nvalid, conv, 0.0)
    act = conv * jax.nn.sigmoid(conv)
    xs = act[:, :d_ssm]
    bm = act[:, d_ssm:d_ssm + SSM_GROUPS * D_STATE].astype(BF16)
    cm = act[:, d_ssm + SSM_GROUPS * D_STATE:].astype(BF16)

    dt = jax.nn.softplus(dt_raw + dtb_ref[...])
    live = lane < n_heads
    if nvalid < L:
        live = live & (row < nvalid)
    dt = jnp.where(live, dt, 0.0)
    dta = dt * (-jnp.exp(alog_ref[...]))
    tril = row >= lane
    acum = jnp.dot(tril.astype(F32), dta, precision=lax.Precision.HIGHEST, preferred_element_type=F32)
    last = acum[L - 1:L, :]
    acum_t = acum.T
    eb = e_ref[...]
    stacked = jnp.concatenate([jnp.exp(last - acum) * dt, jnp.exp(acum), dt], axis=0)
    ex = _split_dot(stacked, eb)
    xw = (xs * ex[0:L]).astype(BF16)
    eac_x = ex[L:2 * L]
    xd = xs * ex[2 * L:3 * L]

    cd_col = jnp.broadcast_to(jnp.exp(acum_t[:, L - 1:L]), (LANES, LANES))
    cd_rows = _split_dot_left(r_ref[...], cd_col)

    h0l = lane < SSM_HD
    y_parts = []
    yoff_parts = []
    for g in range(SSM_GROUPS):
        bg = bm[:, g * D_STATE:(g + 1) * D_STATE]
        cg = cm[:, g * D_STATE:(g + 1) * D_STATE]
        cbg = _dot_nt(cg, bg)
        rows = slice(g * gw, (g + 1) * gw)
        hg = h_sc[rows, :]
        yoff_parts.append(_dot_nt(cg, hg.astype(BF16)))
        st = lax.dot_general(xw[:, rows], bg, (((0,), (0,)), ((), ())), preferred_element_type=F32)
        h_sc[rows, :] = cd_rows[rows, :] * hg + st
        for jp in range(hpg // 2):
            ms = []
            for hh in range(2):
                h = g * hpg + 2 * jp + hh
                diff = acum[:, h:h + 1] - acum_t[h:h + 1, :]
                ms.append(cbg * jnp.exp(jnp.where(tril, diff, NEG)))
            lhs = jnp.concatenate(ms, axis=1).astype(BF16)
            col = (g * hpg + 2 * jp) * SSM_HD
            xp = xd[:, col:col + LANES]
            rhs = jnp.concatenate([jnp.where(h0l, xp, 0.0), jnp.where(h0l, 0.0, xp)], axis=0).astype(BF16)
            y_parts.append(_dot(lhs, rhs))
    y = jnp.concatenate(y_parts, axis=1) + jnp.concatenate(yoff_parts, axis=1) * eac_x + dskx_ref[...] * xs
    if nvalid < L:
        y = y[:8]
        z = jnp.concatenate([z_ref[0], jnp.zeros((8 - nvalid, d_ssm), F32)], axis=0)
    else:
        z = z_ref[...]
    y = y * (z * jax.nn.sigmoid(z))
    outs = []
    for g in range(SSM_GROUPS):
        yg = y[:, g * gw:(g + 1) * gw]
        outs.append(yg * lax.rsqrt(jnp.mean(yg * yg, axis=-1, keepdims=True) + EPS))
    y = jnp.concatenate(outs, axis=1) * nw_ref[...]
    if nvalid < L:
        y_ref[0] = y[:nvalid].astype(y_ref.dtype)
    else:
        y_ref[...] = y.astype(y_ref.dtype)
    hout_ref[0] = h_sc[...]


def _split_dot_left(onehot_bf16, v):
    p1 = v.astype(BF16)
    r1 = v - p1.astype(F32)
    p2 = r1.astype(BF16)
    p3 = (r1 - p2.astype(F32)).astype(BF16)
    return _dot(onehot_bf16, p1) + _dot(onehot_bf16, p2) + _dot(onehot_bf16, p3)


def _ssd(xbc_src, z_src, dt_src, conv_prev8, h0, consts, *, batch, n_chunks, nvalid, xbc_spec, z_spec, dt_spec,
         d_ssm, n_heads):
    conv_dim = conv_prev8.shape[-1]
    hp = h0.shape[1]
    body = functools.partial(_ssd_body, nvalid=nvalid, d_ssm=d_ssm, n_heads=n_heads)
    cw, cb, dtb, alog, dskx, nw, e_mat, r_mat = consts

    def const(a):
        return pl.BlockSpec(a.shape, lambda b, c: (0,) * a.ndim)

    if nvalid < CHUNK:
        assert n_chunks == 1 and nvalid <= 8
        y_spec = pl.BlockSpec((1, nvalid, d_ssm), lambda b, c: (b, 0, 0))
        y_shape = jax.ShapeDtypeStruct((batch, nvalid, d_ssm), BF16)
    else:
        y_spec = pl.BlockSpec((CHUNK, d_ssm), lambda b, c: (b * n_chunks + c, 0))
        y_shape = jax.ShapeDtypeStruct((batch * n_chunks * CHUNK, d_ssm), BF16)
    return pl.pallas_call(
        body,
        grid=(batch, n_chunks),
        in_specs=[
            xbc_spec, z_spec, dt_spec,
            pl.BlockSpec((1, 8, conv_dim), lambda b, c: (b, 0, 0)),
            pl.BlockSpec((1, hp, D_STATE), lambda b, c: (b, 0, 0)),
            const(cw), const(cb), const(dtb), const(alog), const(dskx), const(nw), const(e_mat), const(r_mat),
        ],
        out_specs=[
            y_spec,
            pl.BlockSpec((1, 8, conv_dim), lambda b, c: (b, 0, 0)),
            pl.BlockSpec((1, hp, D_STATE), lambda b, c: (b, 0, 0)),
        ],
        out_shape=[
            y_shape,
            jax.ShapeDtypeStruct((batch, 8, conv_dim), F32),
            jax.ShapeDtypeStruct((batch, hp, D_STATE), F32),
        ],
        scratch_shapes=[pltpu.VMEM((hp, D_STATE), F32), pltpu.VMEM((CHUNK + 8, conv_dim), F32)],
        compiler_params=_cparams(("parallel", "arbitrary"), 48),
        name="ssd",
    )(xbc_src, z_src, dt_src, conv_prev8, h0, cw, cb, dtb, alog, dskx, nw, e_mat, r_mat)


def _outproj_body(x_ref, ya_ref, ys_ref, wa_ref, ws_ref, nfw_ref, wr_ref, br_ref,
                  xmid_ref, xpk_ref, idx_ref, gate_ref):
    tm, d = x_ref.shape
    pr = tm // OUTPROJ_PARTS
    parts = range(OUTPROJ_PARTS)
    rs = [pl.ds(i * pr, pr) for i in parts]
    mix = [_dot(ya_ref[r, :].astype(BF16), wa_ref[...]) + _dot(ys_ref[r, :], ws_ref[...]) for r in rs]
    xm = [x_ref[rs[i], :] + mix[i] for i in parts]
    for i in parts:
        xmid_ref[rs[i], :] = xm[i]
    xn = [xm[i] * lax.rsqrt(jnp.mean(xm[i] * xm[i], axis=-1, keepdims=True) + EPS) * nfw_ref[...] for i in parts]
    xh = [xn[i].astype(BF16) for i in parts]
    xhf = [xh[i].astype(F32) for i in parts]
    for i in parts:
        _store_packed(xpk_ref, pr, _pack_words(xh[i][:, :d // 2], xh[i][:, d // 2:]), row0=i * pr)
    xl = [(xn[i] - xhf[i]).astype(BF16) for i in parts]
    hl = [_dot(xh[i], wr_ref[...]) for i in parts]
    lh = [_dot(xl[i], wr_ref[:, :LANES]) for i in parts]
    logits = [hl[i][:, :LANES] + hl[i][:, LANES:] + lh[i] + br_ref[...] for i in parts]
    lane = lax.broadcasted_iota(I32, (pr, LANES), 1)
    idx_out = [jnp.zeros((pr, LANES), I32) for _ in parts]
    val_out = [jnp.full((pr, LANES), NEG, F32) for _ in parts]
    for k in range(TOP_K):
        mx = [jnp.max(logits[i], axis=1, keepdims=True) for i in parts]
        ix = [jnp.min(jnp.where(logits[i] == mx[i], lane, LANES), axis=1, keepdims=True) for i in parts]
        idx_out = [jnp.where(lane == k, ix[i], idx_out[i]) for i in parts]
        val_out = [jnp.where(lane == k, mx[i], val_out[i]) for i in parts]
        logits = [jnp.where(lane == ix[i], -jnp.inf, logits[i]) for i in parts]
    for i in parts:
        ev = jnp.exp(val_out[i] - jnp.max(val_out[i], axis=1, keepdims=True))
        gate_ref[rs[i], :] = ev / jnp.sum(ev, axis=1, keepdims=True)
        idx_ref[rs[i], :] = idx_out[i]


def _outproj(x_all, y_att, y_ssd, w_att, w_ssd, nfw, w_router, b_router):
    m, d = x_all.shape
    assert d == 2 * PACK_ROWS * LANES
    tm = _row_tile(m, TM_PROJ)

    def rows(width):
        return pl.BlockSpec((tm, width), lambda i: (i, 0))

    def const(a):
        return pl.BlockSpec(a.shape, lambda i: (0,) * a.ndim)

    return pl.pallas_call(
        _outproj_body,
        grid=(m // tm,),
        in_specs=[rows(d), rows(y_att.shape[1]), rows(y_ssd.shape[1]), const(w_att), const(w_ssd), const(nfw),
                  const(w_router), const(b_router)],
        out_specs=[rows(d), pl.BlockSpec((tm * PACK_ROWS, LANES), lambda i: (i, 0)), rows(LANES), rows(LANES)],
        out_shape=[jax.ShapeDtypeStruct((m, d), F32), jax.ShapeDtypeStruct((m * PACK_ROWS, LANES), U32),
                   jax.ShapeDtypeStruct((m, LANES), I32), jax.ShapeDtypeStruct((m, LANES), F32)],
        compiler_params=_cparams(("parallel",), 56),
        name="outproj",
    )(x_all, y_att, y_ssd, w_att, w_ssd, nfw, w_router, b_router)


def _route_body(idx_ref, cin_ref, rank_ref, cout_ref, carry):
    @pl.when(pl.program_id(0) == 0)
    def _():
        carry[...] = cin_ref[...]

    tr = idx_ref.shape[0]
    lane = lax.broadcasted_iota(I32, (tr, LANES), 1)
    tril = (lax.broadcasted_iota(I32, (tr, tr), 0) >= lax.broadcasted_iota(I32, (tr, tr), 1))
    tril = jnp.where(tril, 1.0, 0.0).astype(BF16)
    idx = idx_ref[...]
    cnt = carry[...]
    out = jnp.zeros((tr, LANES), I32)
    for k in range(TOP_K):
        hit = lane == idx[:, k:k + 1]
        pre = _dot(tril, jnp.where(hit, 1.0, 0.0).astype(BF16))
        rk = jnp.sum(jnp.where(hit, pre + cnt - 1.0, 0.0), axis=1, keepdims=True)
        out = jnp.where(lane == k, rk.astype(I32), out)
        cnt = cnt + pre[tr - 1:tr, :]
    carry[...] = cnt
    rank_ref[...] = out
    cout_ref[...] = cnt


def _route(top_idx, counts_in):
    m = top_idx.shape[0]
    tr = _row_tile(m, TM_ROUTE)
    return pl.pallas_call(
        _route_body,
        grid=(m // tr,),
        in_specs=[pl.BlockSpec((tr, LANES), lambda i: (i, 0)), pl.BlockSpec((1, LANES), lambda i: (0, 0))],
        out_specs=[pl.BlockSpec((tr, LANES), lambda i: (i, 0)), pl.BlockSpec((1, LANES), lambda i: (0, 0))],
        out_shape=[jax.ShapeDtypeStruct((m, LANES), I32), jax.ShapeDtypeStruct((1, LANES), F32)],
        scratch_shapes=[pltpu.VMEM((1, LANES), F32)],
        compiler_params=_cparams(("arbitrary",), 32),
        name="route",
    )(top_idx, counts_in)


def _dispatch_body(pos_ref, x_ref, xd_in, xd_hbm, sem):
    del xd_in
    i = pl.program_id(0)
    ts = x_ref.shape[0] // PACK_ROWS

    def row_copy(r, k):
        p = pos_ref[(i * ts + r) * TOP_K + k]
        src = x_ref.at[pl.ds(pl.multiple_of(r * PACK_ROWS, PACK_ROWS), PACK_ROWS), :]
        dst = xd_hbm.at[pl.ds(pl.multiple_of(p * PACK_ROWS, PACK_ROWS), PACK_ROWS), :]
        return pltpu.make_async_copy(src, dst, sem)

    def issue(r, c):
        for k in range(TOP_K):
            row_copy(r, k).start()
        return c
    lax.fori_loop(0, ts, issue, 0)

    def drain(r, c):
        for k in range(TOP_K):
            row_copy(r, k).wait()
        return c
    lax.fori_loop(0, ts, drain, 0)


def _dispatch(pos_flat, x_packed, x_disp):
    m = x_packed.shape[0] // PACK_ROWS
    ts = _row_tile(m, TM_DISPATCH)
    return pl.pallas_call(
        _dispatch_body,
        grid_spec=pltpu.PrefetchScalarGridSpec(
            num_scalar_prefetch=1,
            grid=(m // ts,),
            in_specs=[pl.BlockSpec((ts * PACK_ROWS, LANES), lambda i, pos: (i, 0)),
                      pl.BlockSpec(memory_space=pl.ANY)],
            out_specs=pl.BlockSpec(memory_space=pl.ANY),
            scratch_shapes=[pltpu.SemaphoreType.DMA(())],
        ),
        out_shape=jax.ShapeDtypeStruct(x_disp.shape, x_disp.dtype),
        input_output_aliases={2: 0},
        compiler_params=_cparams(("arbitrary",), 32),
        name="dispatch",
    )(pos_flat, x_packed, x_disp)


def _tile_changed(te_ref, i):
    return (i == 0) | (te_ref[i] != te_ref[jnp.maximum(i - 1, 0)])


def _gmm1_body(te_ref, nu_ref, x_ref, wg_ref, wu_ref, bg_ref, bu_ref, h_ref, wgb, wub):
    i = pl.program_id(1)

    @pl.when(_tile_changed(te_ref, i))
    def _():
        wgb[...] = wg_ref[0].astype(BF16)
        wub[...] = wu_ref[0].astype(BF16)

    @pl.when(i < nu_ref[0])
    def _():
        tm = h_ref.shape[0]
        x = _load_packed(lambda s: x_ref[pl.ds(s, tm, stride=PACK_ROWS), :], BF16)
        gt = _dot(x, wgb[...]) + bg_ref[0]
        up = _dot(x, wub[...]) + bu_ref[0]
        gt = jnp.minimum(gt, SWIGLU_LIMIT)
        up = jnp.clip(up, -SWIGLU_LIMIT, SWIGLU_LIMIT)
        act = (up + 1.0) * gt * jax.nn.sigmoid(SWIGLU_ALPHA * gt)
        h_ref[...] = act.astype(h_ref.dtype)

    @pl.when(i >= nu_ref[0])
    def _():
        h_ref[...] = jnp.zeros(h_ref.shape, h_ref.dtype)


def _gmm2_body(te_ref, nu_ref, h_ref, wd_ref, bd_ref, y_ref, wdb):
    i = pl.program_id(0)

    @pl.when(_tile_changed(te_ref, i))
    def _():
        wdb[...] = wd_ref[0].astype(BF16)

    @pl.when(i < nu_ref[0])
    def _():
        tm, half = h_ref.shape[0], wdb.shape[1] // 2
        h = h_ref[...]
        lo = _dot(h, wdb[:, :half]) + bd_ref[0][:, :half]
        hi = _dot(h, wdb[:, half:]) + bd_ref[0][:, half:]
        _store_packed(y_ref, tm, _pack_words(lo.astype(BF16), hi.astype(BF16)))

    @pl.when(i >= nu_ref[0])
    def _():
        y_ref[...] = jnp.zeros(y_ref.shape, y_ref.dtype)


def _used_tile(i, nu):
    return jnp.minimum(i, nu[0] - 1)


def _gmm1(tile_exp, n_used, x_disp, w_gate, w_up, b_gate, b_up):
    cap = x_disp.shape[0] // PACK_ROWS
    d, dff = w_gate.shape[1], w_gate.shape[2]
    tm, tn = TM_GMM, TN_GMM
    x_spec = pl.BlockSpec((tm * PACK_ROWS, LANES), lambda j, i, te, nu: (_used_tile(i, nu), 0))
    w_spec = pl.BlockSpec((1, d, tn), lambda j, i, te, nu: (te[_used_tile(i, nu)], 0, j))
    b_spec = pl.BlockSpec((1, 1, tn), lambda j, i, te, nu: (te[_used_tile(i, nu)], 0, j))
    return pl.pallas_call(
        _gmm1_body,
        grid_spec=pltpu.PrefetchScalarGridSpec(
            num_scalar_prefetch=2,
            grid=(dff // tn, cap // tm),
            in_specs=[x_spec, w_spec, w_spec, b_spec, b_spec],
            out_specs=pl.BlockSpec((tm, tn), lambda j, i, te, nu: (i, j)),
            scratch_shapes=[pltpu.VMEM((d, tn), BF16), pltpu.VMEM((d, tn), BF16)],
        ),
        out_shape=jax.ShapeDtypeStruct((cap, dff), BF16),
        compiler_params=_cparams(("arbitrary", "arbitrary"), 60),
        name="gmm1",
    )(tile_exp, n_used, x_disp, w_gate, w_up, b_gate, b_up)


def _gmm2(tile_exp, n_used, h_disp, w_down, b_down):
    cap, dff = h_disp.shape
    d = w_down.shape[2]
    assert d == 2 * PACK_ROWS * LANES
    tm = TM_GMM
    return pl.pallas_call(
        _gmm2_body,
        grid_spec=pltpu.PrefetchScalarGridSpec(
            num_scalar_prefetch=2,
            grid=(cap // tm,),
            in_specs=[
                pl.BlockSpec((tm, dff), lambda i, te, nu: (_used_tile(i, nu), 0)),
                pl.BlockSpec((1, dff, d), lambda i, te, nu: (te[_used_tile(i, nu)], 0, 0)),
                pl.BlockSpec((1, 1, d), lambda i, te, nu: (te[_used_tile(i, nu)], 0, 0)),
            ],
            out_specs=pl.BlockSpec((tm * PACK_ROWS, LANES), lambda i, te, nu: (i, 0)),
            scratch_shapes=[pltpu.VMEM((dff, d), BF16)],
        ),
        out_shape=jax.ShapeDtypeStruct((cap * PACK_ROWS, LANES), U32),
        compiler_params=_cparams(("arbitrary",), 60),
        name="gmm2",
    )(tile_exp, n_used, h_disp, w_down, b_down)


def _combine_body(pos_ref, y_hbm, xmid_ref, gate_ref, fw_ref, o_ref, buf, sem):
    i = pl.program_id(0)
    n = pl.num_programs(0)
    tc = TOK_COMBINE

    def row_copy(step, slot, r, k):
        p = pos_ref[(step * tc + r) * TOP_K + k]
        src = y_hbm.at[pl.ds(pl.multiple_of(p * PACK_ROWS, PACK_ROWS), PACK_ROWS), :]
        dst = buf.at[slot, k, pl.ds(pl.multiple_of(r * PACK_ROWS, PACK_ROWS), PACK_ROWS), :]
        return pltpu.make_async_copy(src, dst, sem.at[slot])

    def issue(step, slot):
        def body(r, c):
            for k in range(TOP_K):
                row_copy(step, slot, r, k).start()
            return c
        lax.fori_loop(0, tc, body, 0)

    @pl.when(i == 0)
    def _():
        issue(0, 0)

    @pl.when(i + 1 < n)
    def _():
        issue(i + 1, (i + 1) % 2)

    slot = i % 2

    def wait_body(r, c):
        for k in range(TOP_K):
            row_copy(i, slot, r, k).wait()
        return c
    lax.fori_loop(0, tc, wait_body, 0)

    gates = gate_ref[...]
    acc = xmid_ref[...]
    for k in range(TOP_K):
        y_k = _load_packed(lambda s, k=k: buf[slot, k, pl.ds(s, tc, stride=PACK_ROWS), :], F32)
        acc = acc + gates[:, k:k + 1] * y_k
    o_ref[...] = acc * lax.rsqrt(jnp.mean(acc * acc, axis=-1, keepdims=True) + EPS) * fw_ref[...]


def _combine(pos_flat, y_disp, x_mid, gates, final_w):
    m, d = x_mid.shape
    tc = TOK_COMBINE
    return pl.pallas_call(
        _combine_body,
        grid_spec=pltpu.PrefetchScalarGridSpec(
            num_scalar_prefetch=1,
            grid=(m // tc,),
            in_specs=[
                pl.BlockSpec(memory_space=pl.ANY),
                pl.BlockSpec((tc, d), lambda i, pos: (i, 0)),
                pl.BlockSpec((tc, LANES), lambda i, pos: (i, 0)),
                pl.BlockSpec((1, d), lambda i, pos: (0, 0)),
            ],
            out_specs=pl.BlockSpec((tc, d), lambda i, pos: (i, 0)),
            scratch_shapes=[pltpu.VMEM((2, TOP_K, tc * PACK_ROWS, LANES), U32), pltpu.SemaphoreType.DMA((2,))],
        ),
        out_shape=jax.ShapeDtypeStruct((m, d), F32),
        compiler_params=_cparams(("arbitrary",), 32),
        name="combine",
    )(pos_flat, y_disp, x_mid, gates, final_w)


def _tile_layout(counts, n_assign, n_exp):
    tiles = (counts + TM_GMM - 1) // TM_GMM
    tile_end = jnp.cumsum(tiles)
    row_start = (tile_end - tiles) * TM_GMM
    n_tiles = -(-n_assign // TM_GMM) + n_exp
    n_used = tile_end[-1]
    tile_ids = jnp.minimum(jnp.arange(n_tiles, dtype=I32), n_used - 1)
    tile_exp = jnp.minimum(jnp.sum((tile_end[None, :] <= tile_ids[:, None]).astype(I32), axis=1), n_exp - 1)
    return row_start.astype(I32), tile_exp.astype(I32), n_used.reshape(1).astype(I32), n_tiles * TM_GMM


def _positions(top_idx, rank, row_start, n_exp):
    idx = top_idx[:, :TOP_K]
    onehot = idx[:, :, None] == jnp.arange(n_exp, dtype=I32)[None, None, :]
    base = jnp.sum(jnp.where(onehot, row_start[None, None, :], 0), axis=-1)
    return (base + rank[:, :TOP_K]).reshape(-1).astype(I32)


def kernel(x_prompt, x_sample, cache_attn_k, cache_attn_v, state_conv, state_ssm, norm_mix_w, w_in, conv_w, conv_b, dt_bias, a_log, d_skip, ssd_norm_w, w_out, norm_ffn_w, w_router, b_router, w_gate, b_gate, w_up, b_up, w_down, b_down, norm_final_w):
    depth = norm_mix_w.shape[0]
    assert depth == 1, "single-layer step"
    bp, seq, d = x_prompt.shape
    bs, dq, _ = x_sample.shape
    n_att_heads = cache_attn_k.shape[3]
    d_att = n_att_heads * HD_ATT
    n_pairs = d_att // LANES
    n_heads = a_log.shape[1]
    d_ssm = n_heads * SSM_HD
    conv_dim = conv_w.shape[2]
    n_exp = w_router.shape[2]
    n_prompt = bp * seq
    n_sample = bs * dq
    assert n_sample == CHUNK and seq % SUPER == 0 and n_heads <= LANES

    wl = w_in[0]
    o_q, o_k, o_v, o_z, o_x, o_dt = 0, d_att, 2 * d_att, 3 * d_att, 3 * d_att + d_ssm, 3 * d_att + d_ssm + conv_dim
    w_main = jnp.concatenate([wl[:, o_x:o_dt], wl[:, o_q:o_z], wl[:, o_z:o_x]], axis=1).astype(BF16)
    w_dt = jnp.pad(wl[:, o_dt:], ((0, 0), (0, LANES - n_heads))).astype(BF16)
    col_q = conv_dim // LANES
    col_k = col_q + n_pairs
    col_v = col_k + n_pairs
    w_att = w_out[0, :d_att].astype(BF16)
    w_ssd = w_out[0, d_att:].astype(BF16)
    w_rp = jnp.pad(w_router[0], ((0, 0), (0, LANES - n_exp)))
    w_rh = w_rp.astype(BF16)
    w_r = jnp.concatenate([w_rh, (w_rp - w_rh.astype(F32)).astype(BF16)], axis=1)
    b_r = jnp.pad(b_router[0], (0, LANES - n_exp), constant_values=NEG).reshape(1, LANES)
    pad_h = (0, LANES - n_heads)
    head_of_col = jnp.arange(d_ssm, dtype=I32) // SSM_HD
    e_mat = (jnp.arange(LANES, dtype=I32)[:, None] == head_of_col[None, :]).astype(BF16)
    r_mat = (head_of_col[:, None] == jnp.arange(LANES, dtype=I32)[None, :]).astype(BF16)
    ssd_consts = (
        conv_w[0], conv_b[0].reshape(1, conv_dim),
        jnp.pad(dt_bias[0], pad_h).reshape(1, LANES), jnp.pad(a_log[0], pad_h).reshape(1, LANES),
        jnp.repeat(d_skip[0], SSM_HD).reshape(1, d_ssm), ssd_norm_w[0].reshape(1, d_ssm), e_mat, r_mat,
    )

    xp = x_prompt.reshape(n_prompt, d)
    xs = x_sample.reshape(n_sample, d)
    nmw = norm_mix_w[0].reshape(1, d)
    proj_p, dt_p = _inproj(xp, nmw, w_main, w_dt)
    proj_s, dt_s = _inproj(xs, nmw, w_main, w_dt)
    n_main = proj_p.shape[1]
    z_col = (conv_dim + 3 * d_att) // d_ssm
    assert z_col * d_ssm == conv_dim + 3 * d_att

    win_p = min(BRANCHES[-1][0], seq)
    y_att_p, k_p, v_p = _attn_prompt(proj_p, bp, seq, win_p, col_q, col_k, col_v, n_pairs)
    ck = cache_attn_k[0].reshape(bs, -1, d_att)
    cv = cache_attn_v[0].reshape(bs, -1, d_att)
    y_att_s = _attn_sample(proj_s, ck, cv, col_q, col_k, col_v, n_pairs, bs, dq)

    n_chunks = seq // CHUNK
    hp = n_heads * SSM_HD
    y_ssd_p, cnew_p, h_p = _ssd(
        proj_p, proj_p, dt_p, jnp.zeros((bp, 8, conv_dim), F32), jnp.zeros((bp, hp, D_STATE), F32), ssd_consts,
        batch=bp, n_chunks=n_chunks, nvalid=CHUNK,
        xbc_spec=pl.BlockSpec((CHUNK, conv_dim), lambda b, c: (b * n_chunks + c, 0)),
        z_spec=pl.BlockSpec((CHUNK, d_ssm), lambda b, c: (b * n_chunks + c, z_col)),
        dt_spec=pl.BlockSpec((CHUNK, LANES), lambda b, c: (b * n_chunks + c, 0)),
        d_ssm=d_ssm, n_heads=n_heads)

    proj_s3 = proj_s.reshape(bs, dq, n_main)
    cprev_s = jnp.pad(state_conv[0], ((0, 0), (8 - (CONV_W - 1), 0), (0, 0)))
    y_ssd_s, cnew_s, h_s = _ssd(
        proj_s3, proj_s3, dt_s.reshape(bs, dq, LANES), cprev_s, state_ssm[0].reshape(bs, hp, D_STATE), ssd_consts,
        batch=bs, n_chunks=1, nvalid=dq,
        xbc_spec=pl.BlockSpec((1, dq, conv_dim), lambda b, c: (b, 0, 0)),
        z_spec=pl.BlockSpec((1, dq, d_ssm), lambda b, c: (b, 0, z_col)),
        dt_spec=pl.BlockSpec((1, dq, LANES), lambda b, c: (b, 0, 0)),
        d_ssm=d_ssm, n_heads=n_heads)

    nfw = norm_ffn_w[0].reshape(1, d)
    xmid_p, xpk_p, idx_p, gates_p = _outproj(xp, y_att_p, y_ssd_p, w_att, w_ssd, nfw, w_r, b_r)
    xmid_s, xpk_s, idx_s, gates_s = _outproj(xs, y_att_s.reshape(n_sample, d_att),
                                             y_ssd_s.reshape(n_sample, d_ssm), w_att, w_ssd, nfw, w_r, b_r)

    rank_p, cnt_p = _route(idx_p, jnp.zeros((1, LANES), F32))
    rank_s, cnt_all = _route(idx_s, cnt_p)
    counts = cnt_all[0, :n_exp].astype(I32)
    n_assign = (n_prompt + n_sample) * TOP_K
    row_start, tile_exp, n_used, cap = _tile_layout(counts, n_assign, n_exp)
    pos_p = _positions(idx_p, rank_p, row_start, n_exp)
    pos_s = _positions(idx_s, rank_s, row_start, n_exp)
    x_disp = jnp.zeros((cap * PACK_ROWS, LANES), U32)
    x_disp = _dispatch(pos_p, xpk_p, x_disp)
    x_disp = _dispatch(pos_s, xpk_s, x_disp)
    h_disp = _gmm1(tile_exp, n_used, x_disp, w_gate[0], w_up[0],
                   b_gate[0].reshape(n_exp, 1, -1), b_up[0].reshape(n_exp, 1, -1))
    y_disp = _gmm2(tile_exp, n_used, h_disp, w_down[0], b_down[0].reshape(n_exp, 1, -1))
    fw = norm_final_w.reshape(1, d)
    y_p = _combine(pos_p, y_disp, xmid_p, gates_p, fw)
    y_s = _combine(pos_s, y_disp, xmid_s, gates_s, fw)

    k_p = k_p.reshape(1, bp, win_p, n_att_heads, HD_ATT)
    v_p = v_p.reshape(1, bp, win_p, n_att_heads, HD_ATT)
    k_s = proj_s3[:, :, col_k * LANES:col_v * LANES].reshape(1, bs, dq, n_att_heads, HD_ATT)
    v_s = proj_s3[:, :, col_v * LANES:col_v * LANES + d_att].reshape(1, bs, dq, n_att_heads, HD_ATT)
    return (
        y_p.reshape(bp, seq, d), y_s.reshape(bs, dq, d), k_p, v_p,
        cnew_p[:, 8 - (CONV_W - 1):][None], h_p.reshape(1, bp, n_heads, SSM_HD, D_STATE),
        k_s, v_s,
        cnew_s[:, 8 - (CONV_W - 1):][None], h_s.reshape(1, bs, n_heads, SSM_HD, D_STATE),
    )
```

```python
import functools

import jax
import jax.numpy as jnp
from jax import lax
from jax.experimental import pallas as pl
from jax.experimental.pallas import tpu as pltpu

F32 = jnp.float32
BF16 = jnp.bfloat16
I32 = jnp.int32
U32 = jnp.uint32

EPS = 1e-5
NEG = -1e30
LANES = 128
HD_ATT = 64
BRANCHES = ((128, 1), (512, 4), (2048, 16))
RES = 16
ATT_BLK = 128
ATT_BLOCKS_PER_STEP = 2
SUPER = RES * ATT_BLK
SSM_HD = 64
SSM_GROUPS = 4
D_STATE = 128
CONV_W = 4
CHUNK = 128
TOP_K = 4
SWIGLU_LIMIT = 7.0
SWIGLU_ALPHA = 1.702

PACK_ROWS = 8

TM_PROJ = 512
OUTPROJ_PARTS = 2
TN_PROJ = 2048
TM_ROUTE = 512
TM_DISPATCH = 512
TM_GMM = 512
TN_GMM = 1024
TOK_COMBINE = 128


def _row_tile(m, cap):
    t = cap - cap % LANES
    while m % t:
        t -= LANES
    return t


def _cparams(sem, vmem_mb):
    return pltpu.CompilerParams(dimension_semantics=sem, vmem_limit_bytes=vmem_mb << 20)


def _dot(a, b):
    return jnp.dot(a, b, preferred_element_type=F32)


def _dot_nt(a, b):
    return lax.dot_general(a, b, (((1,), (1,)), ((), ())), preferred_element_type=F32)


def _split_dot(v, onehot_bf16):
    hi = v.astype(BF16)
    lo = (v - hi.astype(F32)).astype(BF16)
    return _dot(hi, onehot_bf16) + _dot(lo, onehot_bf16)


def _pack_words(lo, hi):
    lo_bits = lax.bitcast_convert_type(lo.astype(F32), U32)
    hi_bits = lax.bitcast_convert_type(hi.astype(F32), U32)
    return (hi_bits & jnp.uint32(0xFFFF0000)) | (lo_bits >> 16)


def _store_packed(ref, rows, words, row0=0):
    for s in range(PACK_ROWS):
        ref[pl.ds(row0 * PACK_ROWS + s, rows, stride=PACK_ROWS), :] = words[:, s * LANES:(s + 1) * LANES]


def _load_packed(load_slab, dtype):
    lo, hi = [], []
    for s in range(PACK_ROWS):
        w = load_slab(s)
        lo.append(lax.bitcast_convert_type(w << 16, F32).astype(dtype))
        hi.append(lax.bitcast_convert_type(w & jnp.uint32(0xFFFF0000), F32).astype(dtype))
    return jnp.concatenate(lo + hi, axis=1)


def _inproj_body(x_ref, nw_ref, w_ref, wdt_ref, o_ref, dt_ref, h_sc):
    @pl.when(pl.program_id(1) == 0)
    def _():
        x = x_ref[...]
        ms = jnp.mean(x * x, axis=-1, keepdims=True)
        h = (x * lax.rsqrt(ms + EPS) * nw_ref[...]).astype(BF16)
        h_sc[...] = h
        dt_ref[...] = _dot(h, wdt_ref[...])

    o_ref[...] = _dot(h_sc[...], w_ref[...])


def _inproj(x_all, norm_w, w_main, w_dt):
    m, d = x_all.shape
    n = w_main.shape[1]
    tm, tn = _row_tile(m, TM_PROJ), TN_PROJ
    return pl.pallas_call(
        _inproj_body,
        grid=(m // tm, n // tn),
        in_specs=[
            pl.BlockSpec((tm, d), lambda i, j: (i, 0)),
            pl.BlockSpec((1, d), lambda i, j: (0, 0)),
            pl.BlockSpec((d, tn), lambda i, j: (0, j)),
            pl.BlockSpec((d, LANES), lambda i, j: (0, 0)),
        ],
        out_specs=[
            pl.BlockSpec((tm, tn), lambda i, j: (i, j)),
            pl.BlockSpec((tm, LANES), lambda i, j: (i, 0)),
        ],
        out_shape=[jax.ShapeDtypeStruct((m, n), F32), jax.ShapeDtypeStruct((m, LANES), F32)],
        scratch_shapes=[pltpu.VMEM((tm, d), BF16)],
        compiler_params=_cparams(("parallel", "arbitrary"), 48),
        name="inproj",
    )(x_all, norm_w, w_main, w_dt)


def _attn_blocks(args):
    lane = lax.broadcasted_iota(I32, (ATT_BLK, LANES), 1)
    h0 = lane < HD_ATT
    hk = lax.broadcasted_iota(I32, (2 * ATT_BLK, LANES), 1) < HD_ATT
    n = len(args)
    qs = [jnp.concatenate([jnp.where(h0, a[0], 0.0), jnp.where(h0, 0.0, a[0])], axis=0).astype(BF16) for a in args]
    kk = [jnp.concatenate([a[1], a[2]], axis=0).astype(BF16) for a in args]
    s = [_dot_nt(qs[i], kk[i]) for i in range(n)]
    s = [s[i] + jnp.concatenate([args[i][8], args[i][8]], axis=0) for i in range(n)]
    mn = [jnp.maximum(args[i][5], jnp.max(s[i], axis=1, keepdims=True)) for i in range(n)]
    p = [jnp.exp(s[i] - mn[i]).astype(BF16) for i in range(n)]
    vv = [jnp.concatenate([a[3], a[4]], axis=0) for a in args]
    v0 = [jnp.where(hk, vv[i], 1.0).astype(BF16) for i in range(n)]
    v1 = [jnp.where(hk, 1.0, vv[i]).astype(BF16) for i in range(n)]
    o0 = [_dot(p[i][:ATT_BLK], v0[i]) for i in range(n)]
    o1 = [_dot(p[i][ATT_BLK:], v1[i]) for i in range(n)]
    al = [jnp.exp(args[i][5] - mn[i]) for i in range(n)]
    a0 = [jnp.broadcast_to(al[i][:ATT_BLK], (ATT_BLK, LANES)) for i in range(n)]
    a1 = [jnp.broadcast_to(al[i][ATT_BLK:], (ATT_BLK, LANES)) for i in range(n)]
    ln = [jnp.where(h0, a1[i], a0[i]) * args[i][6] + jnp.where(h0, o1[i], o0[i]) for i in range(n)]
    an = [jnp.where(h0, a0[i], a1[i]) * args[i][7] + jnp.where(h0, o0[i], o1[i]) for i in range(n)]
    return [(mn[i], ln[i], an[i]) for i in range(n)]


def _attn_prompt_body(q_ref, k_ref, v_ref, o_ref, kw_ref, vw_ref, qr, kr, vr, m_sc, l_sc, acc_sc, bias_sc, *, seq):
    ni = seq // RES
    scale = HD_ATT ** -0.5
    win = kw_ref.shape[0]
    kw_ref[...] = k_ref[seq - win:, :]
    vw_ref[...] = v_ref[seq - win:, :]
    for j in range(RES):
        qr[j] = q_ref[pl.ds(j, ni, stride=RES), :] * scale
        kr[j] = k_ref[pl.ds(j, ni, stride=RES), :]
        vr[j] = v_ref[pl.ds(j, ni, stride=RES), :]

    rq = lax.broadcasted_iota(I32, (ATT_BLK, 2 * ATT_BLK), 0)
    cc = lax.broadcasted_iota(I32, (ATT_BLK, 2 * ATT_BLK), 1)
    is_cur = cc >= ATT_BLK
    rk = jnp.where(is_cur, cc - ATT_BLK, cc)
    for br, (_, dil) in enumerate(BRANCHES):
        nseg = RES // dil
        seg = ATT_BLK // nseg
        oq = nseg * (rq % seg) + rq // seg
        ok = nseg * (rk % seg) + rk // seg
        allowed = (is_cur & (ok <= oq)) | (jnp.logical_not(is_cur) & (ok >= oq))
        bias_sc[2 * br + 1] = jnp.where(allowed, 0.0, NEG)
        bias_sc[2 * br] = jnp.where(allowed & is_cur, 0.0, NEG)

    def run_block(br, rows, qoff, poff, has_prev):
        seg = ATT_BLK // len(rows)
        q = jnp.concatenate([qr[r, pl.ds(qoff, seg), :] for r in rows], axis=0)
        kc = jnp.concatenate([kr[r, pl.ds(qoff, seg), :] for r in rows], axis=0)
        vc = jnp.concatenate([vr[r, pl.ds(qoff, seg), :] for r in rows], axis=0)
        kp = jnp.concatenate([kr[r, pl.ds(poff, seg), :] for r in rows], axis=0)
        vp = jnp.concatenate([vr[r, pl.ds(poff, seg), :] for r in rows], axis=0)
        return q, kp, kc, vp, vc

    def super_block(u, carry):
        base = pl.multiple_of(u * ATT_BLK, ATT_BLK)
        m_sc[...] = jnp.full(m_sc.shape, NEG, F32)
        l_sc[...] = jnp.zeros(l_sc.shape, F32)
        acc_sc[...] = jnp.zeros(acc_sc.shape, F32)

        def branch_steps(br, specs):
            loaded = []
            for rows, loc, seg in specs:
                qoff = pl.multiple_of(base + loc, 8)
                poff = pl.multiple_of(jnp.maximum(base + loc - seg, 0), 8)
                has_prev = (base + loc >= seg).astype(I32)
                q, kp, kc, vp, vc = run_block(br, rows, qoff, poff, has_prev)
                bias = bias_sc[2 * br + has_prev]
                sl = pl.ds(pl.multiple_of(loc, 8), seg)
                mo = jnp.concatenate([m_sc[h, r, sl, :] for h in range(2) for r in rows], axis=0)
                lo = jnp.concatenate([l_sc[r, sl, :] for r in rows], axis=0)
                ao = jnp.concatenate([acc_sc[r, sl, :] for r in rows], axis=0)
                loaded.append((q, kp, kc, vp, vc, mo, lo, ao, bias))
            results = _attn_blocks(loaded)
            for (rows, loc, seg), (mn, ln, an) in zip(specs, results):
                sl = pl.ds(pl.multiple_of(loc, 8), seg)
                for h in range(2):
                    for si, r in enumerate(rows):
                        lo_r = h * ATT_BLK + si * seg
                        m_sc[h, r, sl, :] = mn[lo_r:lo_r + seg]
                for si, r in enumerate(rows):
                    acc_sc[r, sl, :] = an[si * seg:(si + 1) * seg]
                    l_sc[r, sl, :] = ln[si * seg:(si + 1) * seg]

        nb = ATT_BLOCKS_PER_STEP

        def b1(n, c):
            branch_steps(0, [(list(range(RES)), (n * nb + uu) * 8, 8) for uu in range(nb)])
            return c
        lax.fori_loop(0, RES // nb, b1, 0)

        def b4(t, c):
            specs = []
            for uu in range(nb):
                tt = t * nb + uu
                specs.append(([tt // 4 + 4 * m for m in range(4)], (tt % 4) * 32, 32))
            branch_steps(1, specs)
            return c
        lax.fori_loop(0, RES // nb, b4, 0)

        def b16(j, c):
            branch_steps(2, [([j * nb + uu], 0, ATT_BLK) for uu in range(nb)])
            return c
        lax.fori_loop(0, RES // nb, b16, 0)

        for j in range(RES):
            den = pltpu.roll(l_sc[j], HD_ATT, axis=1)
            o_ref[pl.ds(u * SUPER + j, ATT_BLK, stride=RES), :] = acc_sc[j] / den
        return carry

    lax.fori_loop(0, seq // SUPER, super_block, 0)


def _attn_prompt(proj, batch, seq, win, col_q, col_k, col_v, n_pairs):
    ni = seq // RES
    body = functools.partial(_attn_prompt_body, seq=seq)

    def spec(col):
        return pl.BlockSpec((seq, LANES), lambda b, p: (b, col + p))

    wspec = pl.BlockSpec((win, LANES), lambda b, p: (b, p))
    wshape = jax.ShapeDtypeStruct((batch * win, n_pairs * LANES), F32)
    return pl.pallas_call(
        body,
        grid=(batch, n_pairs),
        in_specs=[spec(col_q), spec(col_k), spec(col_v)],
        out_specs=[pl.BlockSpec((seq, LANES), lambda b, p: (b, p)), wspec, wspec],
        out_shape=[jax.ShapeDtypeStruct((batch * seq, n_pairs * LANES), F32), wshape, wshape],
        scratch_shapes=[
            pltpu.VMEM((RES, ni, LANES), F32),
            pltpu.VMEM((RES, ni, LANES), F32),
            pltpu.VMEM((RES, ni, LANES), F32),
            pltpu.VMEM((2, RES, ATT_BLK, 1), F32),
            pltpu.VMEM((RES, ATT_BLK, LANES), F32),
            pltpu.VMEM((RES, ATT_BLK, LANES), F32),
            pltpu.VMEM((2 * len(BRANCHES), ATT_BLK, 2 * ATT_BLK), F32),
        ],
        compiler_params=_cparams(("parallel", "parallel"), 60),
        name="attn_prompt",
    )(proj, proj, proj)


def _sample_key_rows(nbuf, dq):
    tail = max(w for w, dil in BRANCHES if dil < RES)
    assert nbuf % RES == 0 and tail % RES == 0 and tail <= nbuf and dq <= RES
    return (nbuf - tail) // RES, nbuf - tail


def _attn_sample_body(q_ref, kn_ref, vn_ref, kc_ref, vc_ref, o_ref, *, dq, nbuf, n_pairs):
    b = pl.program_id(0)
    assert 8 % dq == 0
    groups, tail_start = _sample_key_rows(nbuf, dq)
    n_str = groups * 8
    n_tail = nbuf - tail_start

    def strided(ref, cols):
        return ref[0, 0:tail_start, cols].reshape(groups, RES, LANES)[:, :8, :].reshape(n_str, LANES)
    lane = lax.broadcasted_iota(I32, (dq, LANES), 1)
    h0 = lane < HD_ATT
    zpad = jnp.zeros((8 - dq, LANES), F32)

    def counts(delta):
        cnt = jnp.zeros(delta.shape, F32)
        for win, dil in BRANCHES:
            hit = (delta >= 0) & (delta <= win) & (delta % dil == 0)
            cnt = cnt + hit.astype(F32)
        return cnt

    def key_counts(n, pos_of_col):
        col = lax.broadcasted_iota(I32, (2 * dq, n), 1)
        qi = lax.broadcasted_iota(I32, (2 * dq, n), 0) % dq
        return counts(nbuf + qi - pos_of_col(col))

    cnt_s = key_counts(n_str, lambda c: RES * (c // 8) + c % 8)
    cnt_t = key_counts(n_tail, lambda c: tail_start + c)
    qi_n = lax.broadcasted_iota(I32, (2 * dq, 8), 0) % dq
    col_n = lax.broadcasted_iota(I32, (2 * dq, 8), 1)
    cnt_n = jnp.where(col_n < dq, counts(qi_n - col_n), 0.0)

    row8 = pl.multiple_of((b * dq) // 8 * 8, 8)
    sub = (b * dq) % 8 // dq

    def new_rows(ref, cols):
        x8 = ref[pl.ds(row8, 8), cols]
        out = x8[0:dq]
        for j in range(1, 8 // dq):
            out = jnp.where(sub == j, x8[j * dq:(j + 1) * dq], out)
        return out

    for p in range(n_pairs):
        cols = slice(p * LANES, (p + 1) * LANES)
        q = new_rows(q_ref, cols) * (HD_ATT ** -0.5)
        qs = jnp.concatenate([jnp.where(h0, q, 0.0), jnp.where(h0, 0.0, q)], axis=0).astype(BF16)
        kn = jnp.concatenate([new_rows(kn_ref, cols), zpad], axis=0).astype(BF16)
        vn = jnp.concatenate([new_rows(vn_ref, cols), zpad], axis=0).astype(BF16)
        pieces = [
            (strided(kc_ref, cols), strided(vc_ref, cols), cnt_s),
            (kc_ref[0, tail_start:, cols], vc_ref[0, tail_start:, cols], cnt_t),
            (kn, vn, cnt_n),
        ]
        s = [jnp.where(cnt > 0, _dot_nt(qs, k.astype(BF16)), NEG) for k, _, cnt in pieces]
        m = functools.reduce(jnp.maximum, [jnp.max(si, axis=1, keepdims=True) for si in s])
        e = [cnt * jnp.exp(si - m) for si, (_, _, cnt) in zip(s, pieces)]
        den = sum(jnp.sum(ei, axis=1, keepdims=True) for ei in e)
        num = sum(_dot(ei.astype(BF16), v.astype(BF16)) for ei, (_, v, _) in zip(e, pieces))
        out = num / den
        o_ref[0, :, cols] = jnp.where(h0, out[:dq], out[dq:])


def _attn_sample(proj, cache_k, cache_v, col_q, col_k, col_v, n_pairs, dec_batch, dq):
    nbuf = cache_k.shape[1]
    width = n_pairs * LANES
    assert col_q % n_pairs == 0 and col_k % n_pairs == 0 and col_v % n_pairs == 0
    body = functools.partial(_attn_sample_body, dq=dq, nbuf=nbuf, n_pairs=n_pairs)
    nrows = dec_batch * dq

    def pspec(col):
        return pl.BlockSpec((nrows, width), lambda b: (0, col // n_pairs))

    cspec = pl.BlockSpec((1, nbuf, width), lambda b: (b, 0, 0))
    return pl.pallas_call(
        body,
        grid=(dec_batch,),
        in_specs=[pspec(col_q), pspec(col_k), pspec(col_v), cspec, cspec],
        out_specs=pl.BlockSpec((1, dq, width), lambda b: (b, 0, 0)),
        out_shape=jax.ShapeDtypeStruct((dec_batch, dq, width), F32),
        compiler_params=_cparams(("parallel",), 48),
        name="attn_sample",
    )(proj, proj, proj, cache_k, cache_v)


def _ssd_body(xbc_ref, z_ref, dt_ref, cprev_ref, h0_ref, cw_ref, cb_ref, dtb_ref, alog_ref, dskx_ref,
              nw_ref, e_ref, r_ref, y_ref, cnew_ref, hout_ref, h_sc, xf_sc, *, nvalid, d_ssm, n_heads):
    L = CHUNK
    c = pl.program_id(1)
    gw = d_ssm // SSM_GROUPS
    hpg = n_heads // SSM_GROUPS

    @pl.when(c == 0)
    def _():
        h_sc[...] = h0_ref[0]
        xf_sc[0:8] = cprev_ref[0]

    if nvalid < L:
        xf_sc[8:8 + L] = jnp.zeros((L, xf_sc.shape[1]), F32)
        xf_sc[8:8 + nvalid] = xbc_ref[0]
        zpad = jnp.zeros((L - nvalid, LANES), F32)
        dt_raw = jnp.concatenate([dt_ref[0], zpad], axis=0)
    else:
        xf_sc[8:8 + L] = xbc_ref[...]
        dt_raw = dt_ref[...]
    xa = xf_sc[0:L + 8]
    conv = cb_ref[...] + cw_ref[CONV_W - 1:CONV_W, :] * xa[8:]
    for w in range(CONV_W - 1):
        conv = conv + cw_ref[w:w + 1, :] * pltpu.roll(xa, L + 8 - (5 + w), axis=0)[0:L]
    cnew_ref[0] = xf_sc[nvalid:nvalid + 8]
    xf_sc[0:8] = xf_sc[L:L + 8]

    row = lax.broadcasted_iota(I32, (L, LANES), 0)
    lane = lax.broadcasted_iota(I32, (L, LANES), 1)
    if nvalid < L:
        rowc = lax.broadcasted_iota(I32, conv.shape, 0)
        conv = jnp.where(rowc < nvalid, conv, 0.0)
    act = conv * jax.nn.sigmoid(conv)
    xs = act[:, :d_ssm]
    bm = act[:, d_ssm:d_ssm + SSM_GROUPS * D_STATE].astype(BF16)
    cm = act[:, d_ssm + SSM_GROUPS * D_STATE:].astype(BF16)

    dt = jax.nn.softplus(dt_raw + dtb_ref[...])
    live = lane < n_heads
    if nvalid < L:
        live = live & (row < nvalid)
    dt = jnp.where(live, dt, 0.0)
    dta = dt * (-jnp.exp(alog_ref[...]))
    tril = row >= lane
    acum = jnp.dot(tril.astype(F32), dta, precision=lax.Precision.HIGHEST, preferred_element_type=F32)
    last = acum[L - 1:L, :]
    acum_t = acum.T
    eb = e_ref[...]
    stacked = jnp.concatenate([jnp.exp(last - acum) * dt, jnp.exp(acum), dt], axis=0)
    ex = _split_dot(stacked, eb)
    xw = (xs * ex[0:L]).astype(BF16)
    eac_x = ex[L:2 * L]
    xd = xs * ex[2 * L:3 * L]

    cd_col = jnp.broadcast_to(jnp.exp(acum_t[:, L - 1:L]), (LANES, LANES))
    cd_rows = _split_dot_left(r_ref[...], cd_col)

    h0l = lane < SSM_HD
    y_parts = []
    yoff_parts = []
    for g in range(SSM_GROUPS):
        bg = bm[:, g * D_STATE:(g + 1) * D_STATE]
        cg = cm[:, g * D_STATE:(g + 1) * D_STATE]
        cbg = _dot_nt(cg, bg)
        rows = slice(g * gw, (g + 1) * gw)
        hg = h_sc[rows, :]
        yoff_parts.append(_dot_nt(cg, hg.astype(BF16)))
        st = lax.dot_general(xw[:, rows], bg, (((0,), (0,)), ((), ())), preferred_element_type=F32)
        h_sc[rows, :] = cd_rows[rows, :] * hg + st
        for jp in range(hpg // 2):
            ms = []
            for hh in range(2):
                h = g * hpg + 2 * jp + hh
                diff = acum[:, h:h + 1] - acum_t[h:h + 1, :]
                ms.append(cbg * jnp.exp(jnp.where(tril, diff, NEG)))
            lhs = jnp.concatenate(ms, axis=1).astype(BF16)
            col = (g * hpg + 2 * jp) * SSM_HD
            xp = xd[:, col:col + LANES]
            rhs = jnp.concatenate([jnp.where(h0l, xp, 0.0), jnp.where(h0l, 0.0, xp)], axis=0).astype(BF16)
            y_parts.append(_dot(lhs, rhs))
    y = jnp.concatenate(y_parts, axis=1) + jnp.concatenate(yoff_parts, axis=1) * eac_x + dskx_ref[...] * xs
    if nvalid < L:
        y = y[:8]
        z = jnp.concatenate([z_ref[0], jnp.zeros((8 - nvalid, d_ssm), F32)], axis=0)
    else:
        z = z_ref[...]
    y = y * (z * jax.nn.sigmoid(z))
    outs = []
    for g in range(SSM_GROUPS):
        yg = y[:, g * gw:(g + 1) * gw]
        outs.append(yg * lax.rsqrt(jnp.mean(yg * yg, axis=-1, keepdims=True) + EPS))
    y = jnp.concatenate(outs, axis=1) * nw_ref[...]
    if nvalid < L:
        y_ref[0] = y[:nvalid].astype(y_ref.dtype)
    else:
        y_ref[...] = y.astype(y_ref.dtype)
    hout_ref[0] = h_sc[...]


def _split_dot_left(onehot_bf16, v):
    p1 = v.astype(BF16)
    r1 = v - p1.astype(F32)
    p2 = r1.astype(BF16)
    p3 = (r1 - p2.astype(F32)).astype(BF16)
    return _dot(onehot_bf16, p1) + _dot(onehot_bf16, p2) + _dot(onehot_bf16, p3)


def _ssd(xbc_src, z_src, dt_src, conv_prev8, h0, consts, *, batch, n_chunks, nvalid, xbc_spec, z_spec, dt_spec,
         d_ssm, n_heads):
    conv_dim = conv_prev8.shape[-1]
    hp = h0.shape[1]
    body = functools.partial(_ssd_body, nvalid=nvalid, d_ssm=d_ssm, n_heads=n_heads)
    cw, cb, dtb, alog, dskx, nw, e_mat, r_mat = consts

    def const(a):
        return pl.BlockSpec(a.shape, lambda b, c: (0,) * a.ndim)

    if nvalid < CHUNK:
        assert n_chunks == 1 and nvalid <= 8
        y_spec = pl.BlockSpec((1, nvalid, d_ssm), lambda b, c: (b, 0, 0))
        y_shape = jax.ShapeDtypeStruct((batch, nvalid, d_ssm), BF16)
    else:
        y_spec = pl.BlockSpec((CHUNK, d_ssm), lambda b, c: (b * n_chunks + c, 0))
        y_shape = jax.ShapeDtypeStruct((batch * n_chunks * CHUNK, d_ssm), BF16)
    return pl.pallas_call(
        body,
        grid=(batch, n_chunks),
        in_specs=[
            xbc_spec, z_spec, dt_spec,
            pl.BlockSpec((1, 8, conv_dim), lambda b, c: (b, 0, 0)),
            pl.BlockSpec((1, hp, D_STATE), lambda b, c: (b, 0, 0)),
            const(cw), const(cb), const(dtb), const(alog), const(dskx), const(nw), const(e_mat), const(r_mat),
        ],
        out_specs=[
            y_spec,
            pl.BlockSpec((1, 8, conv_dim), lambda b, c: (b, 0, 0)),
            pl.BlockSpec((1, hp, D_STATE), lambda b, c: (b, 0, 0)),
        ],
        out_shape=[
            y_shape,
            jax.ShapeDtypeStruct((batch, 8, conv_dim), F32),
            jax.ShapeDtypeStruct((batch, hp, D_STATE), F32),
        ],
        scratch_shapes=[pltpu.VMEM((hp, D_STATE), F32), pltpu.VMEM((CHUNK + 8, conv_dim), F32)],
        compiler_params=_cparams(("parallel", "arbitrary"), 48),
        name="ssd",
    )(xbc_src, z_src, dt_src, conv_prev8, h0, cw, cb, dtb, alog, dskx, nw, e_mat, r_mat)


def _outproj_body(x_ref, ya_ref, ys_ref, wa_ref, ws_ref, nfw_ref, wr_ref, br_ref,
                  xmid_ref, xpk_ref, idx_ref, gate_ref):
    tm, d = x_ref.shape
    pr = tm // OUTPROJ_PARTS
    parts = range(OUTPROJ_PARTS)
    rs = [pl.ds(i * pr, pr) for i in parts]
    mix = [_dot(ya_ref[r, :].astype(BF16), wa_ref[...]) + _dot(ys_ref[r, :], ws_ref[...]) for r in rs]
    xm = [x_ref[rs[i], :] + mix[i] for i in parts]
    for i in parts:
        xmid_ref[rs[i], :] = xm[i]
    xn = [xm[i] * lax.rsqrt(jnp.mean(xm[i] * xm[i], axis=-1, keepdims=True) + EPS) * nfw_ref[...] for i in parts]
    xh = [xn[i].astype(BF16) for i in parts]
    xhf = [xh[i].astype(F32) for i in parts]
    for i in parts:
        _store_packed(xpk_ref, pr, _pack_words(xh[i][:, :d // 2], xh[i][:, d // 2:]), row0=i * pr)
    xl = [(xn[i] - xhf[i]).astype(BF16) for i in parts]
    hl = [_dot(xh[i], wr_ref[...]) for i in parts]
    lh = [_dot(xl[i], wr_ref[:, :LANES]) for i in parts]
    logits = [hl[i][:, :LANES] + hl[i][:, LANES:] + lh[i] + br_ref[...] for i in parts]
    lane = lax.broadcasted_iota(I32, (pr, LANES), 1)
    idx_out = [jnp.zeros((pr, LANES), I32) for _ in parts]
    val_out = [jnp.full((pr, LANES), NEG, F32) for _ in parts]
    for k in range(TOP_K):
        mx = [jnp.max(logits[i], axis=1, keepdims=True) for i in parts]
        ix = [jnp.min(jnp.where(logits[i] == mx[i], lane, LANES), axis=1, keepdims=True) for i in parts]
        idx_out = [jnp.where(lane == k, ix[i], idx_out[i]) for i in parts]
        val_out = [jnp.where(lane == k, mx[i], val_out[i]) for i in parts]
        logits = [jnp.where(lane == ix[i], -jnp.inf, logits[i]) for i in parts]
    for i in parts:
        ev = jnp.exp(val_out[i] - jnp.max(val_out[i], axis=1, keepdims=True))
        gate_ref[rs[i], :] = ev / jnp.sum(ev, axis=1, keepdims=True)
        idx_ref[rs[i], :] = idx_out[i]


def _outproj(x_all, y_att, y_ssd, w_att, w_ssd, nfw, w_router, b_router):
    m, d = x_all.shape
    assert d == 2 * PACK_ROWS * LANES
    tm = _row_tile(m, TM_PROJ)

    def rows(width):
        return pl.BlockSpec((tm, width), lambda i: (i, 0))

    def const(a):
        return pl.BlockSpec(a.shape, lambda i: (0,) * a.ndim)

    return pl.pallas_call(
        _outproj_body,
        grid=(m // tm,),
        in_specs=[rows(d), rows(y_att.shape[1]), rows(y_ssd.shape[1]), const(w_att), const(w_ssd), const(nfw),
                  const(w_router), const(b_router)],
        out_specs=[rows(d), pl.BlockSpec((tm * PACK_ROWS, LANES), lambda i: (i, 0)), rows(LANES), rows(LANES)],
        out_shape=[jax.ShapeDtypeStruct((m, d), F32), jax.ShapeDtypeStruct((m * PACK_ROWS, LANES), U32),
                   jax.ShapeDtypeStruct((m, LANES), I32), jax.ShapeDtypeStruct((m, LANES), F32)],
        compiler_params=_cparams(("parallel",), 56),
        name="outproj",
    )(x_all, y_att, y_ssd, w_att, w_ssd, nfw, w_router, b_router)


def _route_body(idx_ref, cin_ref, rank_ref, cout_ref, carry):
    @pl.when(pl.program_id(0) == 0)
    def _():
        carry[...] = cin_ref[...]

    tr = idx_ref.shape[0]
    lane = lax.broadcasted_iota(I32, (tr, LANES), 1)
    tril = (lax.broadcasted_iota(I32, (tr, tr), 0) >= lax.broadcasted_iota(I32, (tr, tr), 1))
    tril = jnp.where(tril, 1.0, 0.0).astype(BF16)
    idx = idx_ref[...]
    cnt = carry[...]
    out = jnp.zeros((tr, LANES), I32)
    for k in range(TOP_K):
        hit = lane == idx[:, k:k + 1]
        pre = _dot(tril, jnp.where(hit, 1.0, 0.0).astype(BF16))
        rk = jnp.sum(jnp.where(hit, pre + cnt - 1.0, 0.0), axis=1, keepdims=True)
        out = jnp.where(lane == k, rk.astype(I32), out)
        cnt = cnt + pre[tr - 1:tr, :]
    carry[...] = cnt
    rank_ref[...] = out
    cout_ref[...] = cnt


def _route(top_idx, counts_in):
    m = top_idx.shape[0]
    tr = _row_tile(m, TM_ROUTE)
    return pl.pallas_call(
        _route_body,
        grid=(m // tr,),
        in_specs=[pl.BlockSpec((tr, LANES), lambda i: (i, 0)), pl.BlockSpec((1, LANES), lambda i: (0, 0))],
        out_specs=[pl.BlockSpec((tr, LANES), lambda i: (i, 0)), pl.BlockSpec((1, LANES), lambda i: (0, 0))],
        out_shape=[jax.ShapeDtypeStruct((m, LANES), I32), jax.ShapeDtypeStruct((1, LANES), F32)],
        scratch_shapes=[pltpu.VMEM((1, LANES), F32)],
        compiler_params=_cparams(("arbitrary",), 32),
        name="route",
    )(top_idx, counts_in)


def _dispatch_body(pos_ref, x_ref, xd_in, xd_hbm, sem):
    del xd_in
    i = pl.program_id(0)
    ts = x_ref.shape[0] // PACK_ROWS

    def row_copy(r, k):
        p = pos_ref[(i * ts + r) * TOP_K + k]
        src = x_ref.at[pl.ds(pl.multiple_of(r * PACK_ROWS, PACK_ROWS), PACK_ROWS), :]
        dst = xd_hbm.at[pl.ds(pl.multiple_of(p * PACK_ROWS, PACK_ROWS), PACK_ROWS), :]
        return pltpu.make_async_copy(src, dst, sem)

    def issue(r, c):
        for k in range(TOP_K):
            row_copy(r, k).start()
        return c
    lax.fori_loop(0, ts, issue, 0)

    def drain(r, c):
        for k in range(TOP_K):
            row_copy(r, k).wait()
        return c
    lax.fori_loop(0, ts, drain, 0)


def _dispatch(pos_flat, x_packed, x_disp):
    m = x_packed.shape[0] // PACK_ROWS
    ts = _row_tile(m, TM_DISPATCH)
    return pl.pallas_call(
        _dispatch_body,
        grid_spec=pltpu.PrefetchScalarGridSpec(
            num_scalar_prefetch=1,
            grid=(m // ts,),
            in_specs=[pl.BlockSpec((ts * PACK_ROWS, LANES), lambda i, pos: (i, 0)),
                      pl.BlockSpec(memory_space=pl.ANY)],
            out_specs=pl.BlockSpec(memory_space=pl.ANY),
            scratch_shapes=[pltpu.SemaphoreType.DMA(())],
        ),
        out_shape=jax.ShapeDtypeStruct(x_disp.shape, x_disp.dtype),
        input_output_aliases={2: 0},
        compiler_params=_cparams(("arbitrary",), 32),
        name="dispatch",
    )(pos_flat, x_packed, x_disp)


def _tile_changed(te_ref, i):
    return (i == 0) | (te_ref[i] != te_ref[jnp.maximum(i - 1, 0)])


def _gmm1_body(te_ref, nu_ref, x_ref, wg_ref, wu_ref, bg_ref, bu_ref, h_ref, wgb, wub):
    i = pl.program_id(1)

    @pl.when(_tile_changed(te_ref, i))
    def _():
        wgb[...] = wg_ref[0].astype(BF16)
        wub[...] = wu_ref[0].astype(BF16)

    @pl.when(i < nu_ref[0])
    def _():
        tm = h_ref.shape[0]
        x = _load_packed(lambda s: x_ref[pl.ds(s, tm, stride=PACK_ROWS), :], BF16)
        gt = _dot(x, wgb[...]) + bg_ref[0]
        up = _dot(x, wub[...]) + bu_ref[0]
        gt = jnp.minimum(gt, SWIGLU_LIMIT)
        up = jnp.clip(up, -SWIGLU_LIMIT, SWIGLU_LIMIT)
        act = (up + 1.0) * gt * jax.nn.sigmoid(SWIGLU_ALPHA * gt)
        h_ref[...] = act.astype(h_ref.dtype)

    @pl.when(i >= nu_ref[0])
    def _():
        h_ref[...] = jnp.zeros(h_ref.shape, h_ref.dtype)


def _gmm2_body(te_ref, nu_ref, h_ref, wd_ref, bd_ref, y_ref, wdb):
    i = pl.program_id(0)

    @pl.when(_tile_changed(te_ref, i))
    def _():
        wdb[...] = wd_ref[0].astype(BF16)

    @pl.when(i < nu_ref[0])
    def _():
        tm, half = h_ref.shape[0], wdb.shape[1] // 2
        h = h_ref[...]
        lo = _dot(h, wdb[:, :half]) + bd_ref[0][:, :half]
        hi = _dot(h, wdb[:, half:]) + bd_ref[0][:, half:]
        _store_packed(y_ref, tm, _pack_words(lo.astype(BF16), hi.astype(BF16)))

    @pl.when(i >= nu_ref[0])
    def _():
        y_ref[...] = jnp.zeros(y_ref.shape, y_ref.dtype)


def _used_tile(i, nu):
    return jnp.minimum(i, nu[0] - 1)


def _gmm1(tile_exp, n_used, x_disp, w_gate, w_up, b_gate, b_up):
    cap = x_disp.shape[0] // PACK_ROWS
    d, dff = w_gate.shape[1], w_gate.shape[2]
    tm, tn = TM_GMM, TN_GMM
    x_spec = pl.BlockSpec((tm * PACK_ROWS, LANES), lambda j, i, te, nu: (_used_tile(i, nu), 0))
    w_spec = pl.BlockSpec((1, d, tn), lambda j, i, te, nu: (te[_used_tile(i, nu)], 0, j))
    b_spec = pl.BlockSpec((1, 1, tn), lambda j, i, te, nu: (te[_used_tile(i, nu)], 0, j))
    return pl.pallas_call(
        _gmm1_body,
        grid_spec=pltpu.PrefetchScalarGridSpec(
            num_scalar_prefetch=2,
            grid=(dff // tn, cap // tm),
            in_specs=[x_spec, w_spec, w_spec, b_spec, b_spec],
            out_specs=pl.BlockSpec((tm, tn), lambda j, i, te, nu: (i, j)),
            scratch_shapes=[pltpu.VMEM((d, tn), BF16), pltpu.VMEM((d, tn), BF16)],
        ),
        out_shape=jax.ShapeDtypeStruct((cap, dff), BF16),
        compiler_params=_cparams(("arbitrary", "arbitrary"), 60),
        name="gmm1",
    )(tile_exp, n_used, x_disp, w_gate, w_up, b_gate, b_up)


def _gmm2(tile_exp, n_used, h_disp, w_down, b_down):
    cap, dff = h_disp.shape
    d = w_down.shape[2]
    assert d == 2 * PACK_ROWS * LANES
    tm = TM_GMM
    return pl.pallas_call(
        _gmm2_body,
        grid_spec=pltpu.PrefetchScalarGridSpec(
            num_scalar_prefetch=2,
            grid=(cap // tm,),
            in_specs=[
                pl.BlockSpec((tm, dff), lambda i, te, nu: (_used_tile(i, nu), 0)),
                pl.BlockSpec((1, dff, d), lambda i, te, nu: (te[_used_tile(i, nu)], 0, 0)),
                pl.BlockSpec((1, 1, d), lambda i, te, nu: (te[_used_tile(i, nu)], 0, 0)),
            ],
            out_specs=pl.BlockSpec((tm * PACK_ROWS, LANES), lambda i, te, nu: (i, 0)),
            scratch_shapes=[pltpu.VMEM((dff, d), BF16)],
        ),
        out_shape=jax.ShapeDtypeStruct((cap * PACK_ROWS, LANES), U32),
        compiler_params=_cparams(("arbitrary",), 60),
        name="gmm2",
    )(tile_exp, n_used, h_disp, w_down, b_down)


def _combine_body(pos_ref, y_hbm, xmid_ref, gate_ref, fw_ref, o_ref, buf, sem):
    i = pl.program_id(0)
    n = pl.num_programs(0)
    tc = TOK_COMBINE

    def row_copy(step, slot, r, k):
        p = pos_ref[(step * tc + r) * TOP_K + k]
        src = y_hbm.at[pl.ds(pl.multiple_of(p * PACK_ROWS, PACK_ROWS), PACK_ROWS), :]
        dst = buf.at[slot, k, pl.ds(pl.multiple_of(r * PACK_ROWS, PACK_ROWS), PACK_ROWS), :]
        return pltpu.make_async_copy(src, dst, sem.at[slot])

    def issue(step, slot):
        def body(r, c):
            for k in range(TOP_K):
                row_copy(step, slot, r, k).start()
            return c
        lax.fori_loop(0, tc, body, 0)

    @pl.when(i == 0)
    def _():
        issue(0, 0)

    @pl.when(i + 1 < n)
    def _():
        issue(i + 1, (i + 1) % 2)

    slot = i % 2

    def wait_body(r, c):
        for k in range(TOP_K):
            row_copy(i, slot, r, k).wait()
        return c
    lax.fori_loop(0, tc, wait_body, 0)

    gates = gate_ref[...]
    acc = xmid_ref[...]
    for k in range(TOP_K):
        y_k = _load_packed(lambda s, k=k: buf[slot, k, pl.ds(s, tc, stride=PACK_ROWS), :], F32)
        acc = acc + gates[:, k:k + 1] * y_k
    o_ref[...] = acc * lax.rsqrt(jnp.mean(acc * acc, axis=-1, keepdims=True) + EPS) * fw_ref[...]


def _combine(pos_flat, y_disp, x_mid, gates, final_w):
    m, d = x_mid.shape
    tc = TOK_COMBINE
    return pl.pallas_call(
        _combine_body,
        grid_spec=pltpu.PrefetchScalarGridSpec(
            num_scalar_prefetch=1,
            grid=(m // tc,),
            in_specs=[
                pl.BlockSpec(memory_space=pl.ANY),
                pl.BlockSpec((tc, d), lambda i, pos: (i, 0)),
                pl.BlockSpec((tc, LANES), lambda i, pos: (i, 0)),
                pl.BlockSpec((1, d), lambda i, pos: (0, 0)),
            ],
            out_specs=pl.BlockSpec((tc, d), lambda i, pos: (i, 0)),
            scratch_shapes=[pltpu.VMEM((2, TOP_K, tc * PACK_ROWS, LANES), U32), pltpu.SemaphoreType.DMA((2,))],
        ),
        out_shape=jax.ShapeDtypeStruct((m, d), F32),
        compiler_params=_cparams(("arbitrary",), 32),
        name="combine",
    )(pos_flat, y_disp, x_mid, gates, final_w)


def _tile_layout(counts, n_assign, n_exp):
    tiles = (counts + TM_GMM - 1) // TM_GMM
    tile_end = jnp.cumsum(tiles)
    row_start = (tile_end - tiles) * TM_GMM
    n_tiles = -(-n_assign // TM_GMM) + n_exp
    n_used = tile_end[-1]
    tile_ids = jnp.minimum(jnp.arange(n_tiles, dtype=I32), n_used - 1)
    tile_exp = jnp.minimum(jnp.sum((tile_end[None, :] <= tile_ids[:, None]).astype(I32), axis=1), n_exp - 1)
    return row_start.astype(I32), tile_exp.astype(I32), n_used.reshape(1).astype(I32), n_tiles * TM_GMM


def _positions(top_idx, rank, row_start, n_exp):
    idx = top_idx[:, :TOP_K]
    onehot = idx[:, :, None] == jnp.arange(n_exp, dtype=I32)[None, None, :]
    base = jnp.sum(jnp.where(onehot, row_start[None, None, :], 0), axis=-1)
    return (base + rank[:, :TOP_K]).reshape(-1).astype(I32)


def kernel(x_prompt, x_sample, cache_attn_k, cache_attn_v, state_conv, state_ssm, norm_mix_w, w_in, conv_w, conv_b, dt_bias, a_log, d_skip, ssd_norm_w, w_out, norm_ffn_w, w_router, b_router, w_gate, b_gate, w_up, b_up, w_down, b_down, norm_final_w):
    depth = norm_mix_w.shape[0]
    assert depth == 1, "single-layer step"
    bp, seq, d = x_prompt.shape
    bs, dq, _ = x_sample.shape
    n_att_heads = cache_attn_k.shape[3]
    d_att = n_att_heads * HD_ATT
    n_pairs = d_att // LANES
    n_heads = a_log.shape[1]
    d_ssm = n_heads * SSM_HD
    conv_dim = conv_w.shape[2]
    n_exp = w_router.shape[2]
    n_prompt = bp * seq
    n_sample = bs * dq
    assert n_sample == CHUNK and seq % SUPER == 0 and n_heads <= LANES

    wl = w_in[0]
    o_q, o_k, o_v, o_z, o_x, o_dt = 0, d_att, 2 * d_att, 3 * d_att, 3 * d_att + d_ssm, 3 * d_att + d_ssm + conv_dim
    w_main = jnp.concatenate([wl[:, o_x:o_dt], wl[:, o_q:o_z], wl[:, o_z:o_x]], axis=1).astype(BF16)
    w_dt = jnp.pad(wl[:, o_dt:], ((0, 0), (0, LANES - n_heads))).astype(BF16)
    col_q = conv_dim // LANES
    col_k = col_q + n_pairs
    col_v = col_k + n_pairs
    w_att = w_out[0, :d_att].astype(BF16)
    w_ssd = w_out[0, d_att:].astype(BF16)
    w_rp = jnp.pad(w_router[0], ((0, 0), (0, LANES - n_exp)))
    w_rh = w_rp.astype(BF16)
    w_r = jnp.concatenate([w_rh, (w_rp - w_rh.astype(F32)).astype(BF16)], axis=1)
    b_r = jnp.pad(b_router[0], (0, LANES - n_exp), constant_values=NEG).reshape(1, LANES)
    pad_h = (0, LANES - n_heads)
    head_of_col = jnp.arange(d_ssm, dtype=I32) // SSM_HD
    e_mat = (jnp.arange(LANES, dtype=I32)[:, None] == head_of_col[None, :]).astype(BF16)
    r_mat = (head_of_col[:, None] == jnp.arange(LANES, dtype=I32)[None, :]).astype(BF16)
    ssd_consts = (
        conv_w[0], conv_b[0].reshape(1, conv_dim),
        jnp.pad(dt_bias[0], pad_h).reshape(1, LANES), jnp.pad(a_log[0], pad_h).reshape(1, LANES),
        jnp.repeat(d_skip[0], SSM_HD).reshape(1, d_ssm), ssd_norm_w[0].reshape(1, d_ssm), e_mat, r_mat,
    )

    xp = x_prompt.reshape(n_prompt, d)
    xs = x_sample.reshape(n_sample, d)
    nmw = norm_mix_w[0].reshape(1, d)
    proj_p, dt_p = _inproj(xp, nmw, w_main, w_dt)
    proj_s, dt_s = _inproj(xs, nmw, w_main, w_dt)
    n_main = proj_p.shape[1]
    z_col = (conv_dim + 3 * d_att) // d_ssm
    assert z_col * d_ssm == conv_dim + 3 * d_att

    win_p = min(BRANCHES[-1][0], seq)
    y_att_p, k_p, v_p = _attn_prompt(proj_p, bp, seq, win_p, col_q, col_k, col_v, n_pairs)
    ck = cache_attn_k[0].reshape(bs, -1, d_att)
    cv = cache_attn_v[0].reshape(bs, -1, d_att)
    y_att_s = _attn_sample(proj_s, ck, cv, col_q, col_k, col_v, n_pairs, bs, dq)

    n_chunks = seq // CHUNK
    hp = n_heads * SSM_HD
    y_ssd_p, cnew_p, h_p = _ssd(
        proj_p, proj_p, dt_p, jnp.zeros((bp, 8, conv_dim), F32), jnp.zeros((bp, hp, D_STATE), F32), ssd_consts,
        batch=bp, n_chunks=n_chunks, nvalid=CHUNK,
        xbc_spec=pl.BlockSpec((CHUNK, conv_dim), lambda b, c: (b * n_chunks + c, 0)),
        z_spec=pl.BlockSpec((CHUNK, d_ssm), lambda b, c: (b * n_chunks + c, z_col)),
        dt_spec=pl.BlockSpec((CHUNK, LANES), lambda b, c: (b * n_chunks + c, 0)),
        d_ssm=d_ssm, n_heads=n_heads)

    proj_s3 = proj_s.reshape(bs, dq, n_main)
    cprev_s = jnp.pad(state_conv[0], ((0, 0), (8 - (CONV_W - 1), 0), (0, 0)))
    y_ssd_s, cnew_s, h_s = _ssd(
        proj_s3, proj_s3, dt_s.reshape(bs, dq, LANES), cprev_s, state_ssm[0].reshape(bs, hp, D_STATE), ssd_consts,
        batch=bs, n_chunks=1, nvalid=dq,
        xbc_spec=pl.BlockSpec((1, dq, conv_dim), lambda b, c: (b, 0, 0)),
        z_spec=pl.BlockSpec((1, dq, d_ssm), lambda b, c: (b, 0, z_col)),
        dt_spec=pl.BlockSpec((1, dq, LANES), lambda b, c: (b, 0, 0)),
        d_ssm=d_ssm, n_heads=n_heads)

    nfw = norm_ffn_w[0].reshape(1, d)
    xmid_p, xpk_p, idx_p, gates_p = _outproj(xp, y_att_p, y_ssd_p, w_att, w_ssd, nfw, w_r, b_r)
    xmid_s, xpk_s, idx_s, gates_s = _outproj(xs, y_att_s.reshape(n_sample, d_att),
                                             y_ssd_s.reshape(n_sample, d_ssm), w_att, w_ssd, nfw, w_r, b_r)

    rank_p, cnt_p = _route(idx_p, jnp.zeros((1, LANES), F32))
    rank_s, cnt_all = _route(idx_s, cnt_p)
    counts = cnt_all[0, :n_exp].astype(I32)
    n_assign = (n_prompt + n_sample) * TOP_K
    row_start, tile_exp, n_used, cap = _tile_layout(counts, n_assign, n_exp)
    pos_p = _positions(idx_p, rank_p, row_start, n_exp)
    pos_s = _positions(idx_s, rank_s, row_start, n_exp)
    x_disp = jnp.zeros((cap * PACK_ROWS, LANES), U32)
    x_disp = _dispatch(pos_p, xpk_p, x_disp)
    x_disp = _dispatch(pos_s, xpk_s, x_disp)
    h_disp = _gmm1(tile_exp, n_used, x_disp, w_gate[0], w_up[0],
                   b_gate[0].reshape(n_exp, 1, -1), b_up[0].reshape(n_exp, 1, -1))
    y_disp = _gmm2(tile_exp, n_used, h_disp, w_down[0], b_down[0].reshape(n_exp, 1, -1))
    fw = norm_final_w.reshape(1, d)
    y_p = _combine(pos_p, y_disp, xmid_p, gates_p, fw)
    y_s = _combine(pos_s, y_disp, xmid_s, gates_s, fw)

    k_p = k_p.reshape(1, bp, win_p, n_att_heads, HD_ATT)
    v_p = v_p.reshape(1, bp, win_p, n_att_heads, HD_ATT)
    k_s = proj_s3[:, :, col_k * LANES:col_v * LANES].reshape(1, bs, dq, n_att_heads, HD_ATT)
    v_s = proj_s3[:, :, col_v * LANES:col_v * LANES + d_att].reshape(1, bs, dq, n_att_heads, HD_ATT)
    return (
        y_p.reshape(bp, seq, d), y_s.reshape(bs, dq, d), k_p, v_p,
        cnew_p[:, 8 - (CONV_W - 1):][None], h_p.reshape(1, bp, n_heads, SSM_HD, D_STATE),
        k_s, v_s,
        cnew_s[:, 8 - (CONV_W - 1):][None], h_s.reshape(1, bs, n_heads, SSM_HD, D_STATE),
    )
```
